```python
import math
import jax, jax.numpy as jnp
from jax import lax
import numpy as np

D_MODEL = 2048
BATCH = 4
SEQ = 4096
DEPTH = 2

MEM_LEN = 256
D_FF = 5504
FFN_RES_SCALE = 0.5
W_LRU = D_MODEL // 2
LRU_BLOCKS = 8
LRU_BW = W_LRU // LRU_BLOCKS
CONV_W = 4
LRU_C = 8.0
GLA_H = 4
GLA_DK = 128
GLA_DV = 256
GLA_RANK = 16
GLA_NORMALIZER = 16.0
GLA_CHUNK = 64
XA_H = 4
XA_DH = 128
N_BRANCH = 2
EPS = 1e-6

IN_SIZES = (W_LRU, W_LRU, GLA_H * GLA_DK, GLA_H * GLA_DK, GLA_H * GLA_DV, GLA_H * GLA_DV,
            GLA_RANK, N_BRANCH * D_MODEL)
IN_SPLITS = tuple(int(v) for v in np.cumsum(IN_SIZES)[:-1])
N_IN = int(sum(IN_SIZES))
W_BRANCH_IN = W_LRU + GLA_H * GLA_DV

kernel_name = "hybrid_rglru_gla_memxattn_macaron"


def rms_norm(x, g):
    xf = x.astype(jnp.float32)
    y = xf * lax.rsqrt(jnp.mean(xf * xf, axis=-1, keepdims=True) + EPS)
    return (y * g.astype(jnp.float32)).astype(x.dtype)


def swiglu_ffn(xn, w_up, w_down):
    gate, up = jnp.split(xn @ w_up, 2, axis=-1)
    return (jax.nn.silu(gate) * up) @ w_down


def rglru_branch(x_lru, g_lru, conv_w, conv_b, w_a, b_a, w_i, b_i, lam):
    B, S, _ = x_lru.shape
    xc = lax.conv_general_dilated(x_lru, conv_w[:, None, :], window_strides=(1,),
                                  padding=[(CONV_W - 1, 0)],
                                  dimension_numbers=('NWC', 'WIO', 'NWC'),
                                  feature_group_count=W_LRU) + conv_b
    xb = xc.reshape(B, S, LRU_BLOCKS, LRU_BW)
    r = jax.nn.sigmoid((jnp.einsum('bsnc,ncd->bsnd', xb, w_a).reshape(B, S, W_LRU) + b_a).astype(jnp.float32))
    i = jax.nn.sigmoid((jnp.einsum('bsnc,ncd->bsnd', xb, w_i).reshape(B, S, W_LRU) + b_i).astype(jnp.float32))
    log_a = -LRU_C * r * jax.nn.softplus(-lam.astype(jnp.float32))
    a = jnp.exp(log_a)
    u = jnp.sqrt(-jnp.expm1(2.0 * log_a)) * (i * xc.astype(jnp.float32))

    def combine(c1, c2):
        a1, b1 = c1
        a2, b2 = c2
        return a1 * a2, a2 * b1 + b2

    _, h = lax.associative_scan(combine, (a, u), axis=1)
    y = h * jax.nn.gelu(g_lru.astype(jnp.float32))
    return y.astype(x_lru.dtype)


def gla_branch(q, k, v, r, gk, w_gk2, b_gk, norm_g):
    B, S, _ = q.shape
    N = S // GLA_CHUNK
    f32 = jnp.float32
    qc = q.astype(f32).reshape(B, N, GLA_CHUNK, GLA_H, GLA_DK) * (GLA_DK ** -0.5)
    kc = k.astype(f32).reshape(B, N, GLA_CHUNK, GLA_H, GLA_DK)
    vc = v.astype(f32).reshape(B, N, GLA_CHUNK, GLA_H, GLA_DV)
    log_alpha = jax.nn.log_sigmoid((gk @ w_gk2 + b_gk).astype(f32)) / GLA_NORMALIZER
    bcum = jnp.cumsum(log_alpha.reshape(B, N, GLA_CHUNK, GLA_H, GLA_DK), axis=2)
    b_last = bcum[:, :, -1:]
    qe = qc * jnp.exp(bcum)
    ke = kc * jnp.exp(-bcum)
    kd = kc * jnp.exp(b_last - bcum)
    scores = jnp.einsum('bnihd,bnjhd->bnhij', qe, ke)
    causal = jnp.tril(jnp.ones((GLA_CHUNK, GLA_CHUNK), dtype=bool))
    scores = jnp.where(causal, scores, 0.0)
    o_intra = jnp.einsum('bnhij,bnjhv->bnihv', scores, vc)
    upd = jnp.einsum('bnjhd,bnjhv->bnhdv', kd, vc)
    g = jnp.exp(b_last[:, :, 0])

    def step(state, inp):
        g_n, u_n = inp
        return g_n[..., None] * state + u_n, state

    init = jnp.zeros((B, GLA_H, GLA_DK, GLA_DV), f32)
    _, states = lax.scan(step, init, (jnp.moveaxis(g, 1, 0), jnp.moveaxis(upd, 1, 0)))
    states = jnp.moveaxis(states, 0, 1)
    o_inter = jnp.einsum('bnihd,bnhdv->bnihv', qe, states)
    o = (o_intra + o_inter).reshape(B, S, GLA_H, GLA_DV)
    o = o * lax.rsqrt(jnp.mean(o * o, axis=-1, keepdims=True) + EPS) * norm_g.astype(f32)
    o = o.reshape(B, S, GLA_H * GLA_DV) * jax.nn.silu(r.astype(f32))
    return o.astype(q.dtype)


def parallel_mixer(xn, w_in, conv_w, conv_b, lru_w_a, lru_b_a, lru_w_i, lru_b_i, lru_lambda,
                   gla_w_gk2, gla_b_gk, gla_norm_g, b_gate, w_branch, w_out):
    B, S, _ = xn.shape
    x_lru, g_lru, q, k, v, r, gk, gate_logits = jnp.split(xn @ w_in, IN_SPLITS, axis=-1)
    y_a = rglru_branch(x_lru, g_lru, conv_w, conv_b, lru_w_a, lru_b_a, lru_w_i, lru_b_i, lru_lambda)
    y_b = gla_branch(q, k, v, r, gk, gla_w_gk2, gla_b_gk, gla_norm_g)
    z_a = y_a @ w_branch[:W_LRU]
    z_b = y_b @ w_branch[W_LRU:]
    gates = jax.nn.sigmoid((gate_logits.reshape(B, S, N_BRANCH, D_MODEL) + b_gate).astype(jnp.float32))
    merged = gates[:, :, 0] * z_a.astype(jnp.float32) + gates[:, :, 1] * z_b.astype(jnp.float32)
    return merged.astype(xn.dtype) @ w_out


def memory_cross_attention(xn, memn, w_q, w_kv, w_o):
    B, S, _ = xn.shape
    q = (xn @ w_q).reshape(B, S, XA_H, XA_DH)
    kk, vv = jnp.split(memn @ w_kv, 2, axis=-1)
    kk = kk.reshape(B, MEM_LEN, XA_H, XA_DH)
    vv = vv.reshape(B, MEM_LEN, XA_H, XA_DH)
    s = jnp.einsum('bshd,bmhd->bhsm', q, kk).astype(jnp.float32) * (XA_DH ** -0.5)
    p = jax.nn.softmax(s, axis=-1).astype(xn.dtype)
    o = jnp.einsum('bhsm,bmhd->bshd', p, vv).reshape(B, S, XA_H * XA_DH)
    return o @ w_o


def setup_inputs(seed: int = 0) -> dict:
    key = jax.random.key(seed)
    ks = iter(jax.random.split(key, 40))
    L, D = DEPTH, D_MODEL

    def dense(shape, fan_in):
        return jax.random.normal(next(ks), shape, jnp.float32) * (fan_in ** -0.5)

    def gain(shape):
        return 1.0 + 0.01 * jax.random.normal(next(ks), shape, jnp.float32)

    def bias(shape):
        return 0.01 * jax.random.normal(next(ks), shape, jnp.float32)

    x = jax.random.normal(next(ks), (BATCH, SEQ, D), jnp.float32)
    mem = jax.random.normal(next(ks), (BATCH, MEM_LEN, D), jnp.float32)
    u = jax.random.uniform(next(ks), (L, W_LRU), jnp.float32, minval=0.9, maxval=0.999)
    s = u ** (1.0 / LRU_C)
    lru_lambda = jnp.log(s) - jnp.log1p(-s)
    return {
        "x": x, "mem": mem,
        "ffn1_pre_g": gain((L, D)), "ffn1_post_g": gain((L, D)),
        "ffn1_w_up": dense((L, D, 2 * D_FF), D), "ffn1_w_down": dense((L, D_FF, D), D_FF),
        "mix_pre_g": gain((L, D)), "mix_post_g": gain((L, D)),
        "w_in": dense((L, D, N_IN), D),
        "conv_w": dense((L, CONV_W, W_LRU), CONV_W), "conv_b": bias((L, W_LRU)),
        "lru_w_a": dense((L, LRU_BLOCKS, LRU_BW, LRU_BW), LRU_BW), "lru_b_a": bias((L, W_LRU)),
        "lru_w_i": dense((L, LRU_BLOCKS, LRU_BW, LRU_BW), LRU_BW), "lru_b_i": bias((L, W_LRU)),
        "lru_lambda": lru_lambda,
        "gla_w_gk2": dense((L, GLA_RANK, GLA_H * GLA_DK), GLA_RANK), "gla_b_gk": bias((L, GLA_H * GLA_DK)),
        "gla_norm_g": gain((L, GLA_DV)),
        "b_gate": bias((L, N_BRANCH, D)),
        "w_branch": dense((L, W_BRANCH_IN, D), W_LRU), "w_out": dense((L, D, D), D),
        "xa_pre_g": gain((L, D)), "xa_post_g": gain((L, D)), "mem_g": gain((L, D)),
        "xa_w_q": dense((L, D, XA_H * XA_DH), D), "xa_w_kv": dense((L, D, 2 * XA_H * XA_DH), D),
        "xa_w_o": dense((L, XA_H * XA_DH, D), XA_H * XA_DH),
        "ffn2_pre_g": gain((L, D)), "ffn2_post_g": gain((L, D)),
        "ffn2_w_up": dense((L, D, 2 * D_FF), D), "ffn2_w_down": dense((L, D_FF, D), D_FF),
    }


def reference(x, mem, ffn1_pre_g, ffn1_post_g, ffn1_w_up, ffn1_w_down, mix_pre_g, mix_post_g,
              w_in, conv_w, conv_b, lru_w_a, lru_b_a, lru_w_i, lru_b_i, lru_lambda,
              gla_w_gk2, gla_b_gk, gla_norm_g, b_gate, w_branch, w_out,
              xa_pre_g, xa_post_g, mem_g, xa_w_q, xa_w_kv, xa_w_o,
              ffn2_pre_g, ffn2_post_g, ffn2_w_up, ffn2_w_down):
    for l in range(DEPTH):
        h = swiglu_ffn(rms_norm(x, ffn1_pre_g[l]), ffn1_w_up[l], ffn1_w_down[l])
        x = x + FFN_RES_SCALE * rms_norm(h, ffn1_post_g[l])
        h = parallel_mixer(rms_norm(x, mix_pre_g[l]), w_in[l], conv_w[l], conv_b[l],
                           lru_w_a[l], lru_b_a[l], lru_w_i[l], lru_b_i[l], lru_lambda[l],
                           gla_w_gk2[l], gla_b_gk[l], gla_norm_g[l], b_gate[l], w_branch[l], w_out[l])
        x = x + rms_norm(h, mix_post_g[l])
        h = memory_cross_attention(rms_norm(x, xa_pre_g[l]), rms_norm(mem, mem_g[l]),
                                   xa_w_q[l], xa_w_kv[l], xa_w_o[l])
        x = x + rms_norm(h, xa_post_g[l])
        h = swiglu_ffn(rms_norm(x, ffn2_pre_g[l]), ffn2_w_up[l], ffn2_w_down[l])
        x = x + FFN_RES_SCALE * rms_norm(h, ffn2_post_g[l])
    return x
```

```python
import functools
import math

import jax
import jax.numpy as jnp
from jax import lax
from jax.experimental import pallas as pl
from jax.experimental.pallas import tpu as pltpu

F32 = jnp.float32
BF16 = jnp.bfloat16

D_MODEL = 2048
D_FF = 5504
FFN_RES_SCALE = 0.5
W_LRU = D_MODEL // 2
LRU_BLOCKS = 8
LRU_BW = W_LRU // LRU_BLOCKS
CONV_W = 4
LRU_C = 8.0
GLA_H = 4
GLA_DK = 128
GLA_DV = 256
GLA_RANK = 16
GLA_NORMALIZER = 16.0
GLA_CHUNK = 64
XA_H = 4
XA_DH = 128
N_BRANCH = 2
EPS = 1e-6

LANES = 128
SUBLANES = 8
V7X_VMEM_BYTES = 64 * 1024 * 1024
VMEM_LIMIT_CAP = V7X_VMEM_BYTES - 6 * 1024 * 1024

FFN_TF = 512
D_FF_PAD = -(-D_FF // FFN_TF) * FFN_TF
FFN_TM = 512
PROJ_TM = 1024
PROJ_TN = 1024
N_PROJ = N_BRANCH * D_MODEL + 2 * W_LRU + 2 * GLA_H * GLA_DV + 2 * GLA_H * GLA_DK
GK_PAD = LANES
LRU_T = 256
GLA_T = 256
MERGE_TM = 512
XA_TM = 512

COL_GATE = 0
COL_XLRU = (N_BRANCH * D_MODEL) // W_LRU
COL_GLRU = COL_XLRU + 1
COL_V = COL_GLRU + 1
COL_R = COL_V + 1
COL_Q = (N_BRANCH * D_MODEL + 2 * W_LRU + 2 * GLA_H * GLA_DV) // (GLA_H * GLA_DK)
COL_K = COL_Q + 1


def _vmem_limit(nbytes):
    return int(min(VMEM_LIMIT_CAP, nbytes))


def _params(semantics, vmem_bytes):
    return pltpu.CompilerParams(dimension_semantics=semantics,
                                vmem_limit_bytes=_vmem_limit(vmem_bytes))


def _resident(shape, index_map):
    return pl.BlockSpec(shape, index_map, pipeline_mode=pl.Buffered(1))


def _rms(x, g):
    ms = jnp.mean(x * x, axis=-1, keepdims=True)
    return x * lax.rsqrt(ms + EPS) * g


def _silu(x):
    return x * jax.nn.sigmoid(x)


def _ffn_kernel(x_ref, pre_g_ref, wg_ref, wu_ref, wd_ref, post_g_ref, o_ref, xn_ref, acc_ref):
    j = pl.program_id(1)

    @pl.when(j == 0)
    def _():
        xn_ref[...] = _rms(x_ref[...], pre_g_ref[...]).astype(BF16)
        acc_ref[...] = jnp.zeros_like(acc_ref)

    xn = xn_ref[...]
    gate = jnp.dot(xn, wg_ref[...], preferred_element_type=F32)
    up = jnp.dot(xn, wu_ref[...], preferred_element_type=F32)
    act = (_silu(gate) * up).astype(BF16)
    acc_ref[...] += jnp.dot(act, wd_ref[...], preferred_element_type=F32)

    @pl.when(j == pl.num_programs(1) - 1)
    def _():
        o_ref[...] = x_ref[...] + FFN_RES_SCALE * _rms(acc_ref[...], post_g_ref[...])


def _ffn(x2, pre_g, w_up_p, w_down_p, post_g):
    m, d = x2.shape
    tm, tf = min(FFN_TM, m), FFN_TF
    nf = D_FF_PAD // tf
    vmem = (2 * 2 * tm * d * 4
            + tm * d * (2 + 4)
            + 2 * 3 * d * tf * 2
            + tm * d * 4 + 4 * tm * tf * 4)
    return pl.pallas_call(
        _ffn_kernel,
        grid=(m // tm, nf),
        in_specs=[
            pl.BlockSpec((tm, d), lambda i, j: (i, 0)),
            _resident((1, d), lambda i, j: (0, 0)),
            pl.BlockSpec((d, tf), lambda i, j: (0, j)),
            pl.BlockSpec((d, tf), lambda i, j: (0, j + nf)),
            pl.BlockSpec((tf, d), lambda i, j: (j, 0)),
            _resident((1, d), lambda i, j: (0, 0)),
        ],
        out_specs=pl.BlockSpec((tm, d), lambda i, j: (i, 0)),
        out_shape=jax.ShapeDtypeStruct((m, d), F32),
        scratch_shapes=[pltpu.VMEM((tm, d), BF16), pltpu.VMEM((tm, d), F32)],
        compiler_params=_params(("arbitrary", "arbitrary"), vmem),
        name="ffn",
    )(x2, pre_g, w_up_p, w_up_p, w_down_p, post_g)


def _norm_matmul_kernel(x_ref, g_ref, w_ref, o_ref, xn_ref):
    @pl.when(pl.program_id(1) == 0)
    def _():
        xn_ref[...] = _rms(x_ref[...], g_ref[...]).astype(BF16)

    o_ref[...] = jnp.dot(xn_ref[...], w_ref[...], preferred_element_type=F32).astype(o_ref.dtype)


def _norm_matmul_side_kernel(x_ref, g_ref, w_ref, ws_ref, o_ref, side_ref, xn_ref):
    @pl.when(pl.program_id(1) == 0)
    def _():
        xn = _rms(x_ref[...], g_ref[...]).astype(BF16)
        xn_ref[...] = xn
        side_ref[...] = jnp.dot(xn, ws_ref[...], preferred_element_type=F32)

    o_ref[...] = jnp.dot(xn_ref[...], w_ref[...], preferred_element_type=F32).astype(o_ref.dtype)


def _norm_matmul(x2, g, w, w_side=None, name="norm_matmul"):
    m, d = x2.shape
    n = w.shape[1]
    tm, tn = min(PROJ_TM, m), min(PROJ_TN, n)
    vmem = (2 * tm * d * 4 + tm * d * 2 + 2 * d * tn * 2 + 2 * tm * tn * 2
            + 2 * tm * tn * 4 + tm * d * 4)
    in_specs = [
        pl.BlockSpec((tm, d), lambda i, j: (i, 0)),
        _resident((1, d), lambda i, j: (0, 0)),
        pl.BlockSpec((d, tn), lambda i, j: (0, j)),
    ]
    out_specs = pl.BlockSpec((tm, tn), lambda i, j: (i, j))
    out_shape = jax.ShapeDtypeStruct((m, n), BF16)
    args = [x2, g, w]
    kern = _norm_matmul_kernel
    if w_side is not None:
        ns = w_side.shape[1]
        in_specs.append(_resident((d, ns), lambda i, j: (0, 0)))
        out_specs = [out_specs, pl.BlockSpec((tm, ns), lambda i, j: (i, 0))]
        out_shape = [out_shape, jax.ShapeDtypeStruct((m, ns), F32)]
        args.append(w_side)
        kern = _norm_matmul_side_kernel
        vmem += d * ns * 2 + 2 * tm * ns * 4
    return pl.pallas_call(
        kern,
        grid=(m // tm, n // tn),
        in_specs=in_specs,
        out_specs=out_specs,
        out_shape=out_shape,
        scratch_shapes=[pltpu.VMEM((tm, d), BF16)],
        compiler_params=_params(("arbitrary", "arbitrary"), vmem),
        name=name,
    )(*args)


def _gelu_tanh(x):
    c = math.sqrt(2.0 / math.pi)
    return 0.5 * x * (1.0 + jnp.tanh(c * (x + 0.044715 * (x * x * x))))


def _softplus(x):
    return jnp.maximum(x, 0.0) + jnp.log1p(jnp.exp(-jnp.abs(x)))


def _lru_kernel(xl_ref, gl_ref, cw_ref, cb_ref, wai_ref, ba_ref, bi_ref, lam_ref, y_ref,
                tail_ref, h_ref):
    t_rows = xl_ref.shape[0]
    nblk = t_rows // SUBLANES

    @pl.when(pl.program_id(1) == 0)
    def _():
        tail_ref[...] = jnp.zeros_like(tail_ref)
        h_ref[...] = jnp.zeros_like(h_ref)

    row8 = lax.broadcasted_iota(jnp.int32, (SUBLANES, LRU_BW), 0)
    sub3 = lax.broadcasted_iota(jnp.int32, (nblk, SUBLANES, LRU_BW), 1)

    for n in range(LRU_BLOCKS):
        cs = slice(n * LRU_BW, (n + 1) * LRU_BW)
        x = xl_ref[:, cs].astype(F32)
        prev8 = tail_ref[:, cs]
        xc = x * cw_ref[CONV_W - 1:CONV_W, cs] + cb_ref[:, cs]
        for s in range(1, CONV_W):
            xs = pltpu.roll(x, s, 0)
            ps = pltpu.roll(prev8, s, 0)
            head = jnp.where(row8 < s, ps, xs[:SUBLANES])
            xs = jnp.concatenate([head, xs[SUBLANES:]], axis=0)
            xc = xc + xs * cw_ref[CONV_W - 1 - s:CONV_W - s, cs]
        tail_ref[:, cs] = x[t_rows - SUBLANES:]

        pre = jnp.dot(xc.astype(BF16), wai_ref[n], preferred_element_type=F32)
        r = jax.nn.sigmoid(pre[:, :LRU_BW] + ba_ref[:, cs])
        i = jax.nn.sigmoid(pre[:, LRU_BW:] + bi_ref[:, cs])
        log_a = (-LRU_C * _softplus(-lam_ref[:, cs])) * r
        a = jnp.exp(log_a)
        u = jnp.sqrt(-jnp.tanh(log_a) * (1.0 + a * a)) * (i * xc)

        a3 = a.reshape(nblk, SUBLANES, LRU_BW)
        u3 = u.reshape(nblk, SUBLANES, LRU_BW)
        for dd in (1, 2, 4):
            keep = sub3 >= dd
            a_s = jnp.where(keep, pltpu.roll(a3, dd, 1), 1.0)
            u_s = jnp.where(keep, pltpu.roll(u3, dd, 1), 0.0)
            u3 = a3 * u_s + u3
            a3 = a3 * a_s
        carry = jnp.broadcast_to(h_ref[:, cs], (SUBLANES, LRU_BW))
        hs = []
        for b in range(nblk):
            hb = u3[b] + a3[b] * carry
            hs.append(hb)
            carry = jnp.broadcast_to(hb[SUBLANES - 1:SUBLANES], (SUBLANES, LRU_BW))
        h = jnp.concatenate(hs, axis=0)
        h_ref[:, cs] = carry[0:1]

        y = h * _gelu_tanh(gl_ref[:, cs].astype(F32))
        y_ref[:, cs] = y.astype(y_ref.dtype)


def _lru(proj, conv_w, conv_b, w_ai, b_a, b_i, lam, batch, seq):
    t = min(LRU_T, seq)
    nt = seq // t
    w = W_LRU
    vmem = 2 * 3 * t * w * 2 + 2 * LRU_BLOCKS * LRU_BW * 2 * LRU_BW * 2 + 64 * t * LRU_BW * 4
    return pl.pallas_call(
        _lru_kernel,
        grid=(batch, nt),
        in_specs=[
            pl.BlockSpec((t, w), lambda b, s: (b * nt + s, COL_XLRU)),
            pl.BlockSpec((t, w), lambda b, s: (b * nt + s, COL_GLRU)),
            _resident((CONV_W, w), lambda b, s: (0, 0)),
            _resident((1, w), lambda b, s: (0, 0)),
            _resident((LRU_BLOCKS, LRU_BW, 2 * LRU_BW), lambda b, s: (0, 0, 0)),
            _resident((1, w), lambda b, s: (0, 0)),
            _resident((1, w), lambda b, s: (0, 0)),
            _resident((1, w), lambda b, s: (0, 0)),
        ],
        out_specs=pl.BlockSpec((t, w), lambda b, s: (b * nt + s, 0)),
        out_shape=jax.ShapeDtypeStruct((batch * seq, w), BF16),
        scratch_shapes=[pltpu.VMEM((SUBLANES, w), F32), pltpu.VMEM((1, w), F32)],
        compiler_params=_params(("arbitrary", "arbitrary"), vmem + (8 << 20)),
        name="rglru",
    )(proj, proj, conv_w, conv_b, w_ai, b_a, b_i, lam)


def _log_sigmoid(x):
    return jnp.minimum(x, 0.0) - jnp.log1p(jnp.exp(-jnp.abs(x)))


def _gla_kernel(q_ref, k_ref, v_ref, r_ref, gk_ref, wgk2_ref, bgk_ref, ng_ref, y_ref,
                bcum_ref, st_ref):
    t_rows = q_ref.shape[0]
    c = GLA_CHUNK
    hk = GLA_H * GLA_DK

    @pl.when(pl.program_id(1) == 0)
    def _():
        st_ref[...] = jnp.zeros_like(st_ref)

    z = jnp.dot(gk_ref[...], wgk2_ref[...], preferred_element_type=F32,
                precision=lax.Precision.HIGHEST) + bgk_ref[...]
    la = _log_sigmoid(z) * (1.0 / GLA_NORMALIZER)
    pos = lax.broadcasted_iota(jnp.int32, (t_rows, hk), 0) & (c - 1)
    dd = 1
    while dd < c:
        la = la + jnp.where(pos >= dd, pltpu.roll(la, dd, 0), 0.0)
        dd *= 2
    bcum_ref[...] = la

    tril = (lax.broadcasted_iota(jnp.int32, (c, c), 0)
            >= lax.broadcasted_iota(jnp.int32, (c, c), 1))
    nt_dims = (((1,), (1,)), ((), ()))
    tn_dims = (((0,), (0,)), ((), ()))

    def chunk_body(ci, carry):
        r0 = pl.multiple_of(ci * c, c)
        rows = pl.ds(r0, c)
        bc = bcum_ref[rows, :]
        b_last = bcum_ref[pl.ds(r0 + c - 1, 1), :]
        q = q_ref[rows, :].astype(F32) * (GLA_DK ** -0.5)
        k = k_ref[rows, :].astype(F32)
        qe = (q * jnp.exp(bc)).astype(BF16)
        ke = (k * jnp.exp(-bc)).astype(BF16)
        kd = (k * jnp.exp(b_last - bc)).astype(BF16)
        g = jnp.exp(b_last)
        for h in range(GLA_H):
            ks = slice(h * GLA_DK, (h + 1) * GLA_DK)
            vs = slice(h * GLA_DV, (h + 1) * GLA_DV)
            v_h = v_ref[rows, vs]
            st = st_ref[h]
            s = lax.dot_general(qe[:, ks], ke[:, ks], nt_dims, preferred_element_type=F32)
            s = jnp.where(tril, s, 0.0).astype(BF16)
            o = jnp.dot(s, v_h, preferred_element_type=F32)
            o = o + lax.dot_general(qe[:, ks], st.astype(BF16), nt_dims,
                                    preferred_element_type=F32)
            upd = lax.dot_general(v_h, kd[:, ks], tn_dims, preferred_element_type=F32)
            st_ref[h] = st * g[:, ks] + upd
            o = o * lax.rsqrt(jnp.mean(o * o, axis=-1, keepdims=True) + EPS) * ng_ref[...]
            o = o * _silu(r_ref[rows, vs].astype(F32))
            y_ref[rows, vs] = o.astype(y_ref.dtype)
        return carry

    lax.fori_loop(0, t_rows // c, chunk_body, 0)


def _gla(proj, gk, w_gk2_p, b_gk, norm_g, batch, seq):
    t = min(GLA_T, seq)
    nt = seq // t
    hk, hv = GLA_H * GLA_DK, GLA_H * GLA_DV
    vmem = (2 * (2 * t * hk * 2 + 3 * t * hv * 2 + t * GK_PAD * 4) + t * hk * 4
            + GLA_H * GLA_DV * GLA_DK * 4 + 8 * t * hk * 4)
    return pl.pallas_call(
        _gla_kernel,
        grid=(batch, nt),
        in_specs=[
            pl.BlockSpec((t, hk), lambda b, s: (b * nt + s, COL_Q)),
            pl.BlockSpec((t, hk), lambda b, s: (b * nt + s, COL_K)),
            pl.BlockSpec((t, hv), lambda b, s: (b * nt + s, COL_V)),
            pl.BlockSpec((t, hv), lambda b, s: (b * nt + s, COL_R)),
            pl.BlockSpec((t, GK_PAD), lambda b, s: (b * nt + s, 0)),
            _resident((GK_PAD, hk), lambda b, s: (0, 0)),
            _resident((1, hk), lambda b, s: (0, 0)),
            _resident((1, GLA_DV), lambda b, s: (0, 0)),
        ],
        out_specs=pl.BlockSpec((t, hv), lambda b, s: (b * nt + s, 0)),
        out_shape=jax.ShapeDtypeStruct((batch * seq, hv), BF16),
        scratch_shapes=[pltpu.VMEM((t, hk), F32), pltpu.VMEM((GLA_H, GLA_DV, GLA_DK), F32)],
        compiler_params=_params(("arbitrary", "arbitrary"), vmem + (8 << 20)),
        name="gla",
    )(proj, proj, proj, proj, gk, w_gk2_p, b_gk, norm_g)


def _merge_kernel(x_ref, ya_ref, yb_ref, gl_ref, bg_ref, wb_ref, wo_ref, post_g_ref, o_ref):
    d = x_ref.shape[1]
    z_a = jnp.dot(ya_ref[...], wb_ref[:W_LRU, :], preferred_element_type=F32)
    z_b = jnp.dot(yb_ref[...], wb_ref[W_LRU:, :], preferred_element_type=F32)
    g_a = jax.nn.sigmoid(gl_ref[:, :d].astype(F32) + bg_ref[:, :d])
    g_b = jax.nn.sigmoid(gl_ref[:, d:].astype(F32) + bg_ref[:, d:])
    merged = (g_a * z_a + g_b * z_b).astype(BF16)
    h = jnp.dot(merged, wo_ref[...], preferred_element_type=F32)
    o_ref[...] = x_ref[...] + _rms(h, post_g_ref[...])


def _merge(x2, y_a, y_b, proj, b_gate, w_branch, w_out, post_g):
    m, d = x2.shape
    tm = min(MERGE_TM, m)
    wb_rows = w_branch.shape[0]
    vmem = (2 * 2 * tm * d * 4 + 2 * 2 * tm * W_LRU * 2 + 2 * tm * 2 * d * 2
            + wb_rows * d * 2 + d * d * 2 + 6 * tm * d * 4)
    return pl.pallas_call(
        _merge_kernel,
        grid=(m // tm,),
        in_specs=[
            pl.BlockSpec((tm, d), lambda i: (i, 0)),
            pl.BlockSpec((tm, W_LRU), lambda i: (i, 0)),
            pl.BlockSpec((tm, GLA_H * GLA_DV), lambda i: (i, 0)),
            pl.BlockSpec((tm, N_BRANCH * d), lambda i: (i, COL_GATE)),
            _resident((1, N_BRANCH * d), lambda i: (0, 0)),
            _resident((wb_rows, d), lambda i: (0, 0)),
            _resident((d, d), lambda i: (0, 0)),
            _resident((1, d), lambda i: (0, 0)),
        ],
        out_specs=pl.BlockSpec((tm, d), lambda i: (i, 0)),
        out_shape=jax.ShapeDtypeStruct((m, d), F32),
        compiler_params=_params(("arbitrary",), vmem),
        name="merge",
    )(x2, y_a, y_b, proj, b_gate, w_branch, w_out, post_g)


def _xattn_kernel(x_ref, pre_g_ref, wq_ref, kk_ref, vv_ref, wo_ref, post_g_ref, o_ref):
    xn = _rms(x_ref[...], pre_g_ref[...]).astype(BF16)
    q = jnp.dot(xn, wq_ref[...], preferred_element_type=F32).astype(BF16)
    nt_dims = (((1,), (1,)), ((), ()))
    outs = []
    for h in range(XA_H):
        hs = slice(h * XA_DH, (h + 1) * XA_DH)
        s = lax.dot_general(q[:, hs], kk_ref[:, hs], nt_dims,
                            preferred_element_type=F32) * (XA_DH ** -0.5)
        p = jnp.exp(s - jnp.max(s, axis=-1, keepdims=True))
        p = p / jnp.sum(p, axis=-1, keepdims=True)
        outs.append(jnp.dot(p.astype(BF16), vv_ref[:, hs], preferred_element_type=F32))
    o = jnp.concatenate(outs, axis=1).astype(BF16)
    h_out = jnp.dot(o, wo_ref[...], preferred_element_type=F32)
    o_ref[...] = x_ref[...] + _rms(h_out, post_g_ref[...])


def _xattn(x2, pre_g, w_q, kv, w_o, post_g, batch, seq, mem_len):
    m, d = x2.shape
    tm = min(XA_TM, seq)
    nt = seq // tm
    hd = XA_H * XA_DH
    vmem = (2 * 2 * tm * d * 4 + 2 * d * hd * 2 + 2 * 2 * mem_len * hd * 2
            + 4 * tm * d * 4 + 8 * tm * mem_len * 4)
    return pl.pallas_call(
        _xattn_kernel,
        grid=(batch, nt),
        in_specs=[
            pl.BlockSpec((tm, d), lambda b, s: (b * nt + s, 0)),
            _resident((1, d), lambda b, s: (0, 0)),
            _resident((d, hd), lambda b, s: (0, 0)),
            pl.BlockSpec((mem_len, hd), lambda b, s: (b, 0)),
            pl.BlockSpec((mem_len, hd), lambda b, s: (b, 1)),
            _resident((hd, d), lambda b, s: (0, 0)),
            _resident((1, d), lambda b, s: (0, 0)),
        ],
        out_specs=pl.BlockSpec((tm, d), lambda b, s: (b * nt + s, 0)),
        out_shape=jax.ShapeDtypeStruct((m, d), F32),
        compiler_params=_params(("arbitrary", "arbitrary"), vmem),
        name="xattn",
    )(x2, pre_g, w_q, kv, kv, w_o, post_g)


def _pack_ffn(w_up, w_down):
    pad = D_FF_PAD - D_FF
    gate = jnp.pad(w_up[:, :D_FF], ((0, 0), (0, pad)))
    up = jnp.pad(w_up[:, D_FF:], ((0, 0), (0, pad)))
    w_up_p = jnp.concatenate([gate, up], axis=1).astype(BF16)
    w_down_p = jnp.pad(w_down, ((0, pad), (0, 0))).astype(BF16)
    return w_up_p, w_down_p


def _pack_w_in(w_in):
    sizes = (W_LRU, W_LRU, GLA_H * GLA_DK, GLA_H * GLA_DK, GLA_H * GLA_DV, GLA_H * GLA_DV,
             GLA_RANK, N_BRANCH * D_MODEL)
    offs = [0]
    for s in sizes:
        offs.append(offs[-1] + s)
    x_lru, g_lru, q, k, v, r, gk, gate = (w_in[:, offs[i]:offs[i + 1]] for i in range(8))
    w_main = jnp.concatenate([gate, x_lru, g_lru, v, r, q, k], axis=1).astype(BF16)
    w_gk = jnp.pad(gk, ((0, 0), (0, GK_PAD - GLA_RANK))).astype(BF16)
    return w_main, w_gk


def _row(v):
    return v.reshape(1, -1)


def kernel(x, mem, ffn1_pre_g, ffn1_post_g, ffn1_w_up, ffn1_w_down, mix_pre_g, mix_post_g, w_in,
           conv_w, conv_b, lru_w_a, lru_b_a, lru_w_i, lru_b_i, lru_lambda, gla_w_gk2, gla_b_gk,
           gla_norm_g, b_gate, w_branch, w_out, xa_pre_g, xa_post_g, mem_g, xa_w_q, xa_w_kv,
           xa_w_o, ffn2_pre_g, ffn2_post_g, ffn2_w_up, ffn2_w_down):
    batch, seq, d = x.shape
    mem_len = mem.shape[1]
    depth = ffn1_w_up.shape[0]
    x2 = x.reshape(batch * seq, d)
    mem2 = mem.reshape(batch * mem_len, d)

    for l in range(depth):
        w_up_p, w_down_p = _pack_ffn(ffn1_w_up[l], ffn1_w_down[l])
        x2 = _ffn(x2, _row(ffn1_pre_g[l]), w_up_p, w_down_p, _row(ffn1_post_g[l]))

        w_main, w_gk = _pack_w_in(w_in[l])
        proj, gk = _norm_matmul(x2, _row(mix_pre_g[l]), w_main, w_gk, name="mix_in_proj")
        w_ai = jnp.concatenate([lru_w_a[l], lru_w_i[l]], axis=-1).astype(BF16)
        y_a = _lru(proj, conv_w[l], _row(conv_b[l]), w_ai, _row(lru_b_a[l]), _row(lru_b_i[l]),
                   _row(lru_lambda[l]), batch, seq)
        w_gk2_p = jnp.pad(gla_w_gk2[l], ((0, GK_PAD - GLA_RANK), (0, 0)))
        y_b = _gla(proj, gk, w_gk2_p, _row(gla_b_gk[l]), _row(gla_norm_g[l]), batch, seq)
        x2 = _merge(x2, y_a, y_b, proj, b_gate[l].reshape(1, N_BRANCH * d),
                    w_branch[l].astype(BF16), w_out[l].astype(BF16), _row(mix_post_g[l]))

        kv = _norm_matmul(mem2, _row(mem_g[l]), xa_w_kv[l].astype(BF16), name="mem_kv_proj")
        x2 = _xattn(x2, _row(xa_pre_g[l]), xa_w_q[l].astype(BF16), kv, xa_w_o[l].astype(BF16),
                    _row(xa_post_g[l]), batch, seq, mem_len)

        w_up_p, w_down_p = _pack_ffn(ffn2_w_up[l], ffn2_w_down[l])
        x2 = _ffn(x2, _row(ffn2_pre_g[l]), w_up_p, w_down_p, _row(ffn2_post_g[l]))

    return x2.reshape(batch, seq, d)
```

```python
import math

import jax
import jax.numpy as jnp
from jax import lax
from jax.experimental import pallas as pl
from jax.experimental.pallas import tpu as pltpu

F32 = jnp.float32
BF16 = jnp.bfloat16

D_MODEL = 2048
D_FF = 5504
FFN_RES_SCALE = 0.5
W_LRU = D_MODEL // 2
LRU_BLOCKS = 8
LRU_BW = W_LRU // LRU_BLOCKS
CONV_W = 4
LRU_C = 8.0
GLA_H = 4
GLA_DK = 128
GLA_DV = 256
GLA_RANK = 16
GLA_NORMALIZER = 16.0
GLA_CHUNK = 64
XA_H = 4
XA_DH = 128
N_BRANCH = 2
EPS = 1e-6

LANES = 128
SUBLANES = 8
V7X_VMEM_BYTES = 64 * 1024 * 1024
VMEM_LIMIT_CAP = V7X_VMEM_BYTES - 6 * 1024 * 1024

FFN_TF = 512
FFN_NF = -(-D_FF // FFN_TF)
FFN_TM = 512
PROJ_TM = 1024
PROJ_TN = 1024
N_PROJ = N_BRANCH * D_MODEL + 2 * W_LRU + 2 * GLA_H * GLA_DV + 2 * GLA_H * GLA_DK
GK_PAD = LANES
LRU_T = 256
GLA_T = 256
MERGE_TM = 512
XA_TM = 512

COL_GATE = 0
COL_XLRU = (N_BRANCH * D_MODEL) // W_LRU
COL_GLRU = COL_XLRU + 1
COL_V = COL_GLRU + 1
COL_R = COL_V + 1
COL_Q = (N_BRANCH * D_MODEL + 2 * W_LRU + 2 * GLA_H * GLA_DV) // (GLA_H * GLA_DK)
COL_K = COL_Q + 1


def _params(semantics, vmem_bytes):
    return pltpu.CompilerParams(dimension_semantics=semantics,
                                vmem_limit_bytes=int(min(VMEM_LIMIT_CAP, vmem_bytes)))


def _layer_resident(tail, layer):
    zeros = (0,) * len(tail)
    return pl.BlockSpec((None,) + tuple(tail), lambda *_: (layer,) + zeros,
                        pipeline_mode=pl.Buffered(1))


def _rms(x, g):
    ms = jnp.mean(x * x, axis=-1, keepdims=True)
    return x * lax.rsqrt(ms + EPS) * g


def _silu(x):
    return x * jax.nn.sigmoid(x)


def _ffn_window_start(j, base=0):
    return LANES * (base // LANES + jnp.minimum(j * (FFN_TF // LANES), (D_FF - FFN_TF) // LANES))


def _ffn_kernel(x_ref, pre_g_ref, wg_ref, wu_ref, wd_ref, post_g_ref, o_ref, xn_ref, acc_ref):
    j = pl.program_id(1)
    nf = pl.num_programs(1)

    @pl.when(j == 0)
    def _():
        xn_ref[...] = _rms(x_ref[...], pre_g_ref[...]).astype(BF16)
        acc_ref[...] = jnp.zeros_like(acc_ref)

    xn = xn_ref[...]
    gate = jnp.dot(xn, wg_ref[...], preferred_element_type=F32)
    up = jnp.dot(xn, wu_ref[...], preferred_element_type=F32)
    act = _silu(gate) * up
    covered = jnp.where(j == nf - 1, nf * FFN_TF - D_FF, 0)
    col = lax.broadcasted_iota(jnp.int32, act.shape, 1)
    act = jnp.where(col >= covered, act, 0.0).astype(BF16)
    acc_ref[...] += jnp.dot(act, wd_ref[...], preferred_element_type=F32)

    @pl.when(j == nf - 1)
    def _():
        o_ref[...] = x_ref[...] + FFN_RES_SCALE * _rms(acc_ref[...], post_g_ref[...])


def _ffn(x2, pre_g, w_up, w_down, post_g, layer):
    m, d = x2.shape
    tm, tf = min(FFN_TM, m), FFN_TF
    vmem = (2 * 2 * tm * d * 4
            + tm * d * (2 + 4)
            + 2 * 3 * d * tf * 2
            + tm * d * 4 + 4 * tm * tf * 4)
    return pl.pallas_call(
        _ffn_kernel,
        grid=(m // tm, FFN_NF),
        in_specs=[
            pl.BlockSpec((tm, d), lambda i, j: (i, 0)),
            _layer_resident((1, d), layer),
            pl.BlockSpec((None, pl.Element(d), pl.Element(tf)),
                         lambda i, j: (layer, 0, _ffn_window_start(j))),
            pl.BlockSpec((None, pl.Element(d), pl.Element(tf)),
                         lambda i, j: (layer, 0, _ffn_window_start(j, base=D_FF))),
            pl.BlockSpec((None, pl.Element(tf), pl.Element(d)),
                         lambda i, j: (layer, _ffn_window_start(j), 0)),
            _layer_resident((1, d), layer),
        ],
        out_specs=pl.BlockSpec((tm, d), lambda i, j: (i, 0)),
        out_shape=jax.ShapeDtypeStruct((m, d), F32),
        scratch_shapes=[pltpu.VMEM((tm, d), BF16), pltpu.VMEM((tm, d), F32)],
        compiler_params=_params(("arbitrary", "arbitrary"), vmem),
        name="ffn",
    )(x2, pre_g, w_up, w_up, w_down, post_g)


def _norm_matmul_kernel(x_ref, g_ref, w_ref, o_ref, xn_ref):
    @pl.when(pl.program_id(1) == 0)
    def _():
        xn_ref[...] = _rms(x_ref[...], g_ref[...]).astype(BF16)

    o_ref[...] = jnp.dot(xn_ref[...], w_ref[...], preferred_element_type=F32).astype(o_ref.dtype)


def _norm_matmul_side_kernel(x_ref, g_ref, w_ref, ws_ref, o_ref, side_ref, xn_ref):
    @pl.when(pl.program_id(1) == 0)
    def _():
        xn = _rms(x_ref[...], g_ref[...]).astype(BF16)
        xn_ref[...] = xn
        side_ref[...] = jnp.dot(xn, ws_ref[...], preferred_element_type=F32)

    o_ref[...] = jnp.dot(xn_ref[...], w_ref[...], preferred_element_type=F32).astype(o_ref.dtype)


def _norm_matmul(x2, g, w, layer, w_side=None, name="norm_matmul"):
    m, d = x2.shape
    n = w.shape[2]
    tm, tn = min(PROJ_TM, m), min(PROJ_TN, n)
    vmem = (2 * tm * d * 4 + tm * d * 2 + 2 * d * tn * 2 + 2 * tm * tn * 2
            + 2 * tm * tn * 4 + tm * d * 4)
    in_specs = [
        pl.BlockSpec((tm, d), lambda i, j: (i, 0)),
        _layer_resident((1, d), layer),
        pl.BlockSpec((None, d, tn), lambda i, j: (layer, 0, j)),
    ]
    out_specs = pl.BlockSpec((tm, tn), lambda i, j: (i, j))
    out_shape = jax.ShapeDtypeStruct((m, n), BF16)
    args = [x2, g, w]
    kern = _norm_matmul_kernel
    if w_side is not None:
        ns = w_side.shape[2]
        in_specs.append(_layer_resident((d, ns), layer))
        out_specs = [out_specs, pl.BlockSpec((tm, ns), lambda i, j: (i, 0))]
        out_shape = [out_shape, jax.ShapeDtypeStruct((m, ns), F32)]
        args.append(w_side)
        kern = _norm_matmul_side_kernel
        vmem += d * ns * 2 + 2 * tm * ns * 4
    return pl.pallas_call(
        kern,
        grid=(m // tm, n // tn),
        in_specs=in_specs,
        out_specs=out_specs,
        out_shape=out_shape,
        scratch_shapes=[pltpu.VMEM((tm, d), BF16)],
        compiler_params=_params(("arbitrary", "arbitrary"), vmem),
        name=name,
    )(*args)


def _gelu_tanh(x):
    c = math.sqrt(2.0 / math.pi)
    return 0.5 * x * (1.0 + jnp.tanh(c * (x + 0.044715 * (x * x * x))))


def _softplus(x):
    return jnp.maximum(x, 0.0) + jnp.log1p(jnp.exp(-jnp.abs(x)))


def _lru_kernel(xl_ref, gl_ref, cw_ref, cb_ref, wai_ref, ba_ref, bi_ref, lam_ref, y_ref,
                tail_ref, h_ref):
    t_rows = xl_ref.shape[0]
    nblk = t_rows // SUBLANES

    @pl.when(pl.program_id(1) == 0)
    def _():
        tail_ref[...] = jnp.zeros_like(tail_ref)
        h_ref[...] = jnp.zeros_like(h_ref)

    row8 = lax.broadcasted_iota(jnp.int32, (SUBLANES, LRU_BW), 0)
    sub3 = lax.broadcasted_iota(jnp.int32, (nblk, SUBLANES, LRU_BW), 1)

    for n in range(LRU_BLOCKS):
        cs = slice(n * LRU_BW, (n + 1) * LRU_BW)
        x = xl_ref[:, cs].astype(F32)
        prev8 = tail_ref[:, cs]
        xc = x * cw_ref[CONV_W - 1:CONV_W, cs] + cb_ref[:, cs]
        for s in range(1, CONV_W):
            xs = pltpu.roll(x, s, 0)
            ps = pltpu.roll(prev8, s, 0)
            head = jnp.where(row8 < s, ps, xs[:SUBLANES])
            xs = jnp.concatenate([head, xs[SUBLANES:]], axis=0)
            xc = xc + xs * cw_ref[CONV_W - 1 - s:CONV_W - s, cs]
        tail_ref[:, cs] = x[t_rows - SUBLANES:]

        pre = jnp.dot(xc.astype(BF16), wai_ref[n], preferred_element_type=F32)
        r = jax.nn.sigmoid(pre[:, :LRU_BW] + ba_ref[:, cs])
        i = jax.nn.sigmoid(pre[:, LRU_BW:] + bi_ref[:, cs])
        log_a = (-LRU_C * _softplus(-lam_ref[:, cs])) * r
        a = jnp.exp(log_a)
        u = jnp.sqrt(-jnp.tanh(log_a) * (1.0 + a * a)) * (i * xc)

        a3 = a.reshape(nblk, SUBLANES, LRU_BW)
        u3 = u.reshape(nblk, SUBLANES, LRU_BW)
        for dd in (1, 2, 4):
            keep = sub3 >= dd
            a_s = jnp.where(keep, pltpu.roll(a3, dd, 1), 1.0)
            u_s = jnp.where(keep, pltpu.roll(u3, dd, 1), 0.0)
            u3 = a3 * u_s + u3
            a3 = a3 * a_s
        carry = jnp.broadcast_to(h_ref[:, cs], (SUBLANES, LRU_BW))
        hs = []
        for b in range(nblk):
            hb = u3[b] + a3[b] * carry
            hs.append(hb)
            carry = jnp.broadcast_to(hb[SUBLANES - 1:SUBLANES], (SUBLANES, LRU_BW))
        h = jnp.concatenate(hs, axis=0)
        h_ref[:, cs] = carry[0:1]

        y = h * _gelu_tanh(gl_ref[:, cs].astype(F32))
        y_ref[:, cs] = y.astype(y_ref.dtype)


def _lru(proj, conv_w, conv_b, w_ai, b_a, b_i, lam, layer, batch, seq):
    t = min(LRU_T, seq)
    nt = seq // t
    w = W_LRU
    vmem = 2 * 3 * t * w * 2 + LRU_BLOCKS * LRU_BW * 2 * LRU_BW * 2 + 64 * t * LRU_BW * 4
    return pl.pallas_call(
        _lru_kernel,
        grid=(batch, nt),
        in_specs=[
            pl.BlockSpec((t, w), lambda b, s: (b * nt + s, COL_XLRU)),
            pl.BlockSpec((t, w), lambda b, s: (b * nt + s, COL_GLRU)),
            _layer_resident((CONV_W, w), layer),
            _layer_resident((1, w), layer),
            _layer_resident((LRU_BLOCKS, LRU_BW, 2 * LRU_BW), layer),
            _layer_resident((1, w), layer),
            _layer_resident((1, w), layer),
            _layer_resident((1, w), layer),
        ],
        out_specs=pl.BlockSpec((t, w), lambda b, s: (b * nt + s, 0)),
        out_shape=jax.ShapeDtypeStruct((batch * seq, w), BF16),
        scratch_shapes=[pltpu.VMEM((SUBLANES, w), F32), pltpu.VMEM((1, w), F32)],
        compiler_params=_params(("arbitrary", "arbitrary"), vmem + (8 << 20)),
        name="rglru",
    )(proj, proj, conv_w, conv_b, w_ai, b_a, b_i, lam)


def _log_sigmoid(x):
    return jnp.minimum(x, 0.0) - jnp.log1p(jnp.exp(-jnp.abs(x)))


def _gla_kernel(q_ref, k_ref, v_ref, r_ref, gk_ref, wgk2_ref, bgk_ref, ng_ref, y_ref, st_ref):
    t_rows = q_ref.shape[0]
    c = GLA_CHUNK
    hk = GLA_H * GLA_DK

    @pl.when(pl.program_id(1) == 0)
    def _():
        st_ref[...] = jnp.zeros_like(st_ref)

    z = jnp.dot(gk_ref[...], wgk2_ref[...], preferred_element_type=F32,
                precision=lax.Precision.HIGHEST) + bgk_ref[...]
    la = _log_sigmoid(z) * (1.0 / GLA_NORMALIZER)
    pos = lax.broadcasted_iota(jnp.int32, (t_rows, hk), 0) & (c - 1)
    dd = 1
    while dd < c:
        la = la + jnp.where(pos >= dd, pltpu.roll(la, dd, 0), 0.0)
        dd *= 2
    bcum = la

    tril = (lax.broadcasted_iota(jnp.int32, (c, c), 0)
            >= lax.broadcasted_iota(jnp.int32, (c, c), 1))
    nt_dims = (((1,), (1,)), ((), ()))
    tn_dims = (((0,), (0,)), ((), ()))
    n_chunks = t_rows // c

    qe_c, g_c, o_intra, upd = [], [], [], []
    for ci in range(n_chunks):
        rows = slice(ci * c, (ci + 1) * c)
        bc = bcum[rows]
        b_last = bcum[(ci + 1) * c - 1:(ci + 1) * c]
        q = q_ref[rows, :].astype(F32) * (GLA_DK ** -0.5)
        k = k_ref[rows, :].astype(F32)
        qe = (q * jnp.exp(bc)).astype(BF16)
        ke = (k * jnp.exp(-bc)).astype(BF16)
        kd = (k * jnp.exp(b_last - bc)).astype(BF16)
        qe_c.append(qe)
        g_c.append(jnp.exp(b_last))
        o_h, upd_h = [], []
        for h in range(GLA_H):
            ks = slice(h * GLA_DK, (h + 1) * GLA_DK)
            v_h = v_ref[rows, h * GLA_DV:(h + 1) * GLA_DV]
            s = lax.dot_general(qe[:, ks], ke[:, ks], nt_dims, preferred_element_type=F32)
            s = jnp.where(tril, s, 0.0).astype(BF16)
            o_h.append(jnp.dot(s, v_h, preferred_element_type=F32))
            upd_h.append(lax.dot_general(v_h, kd[:, ks], tn_dims,
                                         preferred_element_type=F32))
        o_intra.append(o_h)
        upd.append(upd_h)

    for h in range(GLA_H):
        ks = slice(h * GLA_DK, (h + 1) * GLA_DK)
        vs = slice(h * GLA_DV, (h + 1) * GLA_DV)
        st = st_ref[h]
        for ci in range(n_chunks):
            rows = slice(ci * c, (ci + 1) * c)
            o = o_intra[ci][h] + lax.dot_general(qe_c[ci][:, ks], st.astype(BF16), nt_dims,
                                                 preferred_element_type=F32)
            st = st * g_c[ci][:, ks] + upd[ci][h]
            o = o * lax.rsqrt(jnp.mean(o * o, axis=-1, keepdims=True) + EPS) * ng_ref[...]
            o = o * _silu(r_ref[rows, vs].astype(F32))
            y_ref[rows, vs] = o.astype(y_ref.dtype)
        st_ref[h] = st


def _gla(proj, gk, w_gk2_p, b_gk, norm_g, layer, batch, seq):
    t = min(GLA_T, seq)
    nt = seq // t
    hk, hv = GLA_H * GLA_DK, GLA_H * GLA_DV
    vmem = (2 * (2 * t * hk * 2 + 3 * t * hv * 2 + t * GK_PAD * 4) + t * hk * 4
            + GLA_H * GLA_DV * GLA_DK * 4 + 8 * t * hk * 4)
    return pl.pallas_call(
        _gla_kernel,
        grid=(batch, nt),
        in_specs=[
            pl.BlockSpec((t, hk), lambda b, s: (b * nt + s, COL_Q)),
            pl.BlockSpec((t, hk), lambda b, s: (b * nt + s, COL_K)),
            pl.BlockSpec((t, hv), lambda b, s: (b * nt + s, COL_V)),
            pl.BlockSpec((t, hv), lambda b, s: (b * nt + s, COL_R)),
            pl.BlockSpec((t, GK_PAD), lambda b, s: (b * nt + s, 0)),
            _layer_resident((GK_PAD, hk), layer),
            _layer_resident((1, hk), layer),
            _layer_resident((1, GLA_DV), layer),
        ],
        out_specs=pl.BlockSpec((t, hv), lambda b, s: (b * nt + s, 0)),
        out_shape=jax.ShapeDtypeStruct((batch * seq, hv), BF16),
        scratch_shapes=[pltpu.VMEM((GLA_H, GLA_DV, GLA_DK), F32)],
        compiler_params=_params(("arbitrary", "arbitrary"), vmem + (8 << 20)),
        name="gla",
    )(proj, proj, proj, proj, gk, w_gk2_p, b_gk, norm_g)


def _merge_kernel(x_ref, ya_ref, yb_ref, gl_ref, bg_ref, wb_ref, wo_ref, post_g_ref, o_ref):
    d = x_ref.shape[1]
    z_a = jnp.dot(ya_ref[...], wb_ref[:W_LRU, :], preferred_element_type=F32)
    z_b = jnp.dot(yb_ref[...], wb_ref[W_LRU:, :], preferred_element_type=F32)
    g_a = jax.nn.sigmoid(gl_ref[:, :d].astype(F32) + bg_ref[:, :d])
    g_b = jax.nn.sigmoid(gl_ref[:, d:].astype(F32) + bg_ref[:, d:])
    merged = (g_a * z_a + g_b * z_b).astype(BF16)
    h = jnp.dot(merged, wo_ref[...], preferred_element_type=F32)
    o_ref[...] = x_ref[...] + _rms(h, post_g_ref[...])


def _merge(x2, y_a, y_b, proj, b_gate, w_branch, w_out, post_g, layer):
    m, d = x2.shape
    tm = min(MERGE_TM, m)
    wb_rows = w_branch.shape[1]
    vmem = (2 * 2 * tm * d * 4 + 2 * 2 * tm * W_LRU * 2 + 2 * tm * 2 * d * 2
            + wb_rows * d * 2 + d * d * 2 + 6 * tm * d * 4)
    return pl.pallas_call(
        _merge_kernel,
        grid=(m // tm,),
        in_specs=[
            pl.BlockSpec((tm, d), lambda i: (i, 0)),
            pl.BlockSpec((tm, W_LRU), lambda i: (i, 0)),
            pl.BlockSpec((tm, GLA_H * GLA_DV), lambda i: (i, 0)),
            pl.BlockSpec((tm, N_BRANCH * d), lambda i: (i, COL_GATE)),
            _layer_resident((1, N_BRANCH * d), layer),
            _layer_resident((wb_rows, d), layer),
            _layer_resident((d, d), layer),
            _layer_resident((1, d), layer),
        ],
        out_specs=pl.BlockSpec((tm, d), lambda i: (i, 0)),
        out_shape=jax.ShapeDtypeStruct((m, d), F32),
        compiler_params=_params(("arbitrary",), vmem),
        name="merge",
    )(x2, y_a, y_b, proj, b_gate, w_branch, w_out, post_g)


def _xattn_kernel(x_ref, pre_g_ref, wq_ref, kk_ref, vv_ref, wo_ref, post_g_ref, o_ref):
    xn = _rms(x_ref[...], pre_g_ref[...]).astype(BF16)
    q = jnp.dot(xn, wq_ref[...], preferred_element_type=F32).astype(BF16)
    nt_dims = (((1,), (1,)), ((), ()))
    outs = []
    for h in range(XA_H):
        hs = slice(h * XA_DH, (h + 1) * XA_DH)
        s = lax.dot_general(q[:, hs], kk_ref[:, hs], nt_dims,
                            preferred_element_type=F32) * (XA_DH ** -0.5)
        p = jnp.exp(s - jnp.max(s, axis=-1, keepdims=True))
        p = p / jnp.sum(p, axis=-1, keepdims=True)
        outs.append(jnp.dot(p.astype(BF16), vv_ref[:, hs], preferred_element_type=F32))
    o = jnp.concatenate(outs, axis=1).astype(BF16)
    h_out = jnp.dot(o, wo_ref[...], preferred_element_type=F32)
    o_ref[...] = x_ref[...] + _rms(h_out, post_g_ref[...])


def _xattn(x2, pre_g, w_q, kv, w_o, post_g, layer, batch, seq, mem_len):
    m, d = x2.shape
    tm = min(XA_TM, seq)
    nt = seq // tm
    hd = XA_H * XA_DH
    vmem = (2 * 2 * tm * d * 4 + 2 * d * hd * 2 + 2 * 2 * mem_len * hd * 2
            + 4 * tm * d * 4 + 8 * tm * mem_len * 4)
    return pl.pallas_call(
        _xattn_kernel,
        grid=(batch, nt),
        in_specs=[
            pl.BlockSpec((tm, d), lambda b, s: (b * nt + s, 0)),
            _layer_resident((1, d), layer),
            _layer_resident((d, hd), layer),
            pl.BlockSpec((mem_len, hd), lambda b, s: (b, 0)),
            pl.BlockSpec((mem_len, hd), lambda b, s: (b, 1)),
            _layer_resident((hd, d), layer),
            _layer_resident((1, d), layer),
        ],
        out_specs=pl.BlockSpec((tm, d), lambda b, s: (b * nt + s, 0)),
        out_shape=jax.ShapeDtypeStruct((m, d), F32),
        compiler_params=_params(("arbitrary", "arbitrary"), vmem),
        name="xattn",
    )(x2, pre_g, w_q, kv, kv, w_o, post_g)


def _pack_w_in(w_in):
    sizes = (W_LRU, W_LRU, GLA_H * GLA_DK, GLA_H * GLA_DK, GLA_H * GLA_DV, GLA_H * GLA_DV,
             GLA_RANK, N_BRANCH * D_MODEL)
    offs = [0]
    for s in sizes:
        offs.append(offs[-1] + s)
    x_lru, g_lru, q, k, v, r, gk, gate = (w_in[..., offs[i]:offs[i + 1]] for i in range(8))
    w_main = jnp.concatenate([gate, x_lru, g_lru, v, r, q, k], axis=-1).astype(BF16)
    w_gk = jnp.pad(gk, ((0, 0), (0, 0), (0, GK_PAD - GLA_RANK))).astype(BF16)
    return w_main, w_gk


def _rows(v):
    return v.reshape(v.shape[0], 1, -1)


def kernel(x, mem, ffn1_pre_g, ffn1_post_g, ffn1_w_up, ffn1_w_down, mix_pre_g, mix_post_g, w_in,
           conv_w, conv_b, lru_w_a, lru_b_a, lru_w_i, lru_b_i, lru_lambda, gla_w_gk2, gla_b_gk,
           gla_norm_g, b_gate, w_branch, w_out, xa_pre_g, xa_post_g, mem_g, xa_w_q, xa_w_kv,
           xa_w_o, ffn2_pre_g, ffn2_post_g, ffn2_w_up, ffn2_w_down):
    batch, seq, d = x.shape
    mem_len = mem.shape[1]
    depth = ffn1_w_up.shape[0]
    x2 = x.reshape(batch * seq, d)
    mem2 = mem.reshape(batch * mem_len, d)

    ffn1_up, ffn1_down = ffn1_w_up.astype(BF16), ffn1_w_down.astype(BF16)
    ffn2_up, ffn2_down = ffn2_w_up.astype(BF16), ffn2_w_down.astype(BF16)
    w_main, w_gk = _pack_w_in(w_in)
    w_ai = jnp.concatenate([lru_w_a, lru_w_i], axis=-1).astype(BF16)
    w_gk2_p = jnp.pad(gla_w_gk2, ((0, 0), (0, GK_PAD - GLA_RANK), (0, 0)))
    wb, wo = w_branch.astype(BF16), w_out.astype(BF16)
    xa_q, xa_kv, xa_o = xa_w_q.astype(BF16), xa_w_kv.astype(BF16), xa_w_o.astype(BF16)
    ffn1_pre, ffn1_post = _rows(ffn1_pre_g), _rows(ffn1_post_g)
    ffn2_pre, ffn2_post = _rows(ffn2_pre_g), _rows(ffn2_post_g)
    mix_pre, mix_post = _rows(mix_pre_g), _rows(mix_post_g)
    xa_pre, xa_post, mem_gain = _rows(xa_pre_g), _rows(xa_post_g), _rows(mem_g)
    conv_bias, b_a, b_i, lam = _rows(conv_b), _rows(lru_b_a), _rows(lru_b_i), _rows(lru_lambda)
    b_gk, norm_g, b_gate_r = _rows(gla_b_gk), _rows(gla_norm_g), _rows(b_gate)

    for l in range(depth):
        x2 = _ffn(x2, ffn1_pre, ffn1_up, ffn1_down, ffn1_post, l)

        proj, gk = _norm_matmul(x2, mix_pre, w_main, l, w_side=w_gk, name="mix_in_proj")
        y_a = _lru(proj, conv_w, conv_bias, w_ai, b_a, b_i, lam, l, batch, seq)
        y_b = _gla(proj, gk, w_gk2_p, b_gk, norm_g, l, batch, seq)
        x2 = _merge(x2, y_a, y_b, proj, b_gate_r, wb, wo, mix_post, l)

        kv = _norm_matmul(mem2, mem_gain, xa_kv, l, name="mem_kv_proj")
        x2 = _xattn(x2, xa_pre, xa_q, kv, xa_o, xa_post, l, batch, seq, mem_len)

        x2 = _ffn(x2, ffn2_pre, ffn2_up, ffn2_down, ffn2_post, l)

    return x2.reshape(batch, seq, d)
```

```python
import functools
import math

import jax
import jax.numpy as jnp
from jax import lax
from jax.experimental import pallas as pl
from jax.experimental.pallas import tpu as pltpu

F32 = jnp.float32
BF16 = jnp.bfloat16

D_MODEL = 2048
D_FF = 5504
FFN_RES_SCALE = 0.5
W_LRU = D_MODEL // 2
LRU_BLOCKS = 8
LRU_BW = W_LRU // LRU_BLOCKS
CONV_W = 4
LRU_C = 8.0
GLA_H = 4
GLA_DK = 128
GLA_DV = 256
GLA_RANK = 16
GLA_NORMALIZER = 16.0
GLA_CHUNK = 64
XA_H = 4
XA_DH = 128
N_BRANCH = 2
EPS = 1e-6

LANES = 128
SUBLANES = 8
V7X_VMEM_BYTES = 64 * 1024 * 1024
VMEM_LIMIT_CAP = V7X_VMEM_BYTES - 6 * 1024 * 1024

FFN_TF = 512
FFN_NF = -(-D_FF // FFN_TF)
FFN_TM = 1024
FFN_SLABS = 8
PROJ_TM = 1024
PROJ_TN = 1024
N_PROJ = N_BRANCH * D_MODEL + 2 * W_LRU + 2 * GLA_H * GLA_DV + 2 * GLA_H * GLA_DK
GK_PAD = LANES
LRU_T = 256
GLA_T = 256
MERGE_TM = 512
XA_TM = 512

COL_GATE = 0
COL_XLRU = (N_BRANCH * D_MODEL) // W_LRU
COL_GLRU = COL_XLRU + 1
COL_V = COL_GLRU + 1
COL_R = COL_V + 1
COL_Q = (N_BRANCH * D_MODEL + 2 * W_LRU + 2 * GLA_H * GLA_DV) // (GLA_H * GLA_DK)
COL_K = COL_Q + 1


def _params(semantics, vmem_bytes):
    return pltpu.CompilerParams(dimension_semantics=semantics,
                                vmem_limit_bytes=int(min(VMEM_LIMIT_CAP, vmem_bytes)))


def _layer_resident(tail, layer):
    zeros = (0,) * len(tail)
    return pl.BlockSpec((None,) + tuple(tail), lambda *_: (layer,) + zeros,
                        pipeline_mode=pl.Buffered(1))


def _rms(x, g):
    ms = jnp.mean(x * x, axis=-1, keepdims=True)
    return x * lax.rsqrt(ms + EPS) * g


def _silu(x):
    return x * jax.nn.sigmoid(x)


def _vector_zero_after(v):
    bits = pltpu.bitcast(v, jnp.int32)
    acc = bits[:, :LANES]
    for k in range(1, bits.shape[1] // LANES):
        acc = acc | bits[:, k * LANES:(k + 1) * LANES]
    out = acc[:SUBLANES]
    for k in range(1, acc.shape[0] // SUBLANES):
        out = out | acc[k * SUBLANES:(k + 1) * SUBLANES]
    return lax.shift_right_logical(lax.shift_right_logical(out, 16), 16)


def _ffn_window_start(j, base=0):
    return LANES * (base // LANES + jnp.minimum(j * (FFN_TF // LANES), (D_FF - FFN_TF) // LANES))


def _ffn_kernel(xnext_ref, xprev_ref, pre_g_ref, wg_ref, wu_ref, wd_ref, post_g_ref, o_ref,
                xn_even, xn_odd, acc_even, acc_odd, *, n_tiles):
    r = pl.program_id(0)
    j = pl.program_id(1)
    nf = pl.num_programs(1)
    slab = xnext_ref.shape[0]
    n_slabs = xn_even.shape[0] // slab
    row0 = pl.multiple_of(jnp.minimum(j, n_slabs - 1) * slab, slab)

    group = 2 * SUBLANES

    def pre_norm(xn_dst):
        tokens = []
        for g0 in range(0, slab, group):
            y = _rms(xnext_ref[g0:g0 + group, :], pre_g_ref[...]).astype(BF16)
            xn_dst[pl.ds(row0 + g0, group), :] = y
            tokens.append(_vector_zero_after(y))
        return jnp.max(functools.reduce(jnp.bitwise_or, tokens))

    def matmul_step(xn_src, acc, zero_pre, zero_post):
        rows = xn_src.shape[0]
        gate = jnp.dot(xn_src[...], wg_ref[...], preferred_element_type=F32)
        half = rows // 2
        xn_top = xn_src[pl.ds(pl.multiple_of(zero_pre, half), half), :]
        xn_bot = xn_src[pl.ds(pl.multiple_of(half + zero_post, half), half), :]
        up = jnp.concatenate([jnp.dot(xn_top, wu_ref[...], preferred_element_type=F32),
                              jnp.dot(xn_bot, wu_ref[...], preferred_element_type=F32)], axis=0)
        act = _silu(gate) * up
        covered = jnp.where(j == nf - 1, nf * FFN_TF - D_FF, 0)
        col = lax.broadcasted_iota(jnp.int32, act.shape, 1)
        act = jnp.where(col >= covered, act, 0.0).astype(BF16)
        prev = jnp.where(j == 0, 0.0, acc[...])
        acc[...] = prev + jnp.dot(act, wd_ref[...], preferred_element_type=F32)

    def post_norm(acc_src):
        tokens = []
        for g0 in range(0, slab, group):
            h = acc_src[pl.ds(row0 + g0, group), :]
            y = xprev_ref[g0:g0 + group, :] + FFN_RES_SCALE * _rms(h, post_g_ref[...])
            o_ref[g0:g0 + group, :] = y
            tokens.append(_vector_zero_after(y))
        return jnp.max(functools.reduce(jnp.bitwise_or, tokens))

    @pl.when(r == 0)
    def _():
        @pl.when(j == 0)
        def _():
            acc_even[...] = jnp.zeros_like(acc_even)
            acc_odd[...] = jnp.zeros_like(acc_odd)

        pre_norm(xn_even)

    steady = (r >= 1) & (r <= n_tiles)

    @pl.when(steady & (r % 2 == 1))
    def _():
        matmul_step(xn_even, acc_even, pre_norm(xn_odd), post_norm(acc_odd))

    @pl.when(steady & (r % 2 == 0))
    def _():
        matmul_step(xn_odd, acc_odd, pre_norm(xn_even), post_norm(acc_even))

    @pl.when(r == n_tiles + 1)
    def _():
        post_norm(acc_odd if (n_tiles - 1) % 2 else acc_even)


def _ffn(x2, pre_g, w_up, w_down, post_g, layer):
    m, d = x2.shape
    tm, tf = min(FFN_TM, m), FFN_TF
    n_tiles = m // tm
    slab = tm // FFN_SLABS
    vmem = (2 * tm * d * (2 + 4)
            + 2 * 3 * d * tf * 2
            + 3 * 2 * slab * d * 4
            + 2 * tm * d * 4 + 4 * tm * tf * 4)

    def next_slab(r, j):
        return (jnp.minimum(r, n_tiles - 1) * FFN_SLABS + jnp.minimum(j, FFN_SLABS - 1), 0)

    def prev_slab(r, j):
        tile = jnp.minimum(r - 2, n_tiles - 1)
        return (jnp.where(r < 2, 0, tile * FFN_SLABS + jnp.minimum(j, FFN_SLABS - 1)), 0)

    def window(r, j):
        return jnp.where(r == 0, 0, jnp.where(r == n_tiles + 1, FFN_NF - 1, j))

    return pl.pallas_call(
        functools.partial(_ffn_kernel, n_tiles=n_tiles),
        grid=(n_tiles + 2, FFN_NF),
        in_specs=[
            pl.BlockSpec((slab, d), next_slab),
            pl.BlockSpec((slab, d), prev_slab),
            _layer_resident((1, d), layer),
            pl.BlockSpec((None, pl.Element(d), pl.Element(tf)),
                         lambda r, j: (layer, 0, _ffn_window_start(window(r, j)))),
            pl.BlockSpec((None, pl.Element(d), pl.Element(tf)),
                         lambda r, j: (layer, 0, _ffn_window_start(window(r, j), base=D_FF))),
            pl.BlockSpec((None, pl.Element(tf), pl.Element(d)),
                         lambda r, j: (layer, _ffn_window_start(window(r, j)), 0)),
            _layer_resident((1, d), layer),
        ],
        out_specs=pl.BlockSpec((slab, d), prev_slab),
        out_shape=jax.ShapeDtypeStruct((m, d), F32),
        scratch_shapes=[pltpu.VMEM((tm, d), BF16), pltpu.VMEM((tm, d), BF16),
                        pltpu.VMEM((tm, d), F32), pltpu.VMEM((tm, d), F32)],
        compiler_params=_params(("arbitrary", "arbitrary"), vmem),
        name="ffn",
    )(x2, x2, pre_g, w_up, w_up, w_down, post_g)


def _norm_matmul_kernel(x_ref, g_ref, w_ref, o_ref, xn_ref):
    @pl.when(pl.program_id(1) == 0)
    def _():
        xn_ref[...] = _rms(x_ref[...], g_ref[...]).astype(BF16)

    o_ref[...] = jnp.dot(xn_ref[...], w_ref[...], preferred_element_type=F32).astype(o_ref.dtype)


def _norm_matmul_side_kernel(x_ref, g_ref, w_ref, ws_ref, o_ref, side_ref, xn_ref):
    @pl.when(pl.program_id(1) == 0)
    def _():
        xn = _rms(x_ref[...], g_ref[...]).astype(BF16)
        xn_ref[...] = xn
        side_ref[...] = jnp.dot(xn, ws_ref[...], preferred_element_type=F32)

    o_ref[...] = jnp.dot(xn_ref[...], w_ref[...], preferred_element_type=F32).astype(o_ref.dtype)


def _norm_matmul(x2, g, w, layer, w_side=None, name="norm_matmul"):
    m, d = x2.shape
    n = w.shape[2]
    tm, tn = min(PROJ_TM, m), min(PROJ_TN, n)
    vmem = (2 * tm * d * 4 + tm * d * 2 + 2 * d * tn * 2 + 2 * tm * tn * 2
            + 2 * tm * tn * 4 + tm * d * 4)
    in_specs = [
        pl.BlockSpec((tm, d), lambda i, j: (i, 0)),
        _layer_resident((1, d), layer),
        pl.BlockSpec((None, d, tn), lambda i, j: (layer, 0, j)),
    ]
    out_specs = pl.BlockSpec((tm, tn), lambda i, j: (i, j))
    out_shape = jax.ShapeDtypeStruct((m, n), BF16)
    args = [x2, g, w]
    kern = _norm_matmul_kernel
    if w_side is not None:
        ns = w_side.shape[2]
        in_specs.append(_layer_resident((d, ns), layer))
        out_specs = [out_specs, pl.BlockSpec((tm, ns), lambda i, j: (i, 0))]
        out_shape = [out_shape, jax.ShapeDtypeStruct((m, ns), F32)]
        args.append(w_side)
        kern = _norm_matmul_side_kernel
        vmem += d * ns * 2 + 2 * tm * ns * 4
    return pl.pallas_call(
        kern,
        grid=(m // tm, n // tn),
        in_specs=in_specs,
        out_specs=out_specs,
        out_shape=out_shape,
        scratch_shapes=[pltpu.VMEM((tm, d), BF16)],
        compiler_params=_params(("arbitrary", "arbitrary"), vmem),
        name=name,
    )(*args)


def _gelu_tanh(x):
    c = math.sqrt(2.0 / math.pi)
    return 0.5 * x * (1.0 + jnp.tanh(c * (x + 0.044715 * (x * x * x))))


def _softplus(x):
    return jnp.maximum(x, 0.0) + jnp.log1p(jnp.exp(-jnp.abs(x)))


def _lru_kernel(xl_ref, gl_ref, cw_ref, cb_ref, wai_ref, ba_ref, bi_ref, lam_ref, y_ref,
                tail_ref, h_ref):
    t_rows = xl_ref.shape[0]
    nblk = t_rows // SUBLANES

    @pl.when(pl.program_id(1) == 0)
    def _():
        tail_ref[...] = jnp.zeros_like(tail_ref)
        h_ref[...] = jnp.zeros_like(h_ref)

    row8 = lax.broadcasted_iota(jnp.int32, (SUBLANES, LRU_BW), 0)
    sub3 = lax.broadcasted_iota(jnp.int32, (nblk, SUBLANES, LRU_BW), 1)

    for n in range(LRU_BLOCKS):
        cs = slice(n * LRU_BW, (n + 1) * LRU_BW)
        x = xl_ref[:, cs].astype(F32)
        prev8 = tail_ref[:, cs]
        xc = x * cw_ref[CONV_W - 1:CONV_W, cs] + cb_ref[:, cs]
        for s in range(1, CONV_W):
            xs = pltpu.roll(x, s, 0)
            ps = pltpu.roll(prev8, s, 0)
            head = jnp.where(row8 < s, ps, xs[:SUBLANES])
            xs = jnp.concatenate([head, xs[SUBLANES:]], axis=0)
            xc = xc + xs * cw_ref[CONV_W - 1 - s:CONV_W - s, cs]
        tail_ref[:, cs] = x[t_rows - SUBLANES:]

        pre = jnp.dot(xc.astype(BF16), wai_ref[n], preferred_element_type=F32)
        r = jax.nn.sigmoid(pre[:, :LRU_BW] + ba_ref[:, cs])
        i = jax.nn.sigmoid(pre[:, LRU_BW:] + bi_ref[:, cs])
        log_a = (-LRU_C * _softplus(-lam_ref[:, cs])) * r
        a = jnp.exp(log_a)
        u = jnp.sqrt(-jnp.tanh(log_a) * (1.0 + a * a)) * (i * xc)

        a3 = a.reshape(nblk, SUBLANES, LRU_BW)
        u3 = u.reshape(nblk, SUBLANES, LRU_BW)
        for dd in (1, 2, 4):
            keep = sub3 >= dd
            a_s = jnp.where(keep, pltpu.roll(a3, dd, 1), 1.0)
            u_s = jnp.where(keep, pltpu.roll(u3, dd, 1), 0.0)
            u3 = a3 * u_s + u3
            a3 = a3 * a_s
        carry = jnp.broadcast_to(h_ref[:, cs], (SUBLANES, LRU_BW))
        hs = []
        for b in range(nblk):
            hb = u3[b] + a3[b] * carry
            hs.append(hb)
            carry = jnp.broadcast_to(hb[SUBLANES - 1:SUBLANES], (SUBLANES, LRU_BW))
        h = jnp.concatenate(hs, axis=0)
        h_ref[:, cs] = carry[0:1]

        y = h * _gelu_tanh(gl_ref[:, cs].astype(F32))
        y_ref[:, cs] = y.astype(y_ref.dtype)


def _lru(proj, conv_w, conv_b, w_ai, b_a, b_i, lam, layer, batch, seq):
    t = min(LRU_T, seq)
    nt = seq // t
    w = W_LRU
    vmem = 2 * 3 * t * w * 2 + LRU_BLOCKS * LRU_BW * 2 * LRU_BW * 2 + 64 * t * LRU_BW * 4
    return pl.pallas_call(
        _lru_kernel,
        grid=(batch, nt),
        in_specs=[
            pl.BlockSpec((t, w), lambda b, s: (b * nt + s, COL_XLRU)),
            pl.BlockSpec((t, w), lambda b, s: (b * nt + s, COL_GLRU)),
            _layer_resident((CONV_W, w), layer),
            _layer_resident((1, w), layer),
            _layer_resident((LRU_BLOCKS, LRU_BW, 2 * LRU_BW), layer),
            _layer_resident((1, w), layer),
            _layer_resident((1, w), layer),
            _layer_resident((1, w), layer),
        ],
        out_specs=pl.BlockSpec((t, w), lambda b, s: (b * nt + s, 0)),
        out_shape=jax.ShapeDtypeStruct((batch * seq, w), BF16),
        scratch_shapes=[pltpu.VMEM((SUBLANES, w), F32), pltpu.VMEM((1, w), F32)],
        compiler_params=_params(("arbitrary", "arbitrary"), vmem + (8 << 20)),
        name="rglru",
    )(proj, proj, conv_w, conv_b, w_ai, b_a, b_i, lam)


def _log_sigmoid(x):
    return jnp.minimum(x, 0.0) - jnp.log1p(jnp.exp(-jnp.abs(x)))


def _gla_kernel(q_ref, k_ref, v_ref, r_ref, gk_ref, wgk2_ref, bgk_ref, ng_ref, y_ref, st_ref):
    t_rows = q_ref.shape[0]
    c = GLA_CHUNK
    hk = GLA_H * GLA_DK

    @pl.when(pl.program_id(1) == 0)
    def _():
        st_ref[...] = jnp.zeros_like(st_ref)

    z = jnp.dot(gk_ref[...], wgk2_ref[...], preferred_element_type=F32,
                precision=lax.Precision.HIGHEST) + bgk_ref[...]
    la = _log_sigmoid(z) * (1.0 / GLA_NORMALIZER)
    pos = lax.broadcasted_iota(jnp.int32, (t_rows, hk), 0) & (c - 1)
    dd = 1
    while dd < c:
        la = la + jnp.where(pos >= dd, pltpu.roll(la, dd, 0), 0.0)
        dd *= 2
    bcum = la

    tril = (lax.broadcasted_iota(jnp.int32, (c, c), 0)
            >= lax.broadcasted_iota(jnp.int32, (c, c), 1))
    nt_dims = (((1,), (1,)), ((), ()))
    tn_dims = (((0,), (0,)), ((), ()))
    n_chunks = t_rows // c

    qe_c, g_c, o_intra, upd = [], [], [], []
    for ci in range(n_chunks):
        rows = slice(ci * c, (ci + 1) * c)
        bc = bcum[rows]
        b_last = bcum[(ci + 1) * c - 1:(ci + 1) * c]
        q = q_ref[rows, :].astype(F32) * (GLA_DK ** -0.5)
        k = k_ref[rows, :].astype(F32)
        qe = (q * jnp.exp(bc)).astype(BF16)
        ke = (k * jnp.exp(-bc)).astype(BF16)
        kd = (k * jnp.exp(b_last - bc)).astype(BF16)
        qe_c.append(qe)
        g_c.append(jnp.exp(b_last))
        o_h, upd_h = [], []
        for h in range(GLA_H):
            ks = slice(h * GLA_DK, (h + 1) * GLA_DK)
            v_h = v_ref[rows, h * GLA_DV:(h + 1) * GLA_DV]
            s = lax.dot_general(qe[:, ks], ke[:, ks], nt_dims, preferred_element_type=F32)
            s = jnp.where(tril, s, 0.0).astype(BF16)
            o_h.append(jnp.dot(s, v_h, preferred_element_type=F32))
            upd_h.append(lax.dot_general(v_h, kd[:, ks], tn_dims,
                                         preferred_element_type=F32))
        o_intra.append(o_h)
        upd.append(upd_h)

    for h in range(GLA_H):
        ks = slice(h * GLA_DK, (h + 1) * GLA_DK)
        vs = slice(h * GLA_DV, (h + 1) * GLA_DV)
        st = st_ref[h]
        for ci in range(n_chunks):
            rows = slice(ci * c, (ci + 1) * c)
            o = o_intra[ci][h] + lax.dot_general(qe_c[ci][:, ks], st.astype(BF16), nt_dims,
                                                 preferred_element_type=F32)
            st = st * g_c[ci][:, ks] + upd[ci][h]
            o = o * lax.rsqrt(jnp.mean(o * o, axis=-1, keepdims=True) + EPS) * ng_ref[...]
            o = o * _silu(r_ref[rows, vs].astype(F32))
            y_ref[rows, vs] = o.astype(y_ref.dtype)
        st_ref[h] = st


def _gla(proj, gk, w_gk2_p, b_gk, norm_g, layer, batch, seq):
    t = min(GLA_T, seq)
    nt = seq // t
    hk, hv = GLA_H * GLA_DK, GLA_H * GLA_DV
    vmem = (2 * (2 * t * hk * 2 + 3 * t * hv * 2 + t * GK_PAD * 4) + t * hk * 4
            + GLA_H * GLA_DV * GLA_DK * 4 + 8 * t * hk * 4)
    return pl.pallas_call(
        _gla_kernel,
        grid=(batch, nt),
        in_specs=[
            pl.BlockSpec((t, hk), lambda b, s: (b * nt + s, COL_Q)),
            pl.BlockSpec((t, hk), lambda b, s: (b * nt + s, COL_K)),
            pl.BlockSpec((t, hv), lambda b, s: (b * nt + s, COL_V)),
            pl.BlockSpec((t, hv), lambda b, s: (b * nt + s, COL_R)),
            pl.BlockSpec((t, GK_PAD), lambda b, s: (b * nt + s, 0)),
            _layer_resident((GK_PAD, hk), layer),
            _layer_resident((1, hk), layer),
            _layer_resident((1, GLA_DV), layer),
        ],
        out_specs=pl.BlockSpec((t, hv), lambda b, s: (b * nt + s, 0)),
        out_shape=jax.ShapeDtypeStruct((batch * seq, hv), BF16),
        scratch_shapes=[pltpu.VMEM((GLA_H, GLA_DV, GLA_DK), F32)],
        compiler_params=_params(("arbitrary", "arbitrary"), vmem + (8 << 20)),
        name="gla",
    )(proj, proj, proj, proj, gk, w_gk2_p, b_gk, norm_g)


def _merge_kernel(x_ref, ya_ref, yb_ref, gl_ref, bg_ref, wb_ref, wo_ref, post_g_ref, o_ref):
    d = x_ref.shape[1]
    z_a = jnp.dot(ya_ref[...], wb_ref[:W_LRU, :], preferred_element_type=F32)
    z_b = jnp.dot(yb_ref[...], wb_ref[W_LRU:, :], preferred_element_type=F32)
    g_a = jax.nn.sigmoid(gl_ref[:, :d].astype(F32) + bg_ref[:, :d])
    g_b = jax.nn.sigmoid(gl_ref[:, d:].astype(F32) + bg_ref[:, d:])
    merged = (g_a * z_a + g_b * z_b).astype(BF16)
    h = jnp.dot(merged, wo_ref[...], preferred_element_type=F32)
    o_ref[...] = x_ref[...] + _rms(h, post_g_ref[...])


def _merge(x2, y_a, y_b, proj, b_gate, w_branch, w_out, post_g, layer):
    m, d = x2.shape
    tm = min(MERGE_TM, m)
    wb_rows = w_branch.shape[1]
    vmem = (2 * 2 * tm * d * 4 + 2 * 2 * tm * W_LRU * 2 + 2 * tm * 2 * d * 2
            + wb_rows * d * 2 + d * d * 2 + 6 * tm * d * 4)
    return pl.pallas_call(
        _merge_kernel,
        grid=(m // tm,),
        in_specs=[
            pl.BlockSpec((tm, d), lambda i: (i, 0)),
            pl.BlockSpec((tm, W_LRU), lambda i: (i, 0)),
            pl.BlockSpec((tm, GLA_H * GLA_DV), lambda i: (i, 0)),
            pl.BlockSpec((tm, N_BRANCH * d), lambda i: (i, COL_GATE)),
            _layer_resident((1, N_BRANCH * d), layer),
            _layer_resident((wb_rows, d), layer),
            _layer_resident((d, d), layer),
            _layer_resident((1, d), layer),
        ],
        out_specs=pl.BlockSpec((tm, d), lambda i: (i, 0)),
        out_shape=jax.ShapeDtypeStruct((m, d), F32),
        compiler_params=_params(("arbitrary",), vmem),
        name="merge",
    )(x2, y_a, y_b, proj, b_gate, w_branch, w_out, post_g)


def _xattn_kernel(x_ref, pre_g_ref, wq_ref, kk_ref, vv_ref, wo_ref, post_g_ref, o_ref):
    xn = _rms(x_ref[...], pre_g_ref[...]).astype(BF16)
    q = jnp.dot(xn, wq_ref[...], preferred_element_type=F32).astype(BF16)
    nt_dims = (((1,), (1,)), ((), ()))
    outs = []
    for h in range(XA_H):
        hs = slice(h * XA_DH, (h + 1) * XA_DH)
        s = lax.dot_general(q[:, hs], kk_ref[:, hs], nt_dims,
                            preferred_element_type=F32) * (XA_DH ** -0.5)
        p = jnp.exp(s - jnp.max(s, axis=-1, keepdims=True))
        p = p / jnp.sum(p, axis=-1, keepdims=True)
        outs.append(jnp.dot(p.astype(BF16), vv_ref[:, hs], preferred_element_type=F32))
    o = jnp.concatenate(outs, axis=1).astype(BF16)
    h_out = jnp.dot(o, wo_ref[...], preferred_element_type=F32)
    o_ref[...] = x_ref[...] + _rms(h_out, post_g_ref[...])


def _xattn(x2, pre_g, w_q, kv, w_o, post_g, layer, batch, seq, mem_len):
    m, d = x2.shape
    tm = min(XA_TM, seq)
    nt = seq // tm
    hd = XA_H * XA_DH
    vmem = (2 * 2 * tm * d * 4 + 2 * d * hd * 2 + 2 * 2 * mem_len * hd * 2
            + 4 * tm * d * 4 + 8 * tm * mem_len * 4)
    return pl.pallas_call(
        _xattn_kernel,
        grid=(batch, nt),
        in_specs=[
            pl.BlockSpec((tm, d), lambda b, s: (b * nt + s, 0)),
            _layer_resident((1, d), layer),
            _layer_resident((d, hd), layer),
            pl.BlockSpec((mem_len, hd), lambda b, s: (b, 0)),
            pl.BlockSpec((mem_len, hd), lambda b, s: (b, 1)),
            _layer_resident((hd, d), layer),
            _layer_resident((1, d), layer),
        ],
        out_specs=pl.BlockSpec((tm, d), lambda b, s: (b * nt + s, 0)),
        out_shape=jax.ShapeDtypeStruct((m, d), F32),
        compiler_params=_params(("arbitrary", "arbitrary"), vmem),
        name="xattn",
    )(x2, pre_g, w_q, kv, kv, w_o, post_g)


def _pack_w_in(w_in):
    sizes = (W_LRU, W_LRU, GLA_H * GLA_DK, GLA_H * GLA_DK, GLA_H * GLA_DV, GLA_H * GLA_DV,
             GLA_RANK, N_BRANCH * D_MODEL)
    offs = [0]
    for s in sizes:
        offs.append(offs[-1] + s)
    x_lru, g_lru, q, k, v, r, gk, gate = (w_in[..., offs[i]:offs[i + 1]] for i in range(8))
    w_main = jnp.concatenate([gate, x_lru, g_lru, v, r, q, k], axis=-1).astype(BF16)
    w_gk = jnp.pad(gk, ((0, 0), (0, 0), (0, GK_PAD - GLA_RANK))).astype(BF16)
    return w_main, w_gk


def _rows(v):
    return v.reshape(v.shape[0], 1, -1)


def kernel(x, mem, ffn1_pre_g, ffn1_post_g, ffn1_w_up, ffn1_w_down, mix_pre_g, mix_post_g, w_in,
           conv_w, conv_b, lru_w_a, lru_b_a, lru_w_i, lru_b_i, lru_lambda, gla_w_gk2, gla_b_gk,
           gla_norm_g, b_gate, w_branch, w_out, xa_pre_g, xa_post_g, mem_g, xa_w_q, xa_w_kv,
           xa_w_o, ffn2_pre_g, ffn2_post_g, ffn2_w_up, ffn2_w_down):
    batch, seq, d = x.shape
    mem_len = mem.shape[1]
    depth = ffn1_w_up.shape[0]
    x2 = x.reshape(batch * seq, d)
    mem2 = mem.reshape(batch * mem_len, d)

    ffn1_up, ffn1_down = ffn1_w_up.astype(BF16), ffn1_w_down.astype(BF16)
    ffn2_up, ffn2_down = ffn2_w_up.astype(BF16), ffn2_w_down.astype(BF16)
    w_main, w_gk = _pack_w_in(w_in)
    w_ai = jnp.concatenate([lru_w_a, lru_w_i], axis=-1).astype(BF16)
    w_gk2_p = jnp.pad(gla_w_gk2, ((0, 0), (0, GK_PAD - GLA_RANK), (0, 0)))
    wb, wo = w_branch.astype(BF16), w_out.astype(BF16)
    xa_q, xa_kv, xa_o = xa_w_q.astype(BF16), xa_w_kv.astype(BF16), xa_w_o.astype(BF16)
    ffn1_pre, ffn1_post = _rows(ffn1_pre_g), _rows(ffn1_post_g)
    ffn2_pre, ffn2_post = _rows(ffn2_pre_g), _rows(ffn2_post_g)
    mix_pre, mix_post = _rows(mix_pre_g), _rows(mix_post_g)
    xa_pre, xa_post, mem_gain = _rows(xa_pre_g), _rows(xa_post_g), _rows(mem_g)
    conv_bias, b_a, b_i, lam = _rows(conv_b), _rows(lru_b_a), _rows(lru_b_i), _rows(lru_lambda)
    b_gk, norm_g, b_gate_r = _rows(gla_b_gk), _rows(gla_norm_g), _rows(b_gate)

    for l in range(depth):
        x2 = _ffn(x2, ffn1_pre, ffn1_up, ffn1_down, ffn1_post, l)

        proj, gk = _norm_matmul(x2, mix_pre, w_main, l, w_side=w_gk, name="mix_in_proj")
        y_a = _lru(proj, conv_w, conv_bias, w_ai, b_a, b_i, lam, l, batch, seq)
        y_b = _gla(proj, gk, w_gk2_p, b_gk, norm_g, l, batch, seq)
        x2 = _merge(x2, y_a, y_b, proj, b_gate_r, wb, wo, mix_post, l)

        kv = _norm_matmul(mem2, mem_gain, xa_kv, l, name="mem_kv_proj")
        x2 = _xattn(x2, xa_pre, xa_q, kv, xa_o, xa_post, l, batch, seq, mem_len)

        x2 = _ffn(x2, ffn2_pre, ffn2_up, ffn2_down, ffn2_post, l)

    return x2.reshape(batch, seq, d)
```

```python
import functools
import math

import jax
import jax.numpy as jnp
from jax import lax
from jax.experimental import pallas as pl
from jax.experimental.pallas import tpu as pltpu

F32 = jnp.float32
BF16 = jnp.bfloat16

D_MODEL = 2048
D_FF = 5504
FFN_RES_SCALE = 0.5
W_LRU = D_MODEL // 2
LRU_BLOCKS = 8
LRU_BW = W_LRU // LRU_BLOCKS
CONV_W = 4
LRU_C = 8.0
GLA_H = 4
GLA_DK = 128
GLA_DV = 256
GLA_RANK = 16
GLA_NORMALIZER = 16.0
GLA_CHUNK = 64
XA_H = 4
XA_DH = 128
N_BRANCH = 2
EPS = 1e-6

LANES = 128
SUBLANES = 8
V7X_VMEM_BYTES = 64 * 1024 * 1024
VMEM_LIMIT_CAP = V7X_VMEM_BYTES - 6 * 1024 * 1024

FFN_TF = 512
FFN_NF = -(-D_FF // FFN_TF)
FFN_TM = 1024
FFN_SLABS = 8
PROJ_TM = 1024
PROJ_TN = 1536
N_PROJ = N_BRANCH * D_MODEL + 2 * W_LRU + 2 * GLA_H * GLA_DV + 2 * GLA_H * GLA_DK
GK_PAD = LANES
LRU_T = 256
GLA_T = 256
MERGE_TM = 512
XA_TM = 512

COL_GATE = 0
COL_XLRU = (N_BRANCH * D_MODEL) // W_LRU
COL_GLRU = COL_XLRU + 1
COL_Q = (N_BRANCH * D_MODEL + 2 * W_LRU) // (GLA_H * GLA_DK)
COL_K = COL_Q + 1
COL_V = (N_BRANCH * D_MODEL + 2 * W_LRU + 2 * GLA_H * GLA_DK) // (GLA_H * GLA_DV)
COL_R = COL_V + 1


def _params(semantics, vmem_bytes):
    return pltpu.CompilerParams(dimension_semantics=semantics,
                                vmem_limit_bytes=int(min(VMEM_LIMIT_CAP, vmem_bytes)))


def _layer_resident(tail, layer):
    zeros = (0,) * len(tail)
    return pl.BlockSpec((None,) + tuple(tail), lambda *_: (layer,) + zeros,
                        pipeline_mode=pl.Buffered(1))


def _rms(x, g):
    ms = jnp.mean(x * x, axis=-1, keepdims=True)
    return x * lax.rsqrt(ms + EPS) * g


def _silu(x):
    return x * jax.nn.sigmoid(x)


def _vector_zero_after(v):
    bits = pltpu.bitcast(v, jnp.int32)
    acc = bits[:, :LANES]
    for k in range(1, bits.shape[1] // LANES):
        acc = acc | bits[:, k * LANES:(k + 1) * LANES]
    out = acc[:SUBLANES]
    for k in range(1, acc.shape[0] // SUBLANES):
        out = out | acc[k * SUBLANES:(k + 1) * SUBLANES]
    return lax.shift_right_logical(lax.shift_right_logical(out, 16), 16)


def _ffn_window_start(j, base=0):
    return LANES * (base // LANES + jnp.minimum(j * (FFN_TF // LANES), (D_FF - FFN_TF) // LANES))


def _ffn_kernel(xnext_ref, xprev_ref, pre_g_ref, wg_ref, wu_ref, wd_ref, post_g_ref, o_ref,
                xn_even, xn_odd, acc_even, acc_odd, *, n_tiles):
    r = pl.program_id(0)
    j = pl.program_id(1)
    nf = pl.num_programs(1)
    slab = xnext_ref.shape[0]
    n_slabs = xn_even.shape[0] // slab
    row0 = pl.multiple_of(jnp.minimum(j, n_slabs - 1) * slab, slab)

    group = 2 * SUBLANES

    def pre_norm(xn_dst):
        tokens = []
        for g0 in range(0, slab, group):
            y = _rms(xnext_ref[g0:g0 + group, :], pre_g_ref[...]).astype(BF16)
            xn_dst[pl.ds(row0 + g0, group), :] = y
            tokens.append(_vector_zero_after(y))
        return jnp.max(functools.reduce(jnp.bitwise_or, tokens))

    def matmul_step(xn_src, acc, zero_pre, zero_post):
        rows = xn_src.shape[0]
        gate = jnp.dot(xn_src[...], wg_ref[...], preferred_element_type=F32)
        half = rows // 2
        xn_top = xn_src[pl.ds(pl.multiple_of(zero_pre, half), half), :]
        xn_bot = xn_src[pl.ds(pl.multiple_of(half + zero_post, half), half), :]
        up = jnp.concatenate([jnp.dot(xn_top, wu_ref[...], preferred_element_type=F32),
                              jnp.dot(xn_bot, wu_ref[...], preferred_element_type=F32)], axis=0)
        act = _silu(gate) * up
        covered = jnp.where(j == nf - 1, nf * FFN_TF - D_FF, 0)
        col = lax.broadcasted_iota(jnp.int32, act.shape, 1)
        act = jnp.where(col >= covered, act, 0.0).astype(BF16)
        prev = jnp.where(j == 0, 0.0, acc[...])
        acc[...] = prev + jnp.dot(act, wd_ref[...], preferred_element_type=F32)

    def post_norm(acc_src):
        tokens = []
        for g0 in range(0, slab, group):
            h = acc_src[pl.ds(row0 + g0, group), :]
            y = xprev_ref[g0:g0 + group, :] + FFN_RES_SCALE * _rms(h, post_g_ref[...])
            o_ref[g0:g0 + group, :] = y
            tokens.append(_vector_zero_after(y))
        return jnp.max(functools.reduce(jnp.bitwise_or, tokens))

    @pl.when(r == 0)
    def _():
        @pl.when(j == 0)
        def _():
            acc_even[...] = jnp.zeros_like(acc_even)
            acc_odd[...] = jnp.zeros_like(acc_odd)

        pre_norm(xn_even)

    steady = (r >= 1) & (r <= n_tiles)

    @pl.when(steady & (r % 2 == 1))
    def _():
        matmul_step(xn_even, acc_even, pre_norm(xn_odd), post_norm(acc_odd))

    @pl.when(steady & (r % 2 == 0))
    def _():
        matmul_step(xn_odd, acc_odd, pre_norm(xn_even), post_norm(acc_even))

    @pl.when(r == n_tiles + 1)
    def _():
        post_norm(acc_odd if (n_tiles - 1) % 2 else acc_even)


def _ffn(x2, pre_g, w_up, w_down, post_g, layer):
    m, d = x2.shape
    tm, tf = min(FFN_TM, m), FFN_TF
    n_tiles = m // tm
    slab = tm // FFN_SLABS
    vmem = (2 * tm * d * (2 + 4)
            + 2 * 3 * d * tf * 2
            + 3 * 2 * slab * d * 4
            + 2 * tm * d * 4 + 4 * tm * tf * 4)

    def next_slab(r, j):
        return (jnp.minimum(r, n_tiles - 1) * FFN_SLABS + jnp.minimum(j, FFN_SLABS - 1), 0)

    def prev_slab(r, j):
        tile = jnp.minimum(r - 2, n_tiles - 1)
        return (jnp.where(r < 2, 0, tile * FFN_SLABS + jnp.minimum(j, FFN_SLABS - 1)), 0)

    def window(r, j):
        return jnp.where(r == 0, 0, jnp.where(r == n_tiles + 1, FFN_NF - 1, j))

    return pl.pallas_call(
        functools.partial(_ffn_kernel, n_tiles=n_tiles),
        grid=(n_tiles + 2, FFN_NF),
        in_specs=[
            pl.BlockSpec((slab, d), next_slab),
            pl.BlockSpec((slab, d), prev_slab),
            _layer_resident((1, d), layer),
            pl.BlockSpec((None, pl.Element(d), pl.Element(tf)),
                         lambda r, j: (layer, 0, _ffn_window_start(window(r, j)))),
            pl.BlockSpec((None, pl.Element(d), pl.Element(tf)),
                         lambda r, j: (layer, 0, _ffn_window_start(window(r, j), base=D_FF))),
            pl.BlockSpec((None, pl.Element(tf), pl.Element(d)),
                         lambda r, j: (layer, _ffn_window_start(window(r, j)), 0)),
            _layer_resident((1, d), layer),
        ],
        out_specs=pl.BlockSpec((slab, d), prev_slab),
        out_shape=jax.ShapeDtypeStruct((m, d), F32),
        scratch_shapes=[pltpu.VMEM((tm, d), BF16), pltpu.VMEM((tm, d), BF16),
                        pltpu.VMEM((tm, d), F32), pltpu.VMEM((tm, d), F32)],
        compiler_params=_params(("arbitrary", "arbitrary"), vmem),
        name="ffn",
    )(x2, x2, pre_g, w_up, w_up, w_down, post_g)


def _norm_matmul_kernel(x_ref, g_ref, w_ref, o_ref, xn_ref):
    @pl.when(pl.program_id(1) == 0)
    def _():
        xn_ref[...] = _rms(x_ref[...], g_ref[...]).astype(BF16)

    o_ref[...] = jnp.dot(xn_ref[...], w_ref[...], preferred_element_type=F32).astype(o_ref.dtype)


def _norm_matmul_side_kernel(x_ref, g_ref, w_ref, ws_ref, o_ref, side_ref, xn_ref):
    @pl.when(pl.program_id(1) == 0)
    def _():
        xn = _rms(x_ref[...], g_ref[...]).astype(BF16)
        xn_ref[...] = xn
        side_ref[...] = jnp.dot(xn, ws_ref[...], preferred_element_type=F32)

    o_ref[...] = jnp.dot(xn_ref[...], w_ref[...], preferred_element_type=F32).astype(o_ref.dtype)


def _norm_matmul(x2, g, w, layer, w_side=None, name="norm_matmul"):
    m, d = x2.shape
    n = w.shape[2]
    tm, tn = min(PROJ_TM, m), min(PROJ_TN, n)
    vmem = (2 * tm * d * 4 + tm * d * 2 + 2 * d * tn * 2 + 2 * tm * tn * 2
            + 2 * tm * tn * 4 + tm * d * 4)
    in_specs = [
        pl.BlockSpec((tm, d), lambda i, j: (i, 0)),
        _layer_resident((1, d), layer),
        pl.BlockSpec((None, d, tn), lambda i, j: (layer, 0, j)),
    ]
    out_specs = pl.BlockSpec((tm, tn), lambda i, j: (i, j))
    out_shape = jax.ShapeDtypeStruct((m, n), BF16)
    args = [x2, g, w]
    kern = _norm_matmul_kernel
    if w_side is not None:
        ns = w_side.shape[2]
        in_specs.append(_layer_resident((d, ns), layer))
        out_specs = [out_specs, pl.BlockSpec((tm, ns), lambda i, j: (i, 0))]
        out_shape = [out_shape, jax.ShapeDtypeStruct((m, ns), F32)]
        args.append(w_side)
        kern = _norm_matmul_side_kernel
        vmem += d * ns * 2 + 2 * tm * ns * 4
    return pl.pallas_call(
        kern,
        grid=(m // tm, n // tn),
        in_specs=in_specs,
        out_specs=out_specs,
        out_shape=out_shape,
        scratch_shapes=[pltpu.VMEM((tm, d), BF16)],
        compiler_params=_params(("arbitrary", "arbitrary"), vmem),
        name=name,
    )(*args)


def _gelu_tanh(x):
    c = math.sqrt(2.0 / math.pi)
    return 0.5 * x * (1.0 + jnp.tanh(c * (x + 0.044715 * (x * x * x))))


def _softplus(x):
    return jnp.maximum(x, 0.0) + jnp.log1p(jnp.exp(-jnp.abs(x)))


def _lru_kernel(xl_ref, gl_ref, cw_ref, cb_ref, wai_ref, ba_ref, bi_ref, lam_ref, y_ref,
                tail_ref, h_ref):
    t_rows = xl_ref.shape[0]
    nblk = t_rows // SUBLANES

    @pl.when(pl.program_id(1) == 0)
    def _():
        tail_ref[...] = jnp.zeros_like(tail_ref)
        h_ref[...] = jnp.zeros_like(h_ref)

    row8 = lax.broadcasted_iota(jnp.int32, (SUBLANES, LRU_BW), 0)
    sub3 = lax.broadcasted_iota(jnp.int32, (nblk, SUBLANES, LRU_BW), 1)

    for n in range(LRU_BLOCKS):
        cs = slice(n * LRU_BW, (n + 1) * LRU_BW)
        x = xl_ref[:, cs].astype(F32)
        prev8 = tail_ref[:, cs]
        xc = x * cw_ref[CONV_W - 1:CONV_W, cs] + cb_ref[:, cs]
        for s in range(1, CONV_W):
            xs = pltpu.roll(x, s, 0)
            ps = pltpu.roll(prev8, s, 0)
            head = jnp.where(row8 < s, ps, xs[:SUBLANES])
            xs = jnp.concatenate([head, xs[SUBLANES:]], axis=0)
            xc = xc + xs * cw_ref[CONV_W - 1 - s:CONV_W - s, cs]
        tail_ref[:, cs] = x[t_rows - SUBLANES:]

        pre = jnp.dot(xc.astype(BF16), wai_ref[n], preferred_element_type=F32)
        r = jax.nn.sigmoid(pre[:, :LRU_BW] + ba_ref[:, cs])
        i = jax.nn.sigmoid(pre[:, LRU_BW:] + bi_ref[:, cs])
        log_a = (-LRU_C * _softplus(-lam_ref[:, cs])) * r
        a = jnp.exp(log_a)
        u = jnp.sqrt(-jnp.tanh(log_a) * (1.0 + a * a)) * (i * xc)

        a3 = a.reshape(nblk, SUBLANES, LRU_BW)
        u3 = u.reshape(nblk, SUBLANES, LRU_BW)
        for dd in (1, 2, 4):
            keep = sub3 >= dd
            a_s = jnp.where(keep, pltpu.roll(a3, dd, 1), 1.0)
            u_s = jnp.where(keep, pltpu.roll(u3, dd, 1), 0.0)
            u3 = a3 * u_s + u3
            a3 = a3 * a_s
        carry = jnp.broadcast_to(h_ref[:, cs], (SUBLANES, LRU_BW))
        hs = []
        for b in range(nblk):
            hb = u3[b] + a3[b] * carry
            hs.append(hb)
            carry = jnp.broadcast_to(hb[SUBLANES - 1:SUBLANES], (SUBLANES, LRU_BW))
        h = jnp.concatenate(hs, axis=0)
        h_ref[:, cs] = carry[0:1]

        y = h * _gelu_tanh(gl_ref[:, cs].astype(F32))
        y_ref[:, cs] = y.astype(y_ref.dtype)


def _lru(proj, conv_w, conv_b, w_ai, b_a, b_i, lam, layer, batch, seq):
    t = min(LRU_T, seq)
    nt = seq // t
    w = W_LRU
    vmem = 2 * 3 * t * w * 2 + LRU_BLOCKS * LRU_BW * 2 * LRU_BW * 2 + 64 * t * LRU_BW * 4
    return pl.pallas_call(
        _lru_kernel,
        grid=(batch, nt),
        in_specs=[
            pl.BlockSpec((t, w), lambda b, s: (b * nt + s, COL_XLRU)),
            pl.BlockSpec((t, w), lambda b, s: (b * nt + s, COL_GLRU)),
            _layer_resident((CONV_W, w), layer),
            _layer_resident((1, w), layer),
            _layer_resident((LRU_BLOCKS, LRU_BW, 2 * LRU_BW), layer),
            _layer_resident((1, w), layer),
            _layer_resident((1, w), layer),
            _layer_resident((1, w), layer),
        ],
        out_specs=pl.BlockSpec((t, w), lambda b, s: (b * nt + s, 0)),
        out_shape=jax.ShapeDtypeStruct((batch * seq, w), BF16),
        scratch_shapes=[pltpu.VMEM((SUBLANES, w), F32), pltpu.VMEM((1, w), F32)],
        compiler_params=_params(("arbitrary", "arbitrary"), vmem + (8 << 20)),
        name="rglru",
    )(proj, proj, conv_w, conv_b, w_ai, b_a, b_i, lam)


def _log_sigmoid(x):
    return jnp.minimum(x, 0.0) - jnp.log1p(jnp.exp(-jnp.abs(x)))


def _gla_kernel(q_ref, k_ref, v_ref, r_ref, gk_ref, wgk2_ref, bgk_ref, ng_ref, y_ref, st_ref):
    t_rows = q_ref.shape[0]
    c = GLA_CHUNK
    hk = GLA_H * GLA_DK

    @pl.when(pl.program_id(1) == 0)
    def _():
        st_ref[...] = jnp.zeros_like(st_ref)

    z = jnp.dot(gk_ref[...], wgk2_ref[...], preferred_element_type=F32,
                precision=lax.Precision.HIGHEST) + bgk_ref[...]
    la = _log_sigmoid(z) * (1.0 / GLA_NORMALIZER)
    pos = lax.broadcasted_iota(jnp.int32, (t_rows, hk), 0) & (c - 1)
    dd = 1
    while dd < c:
        la = la + jnp.where(pos >= dd, pltpu.roll(la, dd, 0), 0.0)
        dd *= 2
    bcum = la

    tril = (lax.broadcasted_iota(jnp.int32, (c, c), 0)
            >= lax.broadcasted_iota(jnp.int32, (c, c), 1))
    nt_dims = (((1,), (1,)), ((), ()))
    tn_dims = (((0,), (0,)), ((), ()))
    n_chunks = t_rows // c

    qe_c, g_c, o_intra, upd = [], [], [], []
    for ci in range(n_chunks):
        rows = slice(ci * c, (ci + 1) * c)
        bc = bcum[rows]
        b_last = bcum[(ci + 1) * c - 1:(ci + 1) * c]
        q = q_ref[rows, :].astype(F32) * (GLA_DK ** -0.5)
        k = k_ref[rows, :].astype(F32)
        qe = (q * jnp.exp(bc)).astype(BF16)
        ke = (k * jnp.exp(-bc)).astype(BF16)
        kd = (k * jnp.exp(b_last - bc)).astype(BF16)
        qe_c.append(qe)
        g_c.append(jnp.exp(b_last))
        o_h, upd_h = [], []
        for h in range(GLA_H):
            ks = slice(h * GLA_DK, (h + 1) * GLA_DK)
            v_h = v_ref[rows, h * GLA_DV:(h + 1) * GLA_DV]
            s = lax.dot_general(qe[:, ks], ke[:, ks], nt_dims, preferred_element_type=F32)
            s = jnp.where(tril, s, 0.0).astype(BF16)
            o_h.append(jnp.dot(s, v_h, preferred_element_type=F32))
            upd_h.append(lax.dot_general(v_h, kd[:, ks], tn_dims,
                                         preferred_element_type=F32))
        o_intra.append(o_h)
        upd.append(upd_h)

    for h in range(GLA_H):
        ks = slice(h * GLA_DK, (h + 1) * GLA_DK)
        vs = slice(h * GLA_DV, (h + 1) * GLA_DV)
        st = st_ref[h]
        for ci in range(n_chunks):
            rows = slice(ci * c, (ci + 1) * c)
            o = o_intra[ci][h] + lax.dot_general(qe_c[ci][:, ks], st.astype(BF16), nt_dims,
                                                 preferred_element_type=F32)
            st = st * g_c[ci][:, ks] + upd[ci][h]
            o = o * lax.rsqrt(jnp.mean(o * o, axis=-1, keepdims=True) + EPS) * ng_ref[...]
            o = o * _silu(r_ref[rows, vs].astype(F32))
            y_ref[rows, vs] = o.astype(y_ref.dtype)
        st_ref[h] = st


def _gla(proj, gk, w_gk2_p, b_gk, norm_g, layer, batch, seq):
    t = min(GLA_T, seq)
    nt = seq // t
    hk, hv = GLA_H * GLA_DK, GLA_H * GLA_DV
    vmem = (2 * (2 * t * hk * 2 + 3 * t * hv * 2 + t * GK_PAD * 4) + t * hk * 4
            + GLA_H * GLA_DV * GLA_DK * 4 + 8 * t * hk * 4)
    return pl.pallas_call(
        _gla_kernel,
        grid=(batch, nt),
        in_specs=[
            pl.BlockSpec((t, hk), lambda b, s: (b * nt + s, COL_Q)),
            pl.BlockSpec((t, hk), lambda b, s: (b * nt + s, COL_K)),
            pl.BlockSpec((t, hv), lambda b, s: (b * nt + s, COL_V)),
            pl.BlockSpec((t, hv), lambda b, s: (b * nt + s, COL_R)),
            pl.BlockSpec((t, GK_PAD), lambda b, s: (b * nt + s, 0)),
            _layer_resident((GK_PAD, hk), layer),
            _layer_resident((1, hk), layer),
            _layer_resident((1, GLA_DV), layer),
        ],
        out_specs=pl.BlockSpec((t, hv), lambda b, s: (b * nt + s, 0)),
        out_shape=jax.ShapeDtypeStruct((batch * seq, hv), BF16),
        scratch_shapes=[pltpu.VMEM((GLA_H, GLA_DV, GLA_DK), F32)],
        compiler_params=_params(("arbitrary", "arbitrary"), vmem + (8 << 20)),
        name="gla",
    )(proj, proj, proj, proj, gk, w_gk2_p, b_gk, norm_g)


def _merge_kernel(x_ref, ya_ref, yb_ref, gl_ref, bg_ref, wb_ref, wo_ref, post_g_ref, o_ref):
    d = x_ref.shape[1]
    z_a = jnp.dot(ya_ref[...], wb_ref[:W_LRU, :], preferred_element_type=F32)
    z_b = jnp.dot(yb_ref[...], wb_ref[W_LRU:, :], preferred_element_type=F32)
    g_a = jax.nn.sigmoid(gl_ref[:, :d].astype(F32) + bg_ref[:, :d])
    g_b = jax.nn.sigmoid(gl_ref[:, d:].astype(F32) + bg_ref[:, d:])
    merged = (g_a * z_a + g_b * z_b).astype(BF16)
    h = jnp.dot(merged, wo_ref[...], preferred_element_type=F32)
    o_ref[...] = x_ref[...] + _rms(h, post_g_ref[...])


def _merge(x2, y_a, y_b, proj, b_gate, w_branch, w_out, post_g, layer):
    m, d = x2.shape
    tm = min(MERGE_TM, m)
    wb_rows = w_branch.shape[1]
    vmem = (2 * 2 * tm * d * 4 + 2 * 2 * tm * W_LRU * 2 + 2 * tm * 2 * d * 2
            + wb_rows * d * 2 + d * d * 2 + 6 * tm * d * 4)
    return pl.pallas_call(
        _merge_kernel,
        grid=(m // tm,),
        in_specs=[
            pl.BlockSpec((tm, d), lambda i: (i, 0)),
            pl.BlockSpec((tm, W_LRU), lambda i: (i, 0)),
            pl.BlockSpec((tm, GLA_H * GLA_DV), lambda i: (i, 0)),
            pl.BlockSpec((tm, N_BRANCH * d), lambda i: (i, COL_GATE)),
            _layer_resident((1, N_BRANCH * d), layer),
            _layer_resident((wb_rows, d), layer),
            _layer_resident((d, d), layer),
            _layer_resident((1, d), layer),
        ],
        out_specs=pl.BlockSpec((tm, d), lambda i: (i, 0)),
        out_shape=jax.ShapeDtypeStruct((m, d), F32),
        compiler_params=_params(("arbitrary",), vmem),
        name="merge",
    )(x2, y_a, y_b, proj, b_gate, w_branch, w_out, post_g)


def _xattn_kernel(x_ref, pre_g_ref, wq_ref, kk_ref, vv_ref, wo_ref, post_g_ref, o_ref):
    xn = _rms(x_ref[...], pre_g_ref[...]).astype(BF16)
    q = (jnp.dot(xn, wq_ref[...], preferred_element_type=F32) * (XA_DH ** -0.5)).astype(BF16)
    nt_dims = (((1,), (1,)), ((), ()))
    outs = []
    for h in range(XA_H):
        hs = slice(h * XA_DH, (h + 1) * XA_DH)
        s = lax.dot_general(q[:, hs], kk_ref[:, hs], nt_dims,
                            preferred_element_type=F32)
        p = jnp.exp(s - jnp.max(s, axis=-1, keepdims=True))
        pv = jnp.dot(p.astype(BF16), vv_ref[:, hs], preferred_element_type=F32)
        outs.append(pv / jnp.sum(p, axis=-1, keepdims=True))
    o = jnp.concatenate(outs, axis=1).astype(BF16)
    h_out = jnp.dot(o, wo_ref[...], preferred_element_type=F32)
    o_ref[...] = x_ref[...] + _rms(h_out, post_g_ref[...])


def _xattn(x2, pre_g, w_q, kv, w_o, post_g, layer, batch, seq, mem_len):
    m, d = x2.shape
    tm = min(XA_TM, seq)
    nt = seq // tm
    hd = XA_H * XA_DH
    vmem = (2 * 2 * tm * d * 4 + 2 * d * hd * 2 + 2 * 2 * mem_len * hd * 2
            + 4 * tm * d * 4 + 8 * tm * mem_len * 4)
    return pl.pallas_call(
        _xattn_kernel,
        grid=(batch, nt),
        in_specs=[
            pl.BlockSpec((tm, d), lambda b, s: (b * nt + s, 0)),
            _layer_resident((1, d), layer),
            _layer_resident((d, hd), layer),
            pl.BlockSpec((mem_len, hd), lambda b, s: (b, 0)),
            pl.BlockSpec((mem_len, hd), lambda b, s: (b, 1)),
            _layer_resident((hd, d), layer),
            _layer_resident((1, d), layer),
        ],
        out_specs=pl.BlockSpec((tm, d), lambda b, s: (b * nt + s, 0)),
        out_shape=jax.ShapeDtypeStruct((m, d), F32),
        compiler_params=_params(("arbitrary", "arbitrary"), vmem),
        name="xattn",
    )(x2, pre_g, w_q, kv, kv, w_o, post_g)


def _pack_w_in(w_in):
    n_head = 2 * W_LRU + 2 * GLA_H * GLA_DK + 2 * GLA_H * GLA_DV
    head = w_in[..., :n_head]
    gk = w_in[..., n_head:n_head + GLA_RANK]
    gate = w_in[..., n_head + GLA_RANK:]
    w_main = jnp.concatenate([gate, head], axis=-1).astype(BF16)
    w_gk = jnp.pad(gk, ((0, 0), (0, 0), (0, GK_PAD - GLA_RANK))).astype(BF16)
    return w_main, w_gk


def _rows(v):
    return v.reshape(v.shape[0], 1, -1)


def kernel(x, mem, ffn1_pre_g, ffn1_post_g, ffn1_w_up, ffn1_w_down, mix_pre_g, mix_post_g, w_in,
           conv_w, conv_b, lru_w_a, lru_b_a, lru_w_i, lru_b_i, lru_lambda, gla_w_gk2, gla_b_gk,
           gla_norm_g, b_gate, w_branch, w_out, xa_pre_g, xa_post_g, mem_g, xa_w_q, xa_w_kv,
           xa_w_o, ffn2_pre_g, ffn2_post_g, ffn2_w_up, ffn2_w_down):
    batch, seq, d = x.shape
    mem_len = mem.shape[1]
    depth = ffn1_w_up.shape[0]
    x2 = x.reshape(batch * seq, d)
    mem2 = mem.reshape(batch * mem_len, d)

    ffn1_up, ffn1_down = ffn1_w_up.astype(BF16), ffn1_w_down.astype(BF16)
    ffn2_up, ffn2_down = ffn2_w_up.astype(BF16), ffn2_w_down.astype(BF16)
    w_main, w_gk = _pack_w_in(w_in)
    w_ai = jnp.concatenate([lru_w_a, lru_w_i], axis=-1).astype(BF16)
    w_gk2_p = jnp.pad(gla_w_gk2, ((0, 0), (0, GK_PAD - GLA_RANK), (0, 0)))
    wb, wo = w_branch.astype(BF16), w_out.astype(BF16)
    xa_q, xa_kv, xa_o = xa_w_q.astype(BF16), xa_w_kv.astype(BF16), xa_w_o.astype(BF16)
    ffn1_pre, ffn1_post = _rows(ffn1_pre_g), _rows(ffn1_post_g)
    ffn2_pre, ffn2_post = _rows(ffn2_pre_g), _rows(ffn2_post_g)
    mix_pre, mix_post = _rows(mix_pre_g), _rows(mix_post_g)
    xa_pre, xa_post, mem_gain = _rows(xa_pre_g), _rows(xa_post_g), _rows(mem_g)
    conv_bias, b_a, b_i, lam = _rows(conv_b), _rows(lru_b_a), _rows(lru_b_i), _rows(lru_lambda)
    b_gk, norm_g, b_gate_r = _rows(gla_b_gk), _rows(gla_norm_g), _rows(b_gate)

    for l in range(depth):
        x2 = _ffn(x2, ffn1_pre, ffn1_up, ffn1_down, ffn1_post, l)

        proj, gk = _norm_matmul(x2, mix_pre, w_main, l, w_side=w_gk, name="mix_in_proj")
        y_a = _lru(proj, conv_w, conv_bias, w_ai, b_a, b_i, lam, l, batch, seq)
        y_b = _gla(proj, gk, w_gk2_p, b_gk, norm_g, l, batch, seq)
        x2 = _merge(x2, y_a, y_b, proj, b_gate_r, wb, wo, mix_post, l)

        kv = _norm_matmul(mem2, mem_gain, xa_kv, l, name="mem_kv_proj")
        x2 = _xattn(x2, xa_pre, xa_q, kv, xa_o, xa_post, l, batch, seq, mem_len)

        x2 = _ffn(x2, ffn2_pre, ffn2_up, ffn2_down, ffn2_post, l)

    return x2.reshape(batch, seq, d)
```

```python
import functools
import math

import jax
import jax.numpy as jnp
from jax import lax
from jax.experimental import pallas as pl
from jax.experimental.pallas import tpu as pltpu

F32 = jnp.float32
BF16 = jnp.bfloat16

D_MODEL = 2048
D_FF = 5504
FFN_RES_SCALE = 0.5
W_LRU = D_MODEL // 2
LRU_BLOCKS = 8
LRU_BW = W_LRU // LRU_BLOCKS
CONV_W = 4
LRU_C = 8.0
GLA_H = 4
GLA_DK = 128
GLA_DV = 256
GLA_RANK = 16
GLA_NORMALIZER = 16.0
GLA_CHUNK = 64
XA_H = 4
XA_DH = 128
N_BRANCH = 2
EPS = 1e-6

LANES = 128
SUBLANES = 8
V7X_VMEM_BYTES = 64 * 1024 * 1024
VMEM_LIMIT_CAP = V7X_VMEM_BYTES - 6 * 1024 * 1024

FFN_TF = 512
FFN_NF = -(-D_FF // FFN_TF)
FFN_TM = 1024
FFN_SLABS = 8
PROJ_TM = 1024
PROJ_TN = 2304
N_PROJ = N_BRANCH * D_MODEL + 2 * W_LRU + 2 * GLA_H * GLA_DV + 2 * GLA_H * GLA_DK
GK_PAD = LANES
LRU_T = 256
GLA_T = 512
MERGE_TM = 512
XA_TM = 512

COL_GATE = 0
COL_XLRU = (N_BRANCH * D_MODEL) // W_LRU
COL_GLRU = COL_XLRU + 1
COL_Q = (N_BRANCH * D_MODEL + 2 * W_LRU) // (GLA_H * GLA_DK)
COL_K = COL_Q + 1
COL_V = (N_BRANCH * D_MODEL + 2 * W_LRU + 2 * GLA_H * GLA_DK) // (GLA_H * GLA_DV)
COL_R = COL_V + 1


def _params(semantics, vmem_bytes):
    return pltpu.CompilerParams(dimension_semantics=semantics,
                                vmem_limit_bytes=int(min(VMEM_LIMIT_CAP, vmem_bytes)))


def _layer_resident(tail, layer):
    zeros = (0,) * len(tail)
    return pl.BlockSpec((None,) + tuple(tail), lambda *_: (layer,) + zeros,
                        pipeline_mode=pl.Buffered(1))


def _rms(x, g):
    ms = jnp.mean(x * x, axis=-1, keepdims=True)
    return x * lax.rsqrt(ms + EPS) * g


def _silu(x):
    return x * jax.nn.sigmoid(x)


def _vector_zero_after(v):
    bits = pltpu.bitcast(v, jnp.int32)
    acc = bits[:, :LANES]
    for k in range(1, bits.shape[1] // LANES):
        acc = acc | bits[:, k * LANES:(k + 1) * LANES]
    out = acc[:SUBLANES]
    for k in range(1, acc.shape[0] // SUBLANES):
        out = out | acc[k * SUBLANES:(k + 1) * SUBLANES]
    return lax.shift_right_logical(lax.shift_right_logical(out, 16), 16)


def _ffn_window_start(j, base=0):
    return LANES * (base // LANES + jnp.minimum(j * (FFN_TF // LANES), (D_FF - FFN_TF) // LANES))


def _ffn_kernel(xnext_ref, xprev_ref, pre_g_ref, wg_ref, wu_ref, wd_ref, post_g_ref, o_ref,
                xn_even, xn_odd, acc_even, acc_odd, *, n_tiles):
    r = pl.program_id(0)
    j = pl.program_id(1)
    nf = pl.num_programs(1)
    slab = xnext_ref.shape[0]
    n_slabs = xn_even.shape[0] // slab
    row0 = pl.multiple_of(jnp.minimum(j, n_slabs - 1) * slab, slab)

    group = 2 * SUBLANES

    def pre_norm(xn_dst):
        tokens = []
        for g0 in range(0, slab, group):
            y = _rms(xnext_ref[g0:g0 + group, :], pre_g_ref[...]).astype(BF16)
            xn_dst[pl.ds(row0 + g0, group), :] = y
            tokens.append(_vector_zero_after(y))
        return jnp.max(functools.reduce(jnp.bitwise_or, tokens))

    def matmul_step(xn_src, acc, zero_pre, zero_post):
        rows = xn_src.shape[0]
        gate = jnp.dot(xn_src[...], wg_ref[...], preferred_element_type=F32)
        half = rows // 2
        xn_top = xn_src[pl.ds(pl.multiple_of(zero_pre, half), half), :]
        xn_bot = xn_src[pl.ds(pl.multiple_of(half + zero_post, half), half), :]
        up = jnp.concatenate([jnp.dot(xn_top, wu_ref[...], preferred_element_type=F32),
                              jnp.dot(xn_bot, wu_ref[...], preferred_element_type=F32)], axis=0)
        act = _silu(gate) * up
        covered = jnp.where(j == nf - 1, nf * FFN_TF - D_FF, 0)
        col = lax.broadcasted_iota(jnp.int32, act.shape, 1)
        act = jnp.where(col >= covered, act, 0.0).astype(BF16)
        prev = jnp.where(j == 0, 0.0, acc[...])
        acc[...] = prev + jnp.dot(act, wd_ref[...], preferred_element_type=F32)

    def post_norm(acc_src):
        tokens = []
        for g0 in range(0, slab, group):
            h = acc_src[pl.ds(row0 + g0, group), :]
            y = xprev_ref[g0:g0 + group, :] + FFN_RES_SCALE * _rms(h, post_g_ref[...])
            o_ref[g0:g0 + group, :] = y
            tokens.append(_vector_zero_after(y))
        return jnp.max(functools.reduce(jnp.bitwise_or, tokens))

    @pl.when(r == 0)
    def _():
        @pl.when(j == 0)
        def _():
            acc_even[...] = jnp.zeros_like(acc_even)
            acc_odd[...] = jnp.zeros_like(acc_odd)

        pre_norm(xn_even)

    steady = (r >= 1) & (r <= n_tiles)

    @pl.when(steady & (r % 2 == 1))
    def _():
        matmul_step(xn_even, acc_even, pre_norm(xn_odd), post_norm(acc_odd))

    @pl.when(steady & (r % 2 == 0))
    def _():
        matmul_step(xn_odd, acc_odd, pre_norm(xn_even), post_norm(acc_even))

    @pl.when(r == n_tiles + 1)
    def _():
        post_norm(acc_odd if (n_tiles - 1) % 2 else acc_even)


def _ffn(x2, pre_g, w_up, w_down, post_g, layer):
    m, d = x2.shape
    tm, tf = min(FFN_TM, m), FFN_TF
    n_tiles = m // tm
    slab = tm // FFN_SLABS
    vmem = (2 * tm * d * (2 + 4)
            + 2 * 3 * d * tf * 2
            + 3 * 2 * slab * d * 4
            + 2 * tm * d * 4 + 4 * tm * tf * 4)

    def next_slab(r, j):
        return (jnp.minimum(r, n_tiles - 1) * FFN_SLABS + jnp.minimum(j, FFN_SLABS - 1), 0)

    def prev_slab(r, j):
        tile = jnp.minimum(r - 2, n_tiles - 1)
        return (jnp.where(r < 2, 0, tile * FFN_SLABS + jnp.minimum(j, FFN_SLABS - 1)), 0)

    def window(r, j):
        return jnp.where(r == 0, 0, jnp.where(r == n_tiles + 1, FFN_NF - 1, j))

    return pl.pallas_call(
        functools.partial(_ffn_kernel, n_tiles=n_tiles),
        grid=(n_tiles + 2, FFN_NF),
        in_specs=[
            pl.BlockSpec((slab, d), next_slab),
            pl.BlockSpec((slab, d), prev_slab),
            _layer_resident((1, d), layer),
            pl.BlockSpec((None, pl.Element(d), pl.Element(tf)),
                         lambda r, j: (layer, 0, _ffn_window_start(window(r, j)))),
            pl.BlockSpec((None, pl.Element(d), pl.Element(tf)),
                         lambda r, j: (layer, 0, _ffn_window_start(window(r, j), base=D_FF))),
            pl.BlockSpec((None, pl.Element(tf), pl.Element(d)),
                         lambda r, j: (layer, _ffn_window_start(window(r, j)), 0)),
            _layer_resident((1, d), layer),
        ],
        out_specs=pl.BlockSpec((slab, d), prev_slab),
        out_shape=jax.ShapeDtypeStruct((m, d), F32),
        scratch_shapes=[pltpu.VMEM((tm, d), BF16), pltpu.VMEM((tm, d), BF16),
                        pltpu.VMEM((tm, d), F32), pltpu.VMEM((tm, d), F32)],
        compiler_params=_params(("arbitrary", "arbitrary"), vmem),
        name="ffn",
    )(x2, x2, pre_g, w_up, w_up, w_down, post_g)


def _norm_matmul_kernel(x_ref, g_ref, w_ref, o_ref, xn_ref):
    @pl.when(pl.program_id(1) == 0)
    def _():
        xn_ref[...] = _rms(x_ref[...], g_ref[...]).astype(BF16)

    o_ref[...] = jnp.dot(xn_ref[...], w_ref[...], preferred_element_type=F32).astype(o_ref.dtype)


def _norm_matmul_side_kernel(x_ref, g_ref, w_ref, ws_ref, o_ref, side_ref, xn_ref):
    @pl.when(pl.program_id(1) == 0)
    def _():
        xn = _rms(x_ref[...], g_ref[...]).astype(BF16)
        xn_ref[...] = xn
        side_ref[...] = jnp.dot(xn, ws_ref[...], preferred_element_type=F32)

    o_ref[...] = jnp.dot(xn_ref[...], w_ref[...], preferred_element_type=F32).astype(o_ref.dtype)


def _norm_matmul(x2, g, w, layer, w_side=None, name="norm_matmul"):
    m, d = x2.shape
    n = w.shape[2]
    tm, tn = min(PROJ_TM, m), min(PROJ_TN, n)
    vmem = (2 * tm * d * 4 + tm * d * 2 + 2 * d * tn * 2 + 2 * tm * tn * 2
            + 2 * tm * tn * 4 + tm * d * 4)
    in_specs = [
        pl.BlockSpec((tm, d), lambda i, j: (i, 0)),
        _layer_resident((1, d), layer),
        pl.BlockSpec((None, d, tn), lambda i, j: (layer, 0, j)),
    ]
    out_specs = pl.BlockSpec((tm, tn), lambda i, j: (i, j))
    out_shape = jax.ShapeDtypeStruct((m, n), BF16)
    args = [x2, g, w]
    kern = _norm_matmul_kernel
    if w_side is not None:
        ns = w_side.shape[2]
        in_specs.append(_layer_resident((d, ns), layer))
        out_specs = [out_specs, pl.BlockSpec((tm, ns), lambda i, j: (i, 0))]
        out_shape = [out_shape, jax.ShapeDtypeStruct((m, ns), F32)]
        args.append(w_side)
        kern = _norm_matmul_side_kernel
        vmem += d * ns * 2 + 2 * tm * ns * 4
    return pl.pallas_call(
        kern,
        grid=(m // tm, n // tn),
        in_specs=in_specs,
        out_specs=out_specs,
        out_shape=out_shape,
        scratch_shapes=[pltpu.VMEM((tm, d), BF16)],
        compiler_params=_params(("arbitrary", "arbitrary"), vmem),
        name=name,
    )(*args)


def _gelu_tanh(x):
    c = math.sqrt(2.0 / math.pi)
    return 0.5 * x * (1.0 + jnp.tanh(c * (x + 0.044715 * (x * x * x))))


def _softplus(x):
    return jnp.maximum(x, 0.0) + jnp.log1p(jnp.exp(-jnp.abs(x)))


def _lru_kernel(xl_ref, gl_ref, cw_ref, cb_ref, wai_ref, ba_ref, bi_ref, lam_ref, y_ref,
                tail_ref, h_ref):
    t_rows = xl_ref.shape[0]
    nblk = t_rows // SUBLANES

    @pl.when(pl.program_id(1) == 0)
    def _():
        tail_ref[...] = jnp.zeros_like(tail_ref)
        h_ref[...] = jnp.zeros_like(h_ref)

    row8 = lax.broadcasted_iota(jnp.int32, (SUBLANES, LRU_BW), 0)
    sub3 = lax.broadcasted_iota(jnp.int32, (nblk, SUBLANES, LRU_BW), 1)

    for n in range(LRU_BLOCKS):
        cs = slice(n * LRU_BW, (n + 1) * LRU_BW)
        x = xl_ref[:, cs].astype(F32)
        prev8 = tail_ref[:, cs]
        xc = x * cw_ref[CONV_W - 1:CONV_W, cs] + cb_ref[:, cs]
        for s in range(1, CONV_W):
            xs = pltpu.roll(x, s, 0)
            ps = pltpu.roll(prev8, s, 0)
            head = jnp.where(row8 < s, ps, xs[:SUBLANES])
            xs = jnp.concatenate([head, xs[SUBLANES:]], axis=0)
            xc = xc + xs * cw_ref[CONV_W - 1 - s:CONV_W - s, cs]
        tail_ref[:, cs] = x[t_rows - SUBLANES:]

        pre = jnp.dot(xc.astype(BF16), wai_ref[n], preferred_element_type=F32)
        r = jax.nn.sigmoid(pre[:, :LRU_BW] + ba_ref[:, cs])
        i = jax.nn.sigmoid(pre[:, LRU_BW:] + bi_ref[:, cs])
        log_a = (-LRU_C * _softplus(-lam_ref[:, cs])) * r
        a = jnp.exp(log_a)
        u = jnp.sqrt(-jnp.tanh(log_a) * (1.0 + a * a)) * (i * xc)

        a3 = a.reshape(nblk, SUBLANES, LRU_BW)
        u3 = u.reshape(nblk, SUBLANES, LRU_BW)
        for dd in (1, 2, 4):
            keep = sub3 >= dd
            a_s = jnp.where(keep, pltpu.roll(a3, dd, 1), 1.0)
            u_s = jnp.where(keep, pltpu.roll(u3, dd, 1), 0.0)
            u3 = a3 * u_s + u3
            a3 = a3 * a_s
        carry = jnp.broadcast_to(h_ref[:, cs], (SUBLANES, LRU_BW))
        hs = []
        for b in range(nblk):
            hb = u3[b] + a3[b] * carry
            hs.append(hb)
            carry = jnp.broadcast_to(hb[SUBLANES - 1:SUBLANES], (SUBLANES, LRU_BW))
        h = jnp.concatenate(hs, axis=0)
        h_ref[:, cs] = carry[0:1]

        y = h * _gelu_tanh(gl_ref[:, cs].astype(F32))
        y_ref[:, cs] = y.astype(y_ref.dtype)


def _lru(proj, conv_w, conv_b, w_ai, b_a, b_i, lam, layer, batch, seq):
    t = min(LRU_T, seq)
    nt = seq // t
    w = W_LRU
    vmem = 2 * 3 * t * w * 2 + LRU_BLOCKS * LRU_BW * 2 * LRU_BW * 2 + 64 * t * LRU_BW * 4
    return pl.pallas_call(
        _lru_kernel,
        grid=(batch, nt),
        in_specs=[
            pl.BlockSpec((t, w), lambda b, s: (b * nt + s, COL_XLRU)),
            pl.BlockSpec((t, w), lambda b, s: (b * nt + s, COL_GLRU)),
            _layer_resident((CONV_W, w), layer),
            _layer_resident((1, w), layer),
            _layer_resident((LRU_BLOCKS, LRU_BW, 2 * LRU_BW), layer),
            _layer_resident((1, w), layer),
            _layer_resident((1, w), layer),
            _layer_resident((1, w), layer),
        ],
        out_specs=pl.BlockSpec((t, w), lambda b, s: (b * nt + s, 0)),
        out_shape=jax.ShapeDtypeStruct((batch * seq, w), BF16),
        scratch_shapes=[pltpu.VMEM((SUBLANES, w), F32), pltpu.VMEM((1, w), F32)],
        compiler_params=_params(("arbitrary", "arbitrary"), vmem + (8 << 20)),
        name="rglru",
    )(proj, proj, conv_w, conv_b, w_ai, b_a, b_i, lam)


def _log_sigmoid(x):
    return jnp.minimum(x, 0.0) - jnp.log1p(jnp.exp(-jnp.abs(x)))


def _gla_kernel(q_ref, k_ref, v_ref, r_ref, gk_ref, wgk2_ref, bgk_ref, ng_ref, y_ref, st_ref):
    t_rows = q_ref.shape[0]
    c = GLA_CHUNK
    hk = GLA_H * GLA_DK

    @pl.when(pl.program_id(1) == 0)
    def _():
        st_ref[...] = jnp.zeros_like(st_ref)

    z = jnp.dot(gk_ref[...], wgk2_ref[...], preferred_element_type=F32,
                precision=lax.Precision.HIGHEST) + bgk_ref[...]
    la = _log_sigmoid(z) * (1.0 / GLA_NORMALIZER)
    pos = lax.broadcasted_iota(jnp.int32, (t_rows, hk), 0) & (c - 1)
    dd = 1
    while dd < c:
        la = la + jnp.where(pos >= dd, pltpu.roll(la, dd, 0), 0.0)
        dd *= 2
    bcum = la

    tril = (lax.broadcasted_iota(jnp.int32, (c, c), 0)
            >= lax.broadcasted_iota(jnp.int32, (c, c), 1))
    nt_dims = (((1,), (1,)), ((), ()))
    tn_dims = (((0,), (0,)), ((), ()))
    n_chunks = t_rows // c

    qe_c, g_c, o_intra, upd = [], [], [], []
    for ci in range(n_chunks):
        rows = slice(ci * c, (ci + 1) * c)
        bc = bcum[rows]
        b_last = bcum[(ci + 1) * c - 1:(ci + 1) * c]
        q = q_ref[rows, :].astype(F32) * (GLA_DK ** -0.5)
        k = k_ref[rows, :].astype(F32)
        qe = (q * jnp.exp(bc)).astype(BF16)
        ke = (k * jnp.exp(-bc)).astype(BF16)
        kd = (k * jnp.exp(b_last - bc)).astype(BF16)
        qe_c.append(qe)
        g_c.append(jnp.exp(b_last))
        o_h, upd_h = [], []
        for h in range(GLA_H):
            ks = slice(h * GLA_DK, (h + 1) * GLA_DK)
            v_h = v_ref[rows, h * GLA_DV:(h + 1) * GLA_DV]
            s = lax.dot_general(qe[:, ks], ke[:, ks], nt_dims, preferred_element_type=F32)
            s = jnp.where(tril, s, 0.0).astype(BF16)
            o_h.append(jnp.dot(s, v_h, preferred_element_type=F32))
            upd_h.append(lax.dot_general(v_h, kd[:, ks], tn_dims,
                                         preferred_element_type=F32))
        o_intra.append(o_h)
        upd.append(upd_h)

    for h in range(GLA_H):
        ks = slice(h * GLA_DK, (h + 1) * GLA_DK)
        vs = slice(h * GLA_DV, (h + 1) * GLA_DV)
        st = st_ref[h]
        for ci in range(n_chunks):
            rows = slice(ci * c, (ci + 1) * c)
            o = o_intra[ci][h] + lax.dot_general(qe_c[ci][:, ks], st.astype(BF16), nt_dims,
                                                 preferred_element_type=F32)
            st = st * g_c[ci][:, ks] + upd[ci][h]
            o = o * lax.rsqrt(jnp.mean(o * o, axis=-1, keepdims=True) + EPS) * ng_ref[...]
            o = o * _silu(r_ref[rows, vs].astype(F32))
            y_ref[rows, vs] = o.astype(y_ref.dtype)
        st_ref[h] = st


def _gla(proj, gk, w_gk2_p, b_gk, norm_g, layer, batch, seq):
    t = min(GLA_T, seq)
    nt = seq // t
    hk, hv = GLA_H * GLA_DK, GLA_H * GLA_DV
    vmem = (2 * (2 * t * hk * 2 + 3 * t * hv * 2 + t * GK_PAD * 4) + t * hk * 4
            + GLA_H * GLA_DV * GLA_DK * 4 + 8 * t * hk * 4)
    return pl.pallas_call(
        _gla_kernel,
        grid=(batch, nt),
        in_specs=[
            pl.BlockSpec((t, hk), lambda b, s: (b * nt + s, COL_Q)),
            pl.BlockSpec((t, hk), lambda b, s: (b * nt + s, COL_K)),
            pl.BlockSpec((t, hv), lambda b, s: (b * nt + s, COL_V)),
            pl.BlockSpec((t, hv), lambda b, s: (b * nt + s, COL_R)),
            pl.BlockSpec((t, GK_PAD), lambda b, s: (b * nt + s, 0)),
            _layer_resident((GK_PAD, hk), layer),
            _layer_resident((1, hk), layer),
            _layer_resident((1, GLA_DV), layer),
        ],
        out_specs=pl.BlockSpec((t, hv), lambda b, s: (b * nt + s, 0)),
        out_shape=jax.ShapeDtypeStruct((batch * seq, hv), BF16),
        scratch_shapes=[pltpu.VMEM((GLA_H, GLA_DV, GLA_DK), F32)],
        compiler_params=_params(("arbitrary", "arbitrary"), vmem + (8 << 20)),
        name="gla",
    )(proj, proj, proj, proj, gk, w_gk2_p, b_gk, norm_g)


def _merge_kernel(x_ref, ya_ref, yb_ref, gl_ref, bg_ref, wb_ref, wo_ref, post_g_ref, o_ref):
    d = x_ref.shape[1]
    z_a = jnp.dot(ya_ref[...], wb_ref[:W_LRU, :], preferred_element_type=F32)
    z_b = jnp.dot(yb_ref[...], wb_ref[W_LRU:, :], preferred_element_type=F32)
    g_a = jax.nn.sigmoid(gl_ref[:, :d].astype(F32) + bg_ref[:, :d])
    g_b = jax.nn.sigmoid(gl_ref[:, d:].astype(F32) + bg_ref[:, d:])
    merged = (g_a * z_a + g_b * z_b).astype(BF16)
    h = jnp.dot(merged, wo_ref[...], preferred_element_type=F32)
    o_ref[...] = x_ref[...] + _rms(h, post_g_ref[...])


def _merge(x2, y_a, y_b, proj, b_gate, w_branch, w_out, post_g, layer):
    m, d = x2.shape
    tm = min(MERGE_TM, m)
    wb_rows = w_branch.shape[1]
    vmem = (2 * 2 * tm * d * 4 + 2 * 2 * tm * W_LRU * 2 + 2 * tm * 2 * d * 2
            + wb_rows * d * 2 + d * d * 2 + 6 * tm * d * 4)
    return pl.pallas_call(
        _merge_kernel,
        grid=(m // tm,),
        in_specs=[
            pl.BlockSpec((tm, d), lambda i: (i, 0)),
            pl.BlockSpec((tm, W_LRU), lambda i: (i, 0)),
            pl.BlockSpec((tm, GLA_H * GLA_DV), lambda i: (i, 0)),
            pl.BlockSpec((tm, N_BRANCH * d), lambda i: (i, COL_GATE)),
            _layer_resident((1, N_BRANCH * d), layer),
            _layer_resident((wb_rows, d), layer),
            _layer_resident((d, d), layer),
            _layer_resident((1, d), layer),
        ],
        out_specs=pl.BlockSpec((tm, d), lambda i: (i, 0)),
        out_shape=jax.ShapeDtypeStruct((m, d), F32),
        compiler_params=_params(("arbitrary",), vmem),
        name="merge",
    )(x2, y_a, y_b, proj, b_gate, w_branch, w_out, post_g)


def _xattn_kernel(x_ref, pre_g_ref, wq_ref, kk_ref, vv_ref, wo_ref, post_g_ref, o_ref):
    xn = _rms(x_ref[...], pre_g_ref[...]).astype(BF16)
    q = (jnp.dot(xn, wq_ref[...], preferred_element_type=F32) * (XA_DH ** -0.5)).astype(BF16)
    nt_dims = (((1,), (1,)), ((), ()))
    outs = []
    for h in range(XA_H):
        hs = slice(h * XA_DH, (h + 1) * XA_DH)
        s = lax.dot_general(q[:, hs], kk_ref[:, hs], nt_dims,
                            preferred_element_type=F32)
        p = jnp.exp(s - jnp.max(s, axis=-1, keepdims=True))
        pv = jnp.dot(p.astype(BF16), vv_ref[:, hs], preferred_element_type=F32)
        outs.append(pv / jnp.sum(p, axis=-1, keepdims=True))
    o = jnp.concatenate(outs, axis=1).astype(BF16)
    h_out = jnp.dot(o, wo_ref[...], preferred_element_type=F32)
    o_ref[...] = x_ref[...] + _rms(h_out, post_g_ref[...])


def _xattn(x2, pre_g, w_q, kv, w_o, post_g, layer, batch, seq, mem_len):
    m, d = x2.shape
    tm = min(XA_TM, seq)
    nt = seq // tm
    hd = XA_H * XA_DH
    vmem = (2 * 2 * tm * d * 4 + 2 * d * hd * 2 + 2 * 2 * mem_len * hd * 2
            + 4 * tm * d * 4 + 8 * tm * mem_len * 4)
    return pl.pallas_call(
        _xattn_kernel,
        grid=(batch, nt),
        in_specs=[
            pl.BlockSpec((tm, d), lambda b, s: (b * nt + s, 0)),
            _layer_resident((1, d), layer),
            _layer_resident((d, hd), layer),
            pl.BlockSpec((mem_len, hd), lambda b, s: (b, 0)),
            pl.BlockSpec((mem_len, hd), lambda b, s: (b, 1)),
            _layer_resident((hd, d), layer),
            _layer_resident((1, d), layer),
        ],
        out_specs=pl.BlockSpec((tm, d), lambda b, s: (b * nt + s, 0)),
        out_shape=jax.ShapeDtypeStruct((m, d), F32),
        compiler_params=_params(("arbitrary", "arbitrary"), vmem),
        name="xattn",
    )(x2, pre_g, w_q, kv, kv, w_o, post_g)


def _pack_w_in(w_in):
    n_head = 2 * W_LRU + 2 * GLA_H * GLA_DK + 2 * GLA_H * GLA_DV
    head = w_in[..., :n_head]
    gate = w_in[..., n_head + GLA_RANK:]
    w_main = jnp.concatenate([gate, head], axis=-1).astype(BF16)
    w_gk = w_in[..., n_head:n_head + GK_PAD].astype(BF16)
    return w_main, w_gk


def _rows(v):
    return v.reshape(v.shape[0], 1, -1)


def kernel(x, mem, ffn1_pre_g, ffn1_post_g, ffn1_w_up, ffn1_w_down, mix_pre_g, mix_post_g, w_in,
           conv_w, conv_b, lru_w_a, lru_b_a, lru_w_i, lru_b_i, lru_lambda, gla_w_gk2, gla_b_gk,
           gla_norm_g, b_gate, w_branch, w_out, xa_pre_g, xa_post_g, mem_g, xa_w_q, xa_w_kv,
           xa_w_o, ffn2_pre_g, ffn2_post_g, ffn2_w_up, ffn2_w_down):
    batch, seq, d = x.shape
    mem_len = mem.shape[1]
    depth = ffn1_w_up.shape[0]
    x2 = x.reshape(batch * seq, d)
    mem2 = mem.reshape(batch * mem_len, d)

    ffn1_up, ffn1_down = ffn1_w_up.astype(BF16), ffn1_w_down.astype(BF16)
    ffn2_up, ffn2_down = ffn2_w_up.astype(BF16), ffn2_w_down.astype(BF16)
    w_main, w_gk = _pack_w_in(w_in)
    w_ai = jnp.concatenate([lru_w_a, lru_w_i], axis=-1).astype(BF16)
    w_gk2_p = jnp.pad(gla_w_gk2, ((0, 0), (0, GK_PAD - GLA_RANK), (0, 0)))
    wb, wo = w_branch.astype(BF16), w_out.astype(BF16)
    xa_q, xa_kv, xa_o = xa_w_q.astype(BF16), xa_w_kv.astype(BF16), xa_w_o.astype(BF16)
    ffn1_pre, ffn1_post = _rows(ffn1_pre_g), _rows(ffn1_post_g)
    ffn2_pre, ffn2_post = _rows(ffn2_pre_g), _rows(ffn2_post_g)
    mix_pre, mix_post = _rows(mix_pre_g), _rows(mix_post_g)
    xa_pre, xa_post, mem_gain = _rows(xa_pre_g), _rows(xa_post_g), _rows(mem_g)
    conv_bias, b_a, b_i, lam = _rows(conv_b), _rows(lru_b_a), _rows(lru_b_i), _rows(lru_lambda)
    b_gk, norm_g, b_gate_r = _rows(gla_b_gk), _rows(gla_norm_g), _rows(b_gate)

    for l in range(depth):
        x2 = _ffn(x2, ffn1_pre, ffn1_up, ffn1_down, ffn1_post, l)

        proj, gk = _norm_matmul(x2, mix_pre, w_main, l, w_side=w_gk, name="mix_in_proj")
        y_a = _lru(proj, conv_w, conv_bias, w_ai, b_a, b_i, lam, l, batch, seq)
        y_b = _gla(proj, gk, w_gk2_p, b_gk, norm_g, l, batch, seq)
        x2 = _merge(x2, y_a, y_b, proj, b_gate_r, wb, wo, mix_post, l)

        kv = _norm_matmul(mem2, mem_gain, xa_kv, l, name="mem_kv_proj")
        x2 = _xattn(x2, xa_pre, xa_q, kv, xa_o, xa_post, l, batch, seq, mem_len)

        x2 = _ffn(x2, ffn2_pre, ffn2_up, ffn2_down, ffn2_post, l)

    return x2.reshape(batch, seq, d)
```

```python
import functools
import math

import jax
import jax.numpy as jnp
from jax import lax
from jax.experimental import pallas as pl
from jax.experimental.pallas import tpu as pltpu

F32 = jnp.float32
BF16 = jnp.bfloat16

D_MODEL = 2048
D_FF = 5504
FFN_RES_SCALE = 0.5
W_LRU = D_MODEL // 2
LRU_BLOCKS = 8
LRU_BW = W_LRU // LRU_BLOCKS
CONV_W = 4
LRU_C = 8.0
GLA_H = 4
GLA_DK = 128
GLA_DV = 256
GLA_RANK = 16
GLA_NORMALIZER = 16.0
GLA_CHUNK = 64
XA_H = 4
XA_DH = 128
N_BRANCH = 2
EPS = 1e-6

LANES = 128
SUBLANES = 8
V7X_VMEM_BYTES = 64 * 1024 * 1024
VMEM_LIMIT_CAP = V7X_VMEM_BYTES - 6 * 1024 * 1024

FFN_TF = 512
FFN_NF = -(-D_FF // FFN_TF)
FFN_TM = 1024
FFN_SLABS = 8
PROJ_TM = 1024
PROJ_TN = 2304
N_PROJ = N_BRANCH * D_MODEL + 2 * W_LRU + 2 * GLA_H * GLA_DV + 2 * GLA_H * GLA_DK
GK_PAD = LANES
LRU_T = 256
GLA_T = 512
MERGE_TM = 512
XA_TM = 512

COL_GATE = 0
COL_XLRU = (N_BRANCH * D_MODEL) // W_LRU
COL_GLRU = COL_XLRU + 1
COL_Q = (N_BRANCH * D_MODEL + 2 * W_LRU) // (GLA_H * GLA_DK)
COL_K = COL_Q + 1
COL_V = (N_BRANCH * D_MODEL + 2 * W_LRU + 2 * GLA_H * GLA_DK) // (GLA_H * GLA_DV)
COL_R = COL_V + 1


def _params(semantics, vmem_bytes):
    return pltpu.CompilerParams(dimension_semantics=semantics,
                                vmem_limit_bytes=int(min(VMEM_LIMIT_CAP, vmem_bytes)))


def _layer_resident(tail, layer):
    zeros = (0,) * len(tail)
    return pl.BlockSpec((None,) + tuple(tail), lambda *_: (layer,) + zeros,
                        pipeline_mode=pl.Buffered(1))


def _rms(x, g):
    ms = jnp.mean(x * x, axis=-1, keepdims=True)
    return x * lax.rsqrt(ms + EPS) * g


def _silu(x):
    return x * jax.nn.sigmoid(x)


def _vector_zero_after(v):
    bits = pltpu.bitcast(v, jnp.int32)
    acc = bits[:, :LANES]
    for k in range(1, bits.shape[1] // LANES):
        acc = acc | bits[:, k * LANES:(k + 1) * LANES]
    out = acc[:SUBLANES]
    for k in range(1, acc.shape[0] // SUBLANES):
        out = out | acc[k * SUBLANES:(k + 1) * SUBLANES]
    return lax.shift_right_logical(lax.shift_right_logical(out, 16), 16)


def _ffn_window_start(j, base=0):
    return LANES * (base // LANES + jnp.minimum(j * (FFN_TF // LANES), (D_FF - FFN_TF) // LANES))


def _ffn_kernel(xnext_ref, xprev_ref, pre_g_ref, wg_ref, wu_ref, wd_ref, post_g_ref, o_ref,
                xn_even, xn_odd, acc_even, acc_odd, *, n_tiles):
    r = pl.program_id(0)
    j = pl.program_id(1)
    nf = pl.num_programs(1)
    slab = xnext_ref.shape[0]
    n_slabs = xn_even.shape[0] // slab
    row0 = pl.multiple_of(jnp.minimum(j, n_slabs - 1) * slab, slab)

    group = 2 * SUBLANES

    def pre_norm(xn_dst):
        tokens = []
        for g0 in range(0, slab, group):
            y = _rms(xnext_ref[g0:g0 + group, :], pre_g_ref[...]).astype(BF16)
            xn_dst[pl.ds(row0 + g0, group), :] = y
            tokens.append(_vector_zero_after(y))
        return jnp.max(functools.reduce(jnp.bitwise_or, tokens))

    def matmul_step(xn_src, acc, zero_pre, zero_post):
        rows = xn_src.shape[0]
        gate = jnp.dot(xn_src[...], wg_ref[...], preferred_element_type=F32)
        half = rows // 2
        xn_top = xn_src[pl.ds(pl.multiple_of(zero_pre, half), half), :]
        xn_bot = xn_src[pl.ds(pl.multiple_of(half + zero_post, half), half), :]
        up = jnp.concatenate([jnp.dot(xn_top, wu_ref[...], preferred_element_type=F32),
                              jnp.dot(xn_bot, wu_ref[...], preferred_element_type=F32)], axis=0)
        act = _silu(gate) * up
        covered = jnp.where(j == nf - 1, nf * FFN_TF - D_FF, 0)
        col = lax.broadcasted_iota(jnp.int32, act.shape, 1)
        act = jnp.where(col >= covered, act, 0.0).astype(BF16)
        prev = jnp.where(j == 0, 0.0, acc[...])
        acc[...] = prev + jnp.dot(act, wd_ref[...], preferred_element_type=F32)

    def post_norm(acc_src):
        tokens = []
        for g0 in range(0, slab, group):
            h = acc_src[pl.ds(row0 + g0, group), :]
            y = xprev_ref[g0:g0 + group, :] + FFN_RES_SCALE * _rms(h, post_g_ref[...])
            o_ref[g0:g0 + group, :] = y
            tokens.append(_vector_zero_after(y))
        return jnp.max(functools.reduce(jnp.bitwise_or, tokens))

    @pl.when(r == 0)
    def _():
        @pl.when(j == 0)
        def _():
            acc_even[...] = jnp.zeros_like(acc_even)
            acc_odd[...] = jnp.zeros_like(acc_odd)

        pre_norm(xn_even)

    steady = (r >= 1) & (r <= n_tiles)

    @pl.when(steady & (r % 2 == 1))
    def _():
        matmul_step(xn_even, acc_even, pre_norm(xn_odd), post_norm(acc_odd))

    @pl.when(steady & (r % 2 == 0))
    def _():
        matmul_step(xn_odd, acc_odd, pre_norm(xn_even), post_norm(acc_even))

    @pl.when(r == n_tiles + 1)
    def _():
        post_norm(acc_odd if (n_tiles - 1) % 2 else acc_even)


def _ffn(x2, pre_g, w_up, w_down, post_g, layer):
    m, d = x2.shape
    tm, tf = min(FFN_TM, m), FFN_TF
    n_tiles = m // tm
    slab = tm // FFN_SLABS
    vmem = (2 * tm * d * (2 + 4)
            + 2 * 3 * d * tf * 2
            + 3 * 2 * slab * d * 4
            + 2 * tm * d * 4 + 4 * tm * tf * 4)

    def next_slab(r, j):
        return (jnp.minimum(r, n_tiles - 1) * FFN_SLABS + jnp.minimum(j, FFN_SLABS - 1), 0)

    def prev_slab(r, j):
        tile = jnp.minimum(r - 2, n_tiles - 1)
        return (jnp.where(r < 2, 0, tile * FFN_SLABS + jnp.minimum(j, FFN_SLABS - 1)), 0)

    def window(r, j):
        return jnp.where(r == 0, 0, jnp.where(r == n_tiles + 1, FFN_NF - 1, j))

    return pl.pallas_call(
        functools.partial(_ffn_kernel, n_tiles=n_tiles),
        grid=(n_tiles + 2, FFN_NF),
        in_specs=[
            pl.BlockSpec((slab, d), next_slab),
            pl.BlockSpec((slab, d), prev_slab),
            _layer_resident((1, d), layer),
            pl.BlockSpec((pl.Element(d), pl.Element(tf)),
                         lambda r, j: (0, _ffn_window_start(window(r, j)))),
            pl.BlockSpec((pl.Element(d), pl.Element(tf)),
                         lambda r, j: (0, _ffn_window_start(window(r, j), base=D_FF))),
            pl.BlockSpec((pl.Element(tf), pl.Element(d)),
                         lambda r, j: (_ffn_window_start(window(r, j)), 0)),
            _layer_resident((1, d), layer),
        ],
        out_specs=pl.BlockSpec((slab, d), prev_slab),
        out_shape=jax.ShapeDtypeStruct((m, d), F32),
        scratch_shapes=[pltpu.VMEM((tm, d), BF16), pltpu.VMEM((tm, d), BF16),
                        pltpu.VMEM((tm, d), F32), pltpu.VMEM((tm, d), F32)],
        compiler_params=_params(("arbitrary", "arbitrary"), vmem),
        name="ffn",
    )(x2, x2, pre_g, w_up, w_up, w_down, post_g)


CAST_BLOCKS = D_FF // LANES
CAST_UP_COLS = 2 * D_FF // CAST_BLOCKS
CAST_DOWN_ROWS = D_FF // CAST_BLOCKS


def _cast_payload(w_up, w_down, layer, step_of):
    d = w_up.shape[1]

    def blk(*idx):
        return jnp.minimum(step_of(*idx), CAST_BLOCKS - 1)

    in_specs = [pl.BlockSpec((None, d, CAST_UP_COLS), lambda *idx: (layer, 0, blk(*idx))),
                pl.BlockSpec((None, CAST_DOWN_ROWS, d), lambda *idx: (layer, blk(*idx), 0))]
    out_specs = [pl.BlockSpec((d, CAST_UP_COLS), lambda *idx: (0, blk(*idx))),
                 pl.BlockSpec((CAST_DOWN_ROWS, d), lambda *idx: (blk(*idx), 0))]
    out_shape = [jax.ShapeDtypeStruct(w_up.shape[1:], BF16),
                 jax.ShapeDtypeStruct(w_down.shape[1:], BF16)]
    vmem = 2 * (d * CAST_UP_COLS + CAST_DOWN_ROWS * d) * (4 + 2)
    return in_specs, out_specs, out_shape, vmem


def _cast_blocks(wu_in, wd_in, wu_out, wd_out):
    wu_out[...] = wu_in[...].astype(BF16)
    wd_out[...] = wd_in[...].astype(BF16)


def _norm_matmul_kernel(x_ref, g_ref, w_ref, o_ref, xn_ref):
    @pl.when(pl.program_id(1) == 0)
    def _():
        xn_ref[...] = _rms(x_ref[...], g_ref[...]).astype(BF16)

    o_ref[...] = jnp.dot(xn_ref[...], w_ref[...], preferred_element_type=F32).astype(o_ref.dtype)


def _norm_matmul_side_kernel(x_ref, g_ref, w_ref, ws_ref, o_ref, side_ref, xn_ref):
    @pl.when(pl.program_id(1) == 0)
    def _():
        xn = _rms(x_ref[...], g_ref[...]).astype(BF16)
        xn_ref[...] = xn
        side_ref[...] = jnp.dot(xn, ws_ref[...], preferred_element_type=F32)

    o_ref[...] = jnp.dot(xn_ref[...], w_ref[...], preferred_element_type=F32).astype(o_ref.dtype)


def _norm_matmul(x2, g, w, layer, w_side=None, name="norm_matmul"):
    m, d = x2.shape
    n = w.shape[2]
    tm, tn = min(PROJ_TM, m), min(PROJ_TN, n)
    vmem = (2 * tm * d * 4 + tm * d * 2 + 2 * d * tn * 2 + 2 * tm * tn * 2
            + 2 * tm * tn * 4 + tm * d * 4)
    in_specs = [
        pl.BlockSpec((tm, d), lambda i, j: (i, 0)),
        _layer_resident((1, d), layer),
        pl.BlockSpec((None, d, tn), lambda i, j: (layer, 0, j)),
    ]
    out_specs = pl.BlockSpec((tm, tn), lambda i, j: (i, j))
    out_shape = jax.ShapeDtypeStruct((m, n), BF16)
    args = [x2, g, w]
    kern = _norm_matmul_kernel
    if w_side is not None:
        ns = w_side.shape[2]
        in_specs.append(_layer_resident((d, ns), layer))
        out_specs = [out_specs, pl.BlockSpec((tm, ns), lambda i, j: (i, 0))]
        out_shape = [out_shape, jax.ShapeDtypeStruct((m, ns), F32)]
        args.append(w_side)
        kern = _norm_matmul_side_kernel
        vmem += d * ns * 2 + 2 * tm * ns * 4
    return pl.pallas_call(
        kern,
        grid=(m // tm, n // tn),
        in_specs=in_specs,
        out_specs=out_specs,
        out_shape=out_shape,
        scratch_shapes=[pltpu.VMEM((tm, d), BF16)],
        compiler_params=_params(("arbitrary", "arbitrary"), vmem),
        name=name,
    )(*args)


def _gelu_tanh(x):
    c = math.sqrt(2.0 / math.pi)
    return 0.5 * x * (1.0 + jnp.tanh(c * (x + 0.044715 * (x * x * x))))


def _softplus(x):
    return jnp.maximum(x, 0.0) + jnp.log1p(jnp.exp(-jnp.abs(x)))


def _lru_kernel(xl_ref, gl_ref, cw_ref, cb_ref, wai_ref, ba_ref, bi_ref, lam_ref, *rest, cast):
    if cast:
        wu_in, wd_in, y_ref, wu_out, wd_out, tail_ref, h_ref = rest
        _cast_blocks(wu_in, wd_in, wu_out, wd_out)
    else:
        y_ref, tail_ref, h_ref = rest
    t_rows = xl_ref.shape[0]
    nblk = t_rows // SUBLANES

    @pl.when(pl.program_id(1) == 0)
    def _():
        tail_ref[...] = jnp.zeros_like(tail_ref)
        h_ref[...] = jnp.zeros_like(h_ref)

    row8 = lax.broadcasted_iota(jnp.int32, (SUBLANES, LRU_BW), 0)
    sub3 = lax.broadcasted_iota(jnp.int32, (nblk, SUBLANES, LRU_BW), 1)

    for n in range(LRU_BLOCKS):
        cs = slice(n * LRU_BW, (n + 1) * LRU_BW)
        x = xl_ref[:, cs].astype(F32)
        prev8 = tail_ref[:, cs]
        xc = x * cw_ref[CONV_W - 1:CONV_W, cs] + cb_ref[:, cs]
        for s in range(1, CONV_W):
            xs = pltpu.roll(x, s, 0)
            ps = pltpu.roll(prev8, s, 0)
            head = jnp.where(row8 < s, ps, xs[:SUBLANES])
            xs = jnp.concatenate([head, xs[SUBLANES:]], axis=0)
            xc = xc + xs * cw_ref[CONV_W - 1 - s:CONV_W - s, cs]
        tail_ref[:, cs] = x[t_rows - SUBLANES:]

        pre = jnp.dot(xc.astype(BF16), wai_ref[n], preferred_element_type=F32)
        r = jax.nn.sigmoid(pre[:, :LRU_BW] + ba_ref[:, cs])
        i = jax.nn.sigmoid(pre[:, LRU_BW:] + bi_ref[:, cs])
        log_a = (-LRU_C * _softplus(-lam_ref[:, cs])) * r
        a = jnp.exp(log_a)
        u = jnp.sqrt(-jnp.tanh(log_a) * (1.0 + a * a)) * (i * xc)

        a3 = a.reshape(nblk, SUBLANES, LRU_BW)
        u3 = u.reshape(nblk, SUBLANES, LRU_BW)
        for dd in (1, 2, 4):
            keep = sub3 >= dd
            a_s = jnp.where(keep, pltpu.roll(a3, dd, 1), 1.0)
            u_s = jnp.where(keep, pltpu.roll(u3, dd, 1), 0.0)
            u3 = a3 * u_s + u3
            a3 = a3 * a_s
        carry = jnp.broadcast_to(h_ref[:, cs], (SUBLANES, LRU_BW))
        hs = []
        for b in range(nblk):
            hb = u3[b] + a3[b] * carry
            hs.append(hb)
            carry = jnp.broadcast_to(hb[SUBLANES - 1:SUBLANES], (SUBLANES, LRU_BW))
        h = jnp.concatenate(hs, axis=0)
        h_ref[:, cs] = carry[0:1]

        y = h * _gelu_tanh(gl_ref[:, cs].astype(F32))
        y_ref[:, cs] = y.astype(y_ref.dtype)


def _lru(proj, conv_w, conv_b, w_ai, b_a, b_i, lam, layer, batch, seq, cast=None):
    t = min(LRU_T, seq)
    nt = seq // t
    w = W_LRU
    vmem = 2 * 3 * t * w * 2 + LRU_BLOCKS * LRU_BW * 2 * LRU_BW * 2 + 64 * t * LRU_BW * 4
    out_specs = pl.BlockSpec((t, w), lambda b, s: (b * nt + s, 0))
    out_shape = jax.ShapeDtypeStruct((batch * seq, w), BF16)
    args = [proj, proj, conv_w, conv_b, w_ai, b_a, b_i, lam]
    cast_in = []
    if cast is not None:
        assert batch * nt >= CAST_BLOCKS
        cast_in, cast_out, cast_shape, cast_vmem = _cast_payload(
            cast[0], cast[1], cast[2], lambda b, s: b * nt + s)
        out_specs, out_shape = [out_specs] + cast_out, [out_shape] + cast_shape
        args += [cast[0], cast[1]]
        vmem += cast_vmem
    return pl.pallas_call(
        functools.partial(_lru_kernel, cast=cast is not None),
        grid=(batch, nt),
        in_specs=[
            pl.BlockSpec((t, w), lambda b, s: (b * nt + s, COL_XLRU)),
            pl.BlockSpec((t, w), lambda b, s: (b * nt + s, COL_GLRU)),
            _layer_resident((CONV_W, w), layer),
            _layer_resident((1, w), layer),
            _layer_resident((LRU_BLOCKS, LRU_BW, 2 * LRU_BW), layer),
            _layer_resident((1, w), layer),
            _layer_resident((1, w), layer),
            _layer_resident((1, w), layer),
        ] + cast_in,
        out_specs=out_specs,
        out_shape=out_shape,
        scratch_shapes=[pltpu.VMEM((SUBLANES, w), F32), pltpu.VMEM((1, w), F32)],
        compiler_params=_params(("arbitrary", "arbitrary"), vmem + (8 << 20)),
        name="rglru",
    )(*args)


def _log_sigmoid(x):
    return jnp.minimum(x, 0.0) - jnp.log1p(jnp.exp(-jnp.abs(x)))


def _gla_kernel(q_ref, k_ref, v_ref, r_ref, gk_ref, wgk2_ref, bgk_ref, ng_ref, *rest, cast):
    if cast:
        wu_in, wd_in, y_ref, wu_out, wd_out, st_ref = rest
        _cast_blocks(wu_in, wd_in, wu_out, wd_out)
    else:
        y_ref, st_ref = rest
    t_rows = q_ref.shape[0]
    c = GLA_CHUNK
    hk = GLA_H * GLA_DK

    @pl.when(pl.program_id(1) == 0)
    def _():
        st_ref[...] = jnp.zeros_like(st_ref)

    z = jnp.dot(gk_ref[...], wgk2_ref[...], preferred_element_type=F32,
                precision=lax.Precision.HIGHEST) + bgk_ref[...]
    la = _log_sigmoid(z) * (1.0 / GLA_NORMALIZER)
    pos = lax.broadcasted_iota(jnp.int32, (t_rows, hk), 0) & (c - 1)
    dd = 1
    while dd < c:
        la = la + jnp.where(pos >= dd, pltpu.roll(la, dd, 0), 0.0)
        dd *= 2
    bcum = la

    tril = (lax.broadcasted_iota(jnp.int32, (c, c), 0)
            >= lax.broadcasted_iota(jnp.int32, (c, c), 1))
    nt_dims = (((1,), (1,)), ((), ()))
    tn_dims = (((0,), (0,)), ((), ()))
    n_chunks = t_rows // c

    qe_c, g_c, o_intra, upd = [], [], [], []
    for ci in range(n_chunks):
        rows = slice(ci * c, (ci + 1) * c)
        bc = bcum[rows]
        b_last = bcum[(ci + 1) * c - 1:(ci + 1) * c]
        q = q_ref[rows, :].astype(F32) * (GLA_DK ** -0.5)
        k = k_ref[rows, :].astype(F32)
        qe = (q * jnp.exp(bc)).astype(BF16)
        ke = (k * jnp.exp(-bc)).astype(BF16)
        kd = (k * jnp.exp(b_last - bc)).astype(BF16)
        qe_c.append(qe)
        g_c.append(jnp.exp(b_last))
        o_h, upd_h = [], []
        for h in range(GLA_H):
            ks = slice(h * GLA_DK, (h + 1) * GLA_DK)
            v_h = v_ref[rows, h * GLA_DV:(h + 1) * GLA_DV]
            s = lax.dot_general(qe[:, ks], ke[:, ks], nt_dims, preferred_element_type=F32)
            s = jnp.where(tril, s, 0.0).astype(BF16)
            o_h.append(jnp.dot(s, v_h, preferred_element_type=F32))
            upd_h.append(lax.dot_general(v_h, kd[:, ks], tn_dims,
                                         preferred_element_type=F32))
        o_intra.append(o_h)
        upd.append(upd_h)

    for h in range(GLA_H):
        ks = slice(h * GLA_DK, (h + 1) * GLA_DK)
        vs = slice(h * GLA_DV, (h + 1) * GLA_DV)
        st = st_ref[h]
        for ci in range(n_chunks):
            rows = slice(ci * c, (ci + 1) * c)
            o = o_intra[ci][h] + lax.dot_general(qe_c[ci][:, ks], st.astype(BF16), nt_dims,
                                                 preferred_element_type=F32)
            st = st * g_c[ci][:, ks] + upd[ci][h]
            o = o * lax.rsqrt(jnp.mean(o * o, axis=-1, keepdims=True) + EPS) * ng_ref[...]
            o = o * _silu(r_ref[rows, vs].astype(F32))
            y_ref[rows, vs] = o.astype(y_ref.dtype)
        st_ref[h] = st


def _gla(proj, gk, w_gk2_p, b_gk, norm_g, layer, batch, seq, cast=None):
    t = min(GLA_T, seq)
    while cast is not None and batch * (seq // t) < CAST_BLOCKS:
        t //= 2
    nt = seq // t
    hk, hv = GLA_H * GLA_DK, GLA_H * GLA_DV
    vmem = (2 * (2 * t * hk * 2 + 3 * t * hv * 2 + t * GK_PAD * 4) + t * hk * 4
            + GLA_H * GLA_DV * GLA_DK * 4 + 8 * t * hk * 4)
    out_specs = pl.BlockSpec((t, hv), lambda b, s: (b * nt + s, 0))
    out_shape = jax.ShapeDtypeStruct((batch * seq, hv), BF16)
    args = [proj, proj, proj, proj, gk, w_gk2_p, b_gk, norm_g]
    cast_in = []
    if cast is not None:
        cast_in, cast_out, cast_shape, cast_vmem = _cast_payload(
            cast[0], cast[1], cast[2], lambda b, s: b * nt + s)
        out_specs, out_shape = [out_specs] + cast_out, [out_shape] + cast_shape
        args += [cast[0], cast[1]]
        vmem += cast_vmem
    return pl.pallas_call(
        functools.partial(_gla_kernel, cast=cast is not None),
        grid=(batch, nt),
        in_specs=[
            pl.BlockSpec((t, hk), lambda b, s: (b * nt + s, COL_Q)),
            pl.BlockSpec((t, hk), lambda b, s: (b * nt + s, COL_K)),
            pl.BlockSpec((t, hv), lambda b, s: (b * nt + s, COL_V)),
            pl.BlockSpec((t, hv), lambda b, s: (b * nt + s, COL_R)),
            pl.BlockSpec((t, GK_PAD), lambda b, s: (b * nt + s, 0)),
            _layer_resident((GK_PAD, hk), layer),
            _layer_resident((1, hk), layer),
            _layer_resident((1, GLA_DV), layer),
        ] + cast_in,
        out_specs=out_specs,
        out_shape=out_shape,
        scratch_shapes=[pltpu.VMEM((GLA_H, GLA_DV, GLA_DK), F32)],
        compiler_params=_params(("arbitrary", "arbitrary"), vmem + (8 << 20)),
        name="gla",
    )(*args)


def _merge_kernel(x_ref, ya_ref, yb_ref, gl_ref, bg_ref, wb_ref, wo_ref, post_g_ref, o_ref):
    d = x_ref.shape[1]
    z_a = jnp.dot(ya_ref[...], wb_ref[:W_LRU, :], preferred_element_type=F32)
    z_b = jnp.dot(yb_ref[...], wb_ref[W_LRU:, :], preferred_element_type=F32)
    g_a = jax.nn.sigmoid(gl_ref[:, :d].astype(F32) + bg_ref[:, :d])
    g_b = jax.nn.sigmoid(gl_ref[:, d:].astype(F32) + bg_ref[:, d:])
    merged = (g_a * z_a + g_b * z_b).astype(BF16)
    h = jnp.dot(merged, wo_ref[...], preferred_element_type=F32)
    o_ref[...] = x_ref[...] + _rms(h, post_g_ref[...])


def _merge(x2, y_a, y_b, proj, b_gate, w_branch, w_out, post_g, layer):
    m, d = x2.shape
    tm = min(MERGE_TM, m)
    wb_rows = w_branch.shape[1]
    vmem = (2 * 2 * tm * d * 4 + 2 * 2 * tm * W_LRU * 2 + 2 * tm * 2 * d * 2
            + wb_rows * d * 2 + d * d * 2 + 6 * tm * d * 4)
    return pl.pallas_call(
        _merge_kernel,
        grid=(m // tm,),
        in_specs=[
            pl.BlockSpec((tm, d), lambda i: (i, 0)),
            pl.BlockSpec((tm, W_LRU), lambda i: (i, 0)),
            pl.BlockSpec((tm, GLA_H * GLA_DV), lambda i: (i, 0)),
            pl.BlockSpec((tm, N_BRANCH * d), lambda i: (i, COL_GATE)),
            _layer_resident((1, N_BRANCH * d), layer),
            _layer_resident((wb_rows, d), layer),
            _layer_resident((d, d), layer),
            _layer_resident((1, d), layer),
        ],
        out_specs=pl.BlockSpec((tm, d), lambda i: (i, 0)),
        out_shape=jax.ShapeDtypeStruct((m, d), F32),
        compiler_params=_params(("arbitrary",), vmem),
        name="merge",
    )(x2, y_a, y_b, proj, b_gate, w_branch, w_out, post_g)


def _xattn_kernel(x_ref, pre_g_ref, wq_ref, kk_ref, vv_ref, wo_ref, post_g_ref, o_ref):
    xn = _rms(x_ref[...], pre_g_ref[...]).astype(BF16)
    q = (jnp.dot(xn, wq_ref[...], preferred_element_type=F32) * (XA_DH ** -0.5)).astype(BF16)
    nt_dims = (((1,), (1,)), ((), ()))
    outs = []
    for h in range(XA_H):
        hs = slice(h * XA_DH, (h + 1) * XA_DH)
        s = lax.dot_general(q[:, hs], kk_ref[:, hs], nt_dims,
                            preferred_element_type=F32)
        p = jnp.exp(s - jnp.max(s, axis=-1, keepdims=True))
        pv = jnp.dot(p.astype(BF16), vv_ref[:, hs], preferred_element_type=F32)
        outs.append(pv / jnp.sum(p, axis=-1, keepdims=True))
    o = jnp.concatenate(outs, axis=1).astype(BF16)
    h_out = jnp.dot(o, wo_ref[...], preferred_element_type=F32)
    o_ref[...] = x_ref[...] + _rms(h_out, post_g_ref[...])


def _xattn(x2, pre_g, w_q, kv, w_o, post_g, layer, batch, seq, mem_len):
    m, d = x2.shape
    tm = min(XA_TM, seq)
    nt = seq // tm
    hd = XA_H * XA_DH
    vmem = (2 * 2 * tm * d * 4 + 2 * d * hd * 2 + 2 * 2 * mem_len * hd * 2
            + 4 * tm * d * 4 + 8 * tm * mem_len * 4)
    return pl.pallas_call(
        _xattn_kernel,
        grid=(batch, nt),
        in_specs=[
            pl.BlockSpec((tm, d), lambda b, s: (b * nt + s, 0)),
            _layer_resident((1, d), layer),
            _layer_resident((d, hd), layer),
            pl.BlockSpec((mem_len, hd), lambda b, s: (b, 0)),
            pl.BlockSpec((mem_len, hd), lambda b, s: (b, 1)),
            _layer_resident((hd, d), layer),
            _layer_resident((1, d), layer),
        ],
        out_specs=pl.BlockSpec((tm, d), lambda b, s: (b * nt + s, 0)),
        out_shape=jax.ShapeDtypeStruct((m, d), F32),
        compiler_params=_params(("arbitrary", "arbitrary"), vmem),
        name="xattn",
    )(x2, pre_g, w_q, kv, kv, w_o, post_g)


def _pack_w_in(w_in):
    n_head = 2 * W_LRU + 2 * GLA_H * GLA_DK + 2 * GLA_H * GLA_DV
    head = w_in[..., :n_head]
    gate = w_in[..., n_head + GLA_RANK:]
    w_main = jnp.concatenate([gate, head], axis=-1).astype(BF16)
    w_gk = w_in[..., n_head:n_head + GK_PAD].astype(BF16)
    return w_main, w_gk


def _rows(v):
    return v.reshape(v.shape[0], 1, -1)


def kernel(x, mem, ffn1_pre_g, ffn1_post_g, ffn1_w_up, ffn1_w_down, mix_pre_g, mix_post_g, w_in,
           conv_w, conv_b, lru_w_a, lru_b_a, lru_w_i, lru_b_i, lru_lambda, gla_w_gk2, gla_b_gk,
           gla_norm_g, b_gate, w_branch, w_out, xa_pre_g, xa_post_g, mem_g, xa_w_q, xa_w_kv,
           xa_w_o, ffn2_pre_g, ffn2_post_g, ffn2_w_up, ffn2_w_down):
    batch, seq, d = x.shape
    mem_len = mem.shape[1]
    depth = ffn1_w_up.shape[0]
    x2 = x.reshape(batch * seq, d)
    mem2 = mem.reshape(batch * mem_len, d)

    ffn1_up, ffn1_down = ffn1_w_up[0].astype(BF16), ffn1_w_down[0].astype(BF16)
    w_main, w_gk = _pack_w_in(w_in)
    w_ai = jnp.concatenate([lru_w_a, lru_w_i], axis=-1).astype(BF16)
    w_gk2_p = jnp.pad(gla_w_gk2, ((0, 0), (0, GK_PAD - GLA_RANK), (0, 0)))
    wb, wo = w_branch.astype(BF16), w_out.astype(BF16)
    xa_q, xa_kv, xa_o = xa_w_q.astype(BF16), xa_w_kv.astype(BF16), xa_w_o.astype(BF16)
    ffn1_pre, ffn1_post = _rows(ffn1_pre_g), _rows(ffn1_post_g)
    ffn2_pre, ffn2_post = _rows(ffn2_pre_g), _rows(ffn2_post_g)
    mix_pre, mix_post = _rows(mix_pre_g), _rows(mix_post_g)
    xa_pre, xa_post, mem_gain = _rows(xa_pre_g), _rows(xa_post_g), _rows(mem_g)
    conv_bias, b_a, b_i, lam = _rows(conv_b), _rows(lru_b_a), _rows(lru_b_i), _rows(lru_lambda)
    b_gk, norm_g, b_gate_r = _rows(gla_b_gk), _rows(gla_norm_g), _rows(b_gate)

    for l in range(depth):
        x2 = _ffn(x2, ffn1_pre, ffn1_up, ffn1_down, ffn1_post, l)

        proj, gk = _norm_matmul(x2, mix_pre, w_main, l, w_side=w_gk, name="mix_in_proj")
        y_a, ffn2_up, ffn2_down = _lru(proj, conv_w, conv_bias, w_ai, b_a, b_i, lam, l, batch, seq,
                                       cast=(ffn2_w_up, ffn2_w_down, l))
        if l + 1 < depth:
            y_b, ffn1_up, ffn1_down = _gla(proj, gk, w_gk2_p, b_gk, norm_g, l, batch, seq,
                                           cast=(ffn1_w_up, ffn1_w_down, l + 1))
        else:
            y_b = _gla(proj, gk, w_gk2_p, b_gk, norm_g, l, batch, seq)
        x2 = _merge(x2, y_a, y_b, proj, b_gate_r, wb, wo, mix_post, l)

        kv = _norm_matmul(mem2, mem_gain, xa_kv, l, name="mem_kv_proj")
        x2 = _xattn(x2, xa_pre, xa_q, kv, xa_o, xa_post, l, batch, seq, mem_len)

        x2 = _ffn(x2, ffn2_pre, ffn2_up, ffn2_down, ffn2_post, l)

    return x2.reshape(batch, seq, d)
```

```python
import functools
import math

import jax
import jax.numpy as jnp
from jax import lax
from jax.experimental import pallas as pl
from jax.experimental.pallas import tpu as pltpu

F32 = jnp.float32
BF16 = jnp.bfloat16

D_MODEL = 2048
D_FF = 5504
FFN_RES_SCALE = 0.5
W_LRU = D_MODEL // 2
LRU_BLOCKS = 8
LRU_BW = W_LRU // LRU_BLOCKS
CONV_W = 4
LRU_C = 8.0
GLA_H = 4
GLA_DK = 128
GLA_DV = 256
GLA_RANK = 16
GLA_NORMALIZER = 16.0
GLA_CHUNK = 64
XA_H = 4
XA_DH = 128
N_BRANCH = 2
EPS = 1e-6

LANES = 128
SUBLANES = 8
V7X_VMEM_BYTES = 64 * 1024 * 1024
VMEM_LIMIT_CAP = V7X_VMEM_BYTES - 6 * 1024 * 1024

FFN_TF = 512
FFN_NF = -(-D_FF // FFN_TF)
FFN_TM = 1024
FFN_SLABS = 8
PROJ_TM = 1024
PROJ_TN = 2304
N_PROJ = N_BRANCH * D_MODEL + 2 * W_LRU + 2 * GLA_H * GLA_DV + 2 * GLA_H * GLA_DK
GK_PAD = LANES
LRU_T = 256
GLA_T = 512
MERGE_TM = 512
XA_TM = 512

COL_GATE = 0
COL_XLRU = (N_BRANCH * D_MODEL) // W_LRU
COL_GLRU = COL_XLRU + 1
COL_Q = (N_BRANCH * D_MODEL + 2 * W_LRU) // (GLA_H * GLA_DK)
COL_K = COL_Q + 1
COL_V = (N_BRANCH * D_MODEL + 2 * W_LRU + 2 * GLA_H * GLA_DK) // (GLA_H * GLA_DV)
COL_R = COL_V + 1


def _params(semantics, vmem_bytes):
    return pltpu.CompilerParams(dimension_semantics=semantics,
                                vmem_limit_bytes=int(min(VMEM_LIMIT_CAP, vmem_bytes)))


def _layer_resident(tail, layer):
    zeros = (0,) * len(tail)
    return pl.BlockSpec((None,) + tuple(tail), lambda *_: (layer,) + zeros,
                        pipeline_mode=pl.Buffered(1))


def _rms(x, g):
    ms = jnp.mean(x * x, axis=-1, keepdims=True)
    return x * lax.rsqrt(ms + EPS) * g


def _silu(x):
    return x * jax.nn.sigmoid(x)


def _ffn_window_start(j, base=0):
    return LANES * (base // LANES + jnp.minimum(j * (FFN_TF // LANES), (D_FF - FFN_TF) // LANES))


def _ffn_kernel(xnext_ref, xprev_ref, pre_g_ref, wg_ref, wu_ref, wd_ref, post_g_ref, o_ref,
                xn_even, xn_odd, acc_even, acc_odd, *, n_tiles):
    r = pl.program_id(0)
    j = pl.program_id(1)
    nf = pl.num_programs(1)
    slab = xnext_ref.shape[0]
    n_slabs = xn_even.shape[0] // slab
    row0 = pl.multiple_of(jnp.minimum(j, n_slabs - 1) * slab, slab)
    has_slab = j < n_slabs
    even = r % 2 == 0
    group = 2 * SUBLANES

    def pre_norm(xn_dst):
        for g0 in range(0, slab, group):
            y = _rms(xnext_ref[g0:g0 + group, :], pre_g_ref[...])
            xn_dst[pl.ds(row0 + g0, group), :] = y.astype(BF16)

    def matmul_step(xn_src, acc):
        xn = xn_src[...]
        gate = jnp.dot(xn, wg_ref[...], preferred_element_type=F32)
        up = jnp.dot(xn, wu_ref[...], preferred_element_type=F32)
        act = _silu(gate) * up
        covered = jnp.where(j == nf - 1, nf * FFN_TF - D_FF, 0)
        col = lax.broadcasted_iota(jnp.int32, act.shape, 1)
        act = jnp.where(col >= covered, act, 0.0).astype(BF16)
        prev = jnp.where(j == 0, 0.0, acc[...])
        acc[...] = prev + jnp.dot(act, wd_ref[...], preferred_element_type=F32)

    def post_norm(acc_src):
        for g0 in range(0, slab, group):
            h = acc_src[pl.ds(row0 + g0, group), :]
            o_ref[g0:g0 + group, :] = (xprev_ref[g0:g0 + group, :]
                                       + FFN_RES_SCALE * _rms(h, post_g_ref[...]))

    @pl.when((r == 0) & (j == 0))
    def _():
        acc_even[...] = jnp.zeros_like(acc_even)
        acc_odd[...] = jnp.zeros_like(acc_odd)

    for parity, xn_buf, acc_buf in ((even, xn_even, acc_even), (~even, xn_odd, acc_odd)):
        pl.when(has_slab & (r < n_tiles) & parity)(functools.partial(pre_norm, xn_buf))
        pl.when((r >= 1) & (r <= n_tiles) & ~parity)(
            functools.partial(matmul_step, xn_buf, acc_buf))
        pl.when(has_slab & (r >= 2) & parity)(functools.partial(post_norm, acc_buf))


def _ffn(x2, pre_g, w_up, w_down, post_g, layer):
    m, d = x2.shape
    tm, tf = min(FFN_TM, m), FFN_TF
    n_tiles = m // tm
    slab = tm // FFN_SLABS
    vmem = (2 * tm * d * (2 + 4)
            + 2 * 3 * d * tf * 2
            + 3 * 2 * slab * d * 4
            + 2 * tm * d * 4 + 4 * tm * tf * 4)

    def next_slab(r, j):
        return (jnp.minimum(r, n_tiles - 1) * FFN_SLABS + jnp.minimum(j, FFN_SLABS - 1), 0)

    def prev_slab(r, j):
        tile = jnp.minimum(r - 2, n_tiles - 1)
        return (jnp.where(r < 2, 0, tile * FFN_SLABS + jnp.minimum(j, FFN_SLABS - 1)), 0)

    def window(r, j):
        return jnp.where(r == 0, 0, jnp.where(r == n_tiles + 1, FFN_NF - 1, j))

    return pl.pallas_call(
        functools.partial(_ffn_kernel, n_tiles=n_tiles),
        grid=(n_tiles + 2, FFN_NF),
        in_specs=[
            pl.BlockSpec((slab, d), next_slab),
            pl.BlockSpec((slab, d), prev_slab),
            _layer_resident((1, d), layer),
            pl.BlockSpec((pl.Element(d), pl.Element(tf)),
                         lambda r, j: (0, _ffn_window_start(window(r, j)))),
            pl.BlockSpec((pl.Element(d), pl.Element(tf)),
                         lambda r, j: (0, _ffn_window_start(window(r, j), base=D_FF))),
            pl.BlockSpec((pl.Element(tf), pl.Element(d)),
                         lambda r, j: (_ffn_window_start(window(r, j)), 0)),
            _layer_resident((1, d), layer),
        ],
        out_specs=pl.BlockSpec((slab, d), prev_slab),
        out_shape=jax.ShapeDtypeStruct((m, d), F32),
        scratch_shapes=[pltpu.VMEM((tm, d), BF16), pltpu.VMEM((tm, d), BF16),
                        pltpu.VMEM((tm, d), F32), pltpu.VMEM((tm, d), F32)],
        compiler_params=_params(("arbitrary", "arbitrary"), vmem),
        name="ffn",
    )(x2, x2, pre_g, w_up, w_up, w_down, post_g)


CAST_BLOCKS = D_FF // LANES
CAST_UP_COLS = 2 * D_FF // CAST_BLOCKS
CAST_DOWN_ROWS = D_FF // CAST_BLOCKS


def _cast_payload(w_up, w_down, layer, step_of):
    d = w_up.shape[1]

    def blk(*idx):
        return jnp.minimum(step_of(*idx), CAST_BLOCKS - 1)

    in_specs = [pl.BlockSpec((None, d, CAST_UP_COLS), lambda *idx: (layer, 0, blk(*idx))),
                pl.BlockSpec((None, CAST_DOWN_ROWS, d), lambda *idx: (layer, blk(*idx), 0))]
    out_specs = [pl.BlockSpec((d, CAST_UP_COLS), lambda *idx: (0, blk(*idx))),
                 pl.BlockSpec((CAST_DOWN_ROWS, d), lambda *idx: (blk(*idx), 0))]
    out_shape = [jax.ShapeDtypeStruct(w_up.shape[1:], BF16),
                 jax.ShapeDtypeStruct(w_down.shape[1:], BF16)]
    vmem = 2 * (d * CAST_UP_COLS + CAST_DOWN_ROWS * d) * (4 + 2)
    return in_specs, out_specs, out_shape, vmem


def _cast_blocks(wu_in, wd_in, wu_out, wd_out):
    wu_out[...] = wu_in[...].astype(BF16)
    wd_out[...] = wd_in[...].astype(BF16)


def _norm_matmul_kernel(x_ref, g_ref, w_ref, o_ref, xn_ref):
    @pl.when(pl.program_id(1) == 0)
    def _():
        xn_ref[...] = _rms(x_ref[...], g_ref[...]).astype(BF16)

    o_ref[...] = jnp.dot(xn_ref[...], w_ref[...], preferred_element_type=F32).astype(o_ref.dtype)


def _norm_matmul_side_kernel(x_ref, g_ref, w_ref, ws_ref, o_ref, side_ref, xn_ref):
    @pl.when(pl.program_id(1) == 0)
    def _():
        xn = _rms(x_ref[...], g_ref[...]).astype(BF16)
        xn_ref[...] = xn
        side_ref[...] = jnp.dot(xn, ws_ref[...], preferred_element_type=F32)

    o_ref[...] = jnp.dot(xn_ref[...], w_ref[...], preferred_element_type=F32).astype(o_ref.dtype)


def _norm_matmul(x2, g, w, layer, w_side=None, name="norm_matmul"):
    m, d = x2.shape
    n = w.shape[2]
    tm, tn = min(PROJ_TM, m), min(PROJ_TN, n)
    vmem = (2 * tm * d * 4 + tm * d * 2 + 2 * d * tn * 2 + 2 * tm * tn * 2
            + 2 * tm * tn * 4 + tm * d * 4)
    in_specs = [
        pl.BlockSpec((tm, d), lambda i, j: (i, 0)),
        _layer_resident((1, d), layer),
        pl.BlockSpec((None, d, tn), lambda i, j: (layer, 0, j)),
    ]
    out_specs = pl.BlockSpec((tm, tn), lambda i, j: (i, j))
    out_shape = jax.ShapeDtypeStruct((m, n), BF16)
    args = [x2, g, w]
    kern = _norm_matmul_kernel
    if w_side is not None:
        ns = w_side.shape[2]
        in_specs.append(_layer_resident((d, ns), layer))
        out_specs = [out_specs, pl.BlockSpec((tm, ns), lambda i, j: (i, 0))]
        out_shape = [out_shape, jax.ShapeDtypeStruct((m, ns), F32)]
        args.append(w_side)
        kern = _norm_matmul_side_kernel
        vmem += d * ns * 2 + 2 * tm * ns * 4
    return pl.pallas_call(
        kern,
        grid=(m // tm, n // tn),
        in_specs=in_specs,
        out_specs=out_specs,
        out_shape=out_shape,
        scratch_shapes=[pltpu.VMEM((tm, d), BF16)],
        compiler_params=_params(("arbitrary", "arbitrary"), vmem),
        name=name,
    )(*args)


def _gelu_tanh(x):
    c = math.sqrt(2.0 / math.pi)
    return 0.5 * x * (1.0 + jnp.tanh(c * (x + 0.044715 * (x * x * x))))


def _softplus(x):
    return jnp.maximum(x, 0.0) + jnp.log1p(jnp.exp(-jnp.abs(x)))


def _lru_kernel(xl_ref, gl_ref, cw_ref, cb_ref, wai_ref, ba_ref, bi_ref, lam_ref, *rest, cast):
    if cast:
        wu_in, wd_in, y_ref, wu_out, wd_out, tail_ref, h_ref = rest
        _cast_blocks(wu_in, wd_in, wu_out, wd_out)
    else:
        y_ref, tail_ref, h_ref = rest
    t_rows = xl_ref.shape[0]
    nblk = t_rows // SUBLANES

    @pl.when(pl.program_id(1) == 0)
    def _():
        tail_ref[...] = jnp.zeros_like(tail_ref)
        h_ref[...] = jnp.zeros_like(h_ref)

    row8 = lax.broadcasted_iota(jnp.int32, (SUBLANES, LRU_BW), 0)
    sub3 = lax.broadcasted_iota(jnp.int32, (nblk, SUBLANES, LRU_BW), 1)

    for n in range(LRU_BLOCKS):
        cs = slice(n * LRU_BW, (n + 1) * LRU_BW)
        x = xl_ref[:, cs].astype(F32)
        prev8 = tail_ref[:, cs]
        xc = x * cw_ref[CONV_W - 1:CONV_W, cs] + cb_ref[:, cs]
        for s in range(1, CONV_W):
            xs = pltpu.roll(x, s, 0)
            ps = pltpu.roll(prev8, s, 0)
            head = jnp.where(row8 < s, ps, xs[:SUBLANES])
            xs = jnp.concatenate([head, xs[SUBLANES:]], axis=0)
            xc = xc + xs * cw_ref[CONV_W - 1 - s:CONV_W - s, cs]
        tail_ref[:, cs] = x[t_rows - SUBLANES:]

        pre = jnp.dot(xc.astype(BF16), wai_ref[n], preferred_element_type=F32)
        r = jax.nn.sigmoid(pre[:, :LRU_BW] + ba_ref[:, cs])
        i = jax.nn.sigmoid(pre[:, LRU_BW:] + bi_ref[:, cs])
        log_a = (-LRU_C * _softplus(-lam_ref[:, cs])) * r
        a = jnp.exp(log_a)
        u = jnp.sqrt(-jnp.tanh(log_a) * (1.0 + a * a)) * (i * xc)

        a3 = a.reshape(nblk, SUBLANES, LRU_BW)
        u3 = u.reshape(nblk, SUBLANES, LRU_BW)
        for dd in (1, 2, 4):
            keep = sub3 >= dd
            a_s = jnp.where(keep, pltpu.roll(a3, dd, 1), 1.0)
            u_s = jnp.where(keep, pltpu.roll(u3, dd, 1), 0.0)
            u3 = a3 * u_s + u3
            a3 = a3 * a_s
        carry = jnp.broadcast_to(h_ref[:, cs], (SUBLANES, LRU_BW))
        hs = []
        for b in range(nblk):
            hb = u3[b] + a3[b] * carry
            hs.append(hb)
            carry = jnp.broadcast_to(hb[SUBLANES - 1:SUBLANES], (SUBLANES, LRU_BW))
        h = jnp.concatenate(hs, axis=0)
        h_ref[:, cs] = carry[0:1]

        y = h * _gelu_tanh(gl_ref[:, cs].astype(F32))
        y_ref[:, cs] = y.astype(y_ref.dtype)


def _lru(proj, conv_w, conv_b, w_ai, b_a, b_i, lam, layer, batch, seq, cast=None):
    t = min(LRU_T, seq)
    nt = seq // t
    w = W_LRU
    vmem = 2 * 3 * t * w * 2 + LRU_BLOCKS * LRU_BW * 2 * LRU_BW * 2 + 64 * t * LRU_BW * 4
    out_specs = pl.BlockSpec((t, w), lambda b, s: (b * nt + s, 0))
    out_shape = jax.ShapeDtypeStruct((batch * seq, w), BF16)
    args = [proj, proj, conv_w, conv_b, w_ai, b_a, b_i, lam]
    cast_in = []
    if cast is not None:
        assert batch * nt >= CAST_BLOCKS
        cast_in, cast_out, cast_shape, cast_vmem = _cast_payload(
            cast[0], cast[1], cast[2], lambda b, s: b * nt + s)
        out_specs, out_shape = [out_specs] + cast_out, [out_shape] + cast_shape
        args += [cast[0], cast[1]]
        vmem += cast_vmem
    return pl.pallas_call(
        functools.partial(_lru_kernel, cast=cast is not None),
        grid=(batch, nt),
        in_specs=[
            pl.BlockSpec((t, w), lambda b, s: (b * nt + s, COL_XLRU)),
            pl.BlockSpec((t, w), lambda b, s: (b * nt + s, COL_GLRU)),
            _layer_resident((CONV_W, w), layer),
            _layer_resident((1, w), layer),
            _layer_resident((LRU_BLOCKS, LRU_BW, 2 * LRU_BW), layer),
            _layer_resident((1, w), layer),
            _layer_resident((1, w), layer),
            _layer_resident((1, w), layer),
        ] + cast_in,
        out_specs=out_specs,
        out_shape=out_shape,
        scratch_shapes=[pltpu.VMEM((SUBLANES, w), F32), pltpu.VMEM((1, w), F32)],
        compiler_params=_params(("arbitrary", "arbitrary"), vmem + (8 << 20)),
        name="rglru",
    )(*args)


def _log_sigmoid(x):
    return jnp.minimum(x, 0.0) - jnp.log1p(jnp.exp(-jnp.abs(x)))


def _gla_kernel(q_ref, k_ref, v_ref, r_ref, gk_ref, wgk2_ref, bgk_ref, ng_ref, *rest, cast):
    if cast:
        wu_in, wd_in, y_ref, wu_out, wd_out, st_ref = rest
        _cast_blocks(wu_in, wd_in, wu_out, wd_out)
    else:
        y_ref, st_ref = rest
    t_rows = q_ref.shape[0]
    c = GLA_CHUNK
    hk = GLA_H * GLA_DK

    @pl.when(pl.program_id(1) == 0)
    def _():
        st_ref[...] = jnp.zeros_like(st_ref)

    z = jnp.dot(gk_ref[...], wgk2_ref[...], preferred_element_type=F32,
                precision=lax.Precision.HIGHEST) + bgk_ref[...]
    la = _log_sigmoid(z) * (1.0 / GLA_NORMALIZER)
    pos = lax.broadcasted_iota(jnp.int32, (t_rows, hk), 0) & (c - 1)
    dd = 1
    while dd < c:
        la = la + jnp.where(pos >= dd, pltpu.roll(la, dd, 0), 0.0)
        dd *= 2
    bcum = la

    tril = (lax.broadcasted_iota(jnp.int32, (c, c), 0)
            >= lax.broadcasted_iota(jnp.int32, (c, c), 1))
    nt_dims = (((1,), (1,)), ((), ()))
    tn_dims = (((0,), (0,)), ((), ()))
    n_chunks = t_rows // c

    qe_c, g_c, o_intra, upd = [], [], [], []
    for ci in range(n_chunks):
        rows = slice(ci * c, (ci + 1) * c)
        bc = bcum[rows]
        b_last = bcum[(ci + 1) * c - 1:(ci + 1) * c]
        q = q_ref[rows, :].astype(F32) * (GLA_DK ** -0.5)
        k = k_ref[rows, :].astype(F32)
        qe = (q * jnp.exp(bc)).astype(BF16)
        ke = (k * jnp.exp(-bc)).astype(BF16)
        kd = (k * jnp.exp(b_last - bc)).astype(BF16)
        qe_c.append(qe)
        g_c.append(jnp.exp(b_last))
        o_h, upd_h = [], []
        for h in range(GLA_H):
            ks = slice(h * GLA_DK, (h + 1) * GLA_DK)
            v_h = v_ref[rows, h * GLA_DV:(h + 1) * GLA_DV]
            s = lax.dot_general(qe[:, ks], ke[:, ks], nt_dims, preferred_element_type=F32)
            s = jnp.where(tril, s, 0.0).astype(BF16)
            o_h.append(jnp.dot(s, v_h, preferred_element_type=F32))
            upd_h.append(lax.dot_general(v_h, kd[:, ks], tn_dims,
                                         preferred_element_type=F32))
        o_intra.append(o_h)
        upd.append(upd_h)

    for h in range(GLA_H):
        ks = slice(h * GLA_DK, (h + 1) * GLA_DK)
        vs = slice(h * GLA_DV, (h + 1) * GLA_DV)
        st = st_ref[h]
        for ci in range(n_chunks):
            rows = slice(ci * c, (ci + 1) * c)
            o = o_intra[ci][h] + lax.dot_general(qe_c[ci][:, ks], st.astype(BF16), nt_dims,
                                                 preferred_element_type=F32)
            st = st * g_c[ci][:, ks] + upd[ci][h]
            o = o * lax.rsqrt(jnp.mean(o * o, axis=-1, keepdims=True) + EPS) * ng_ref[...]
            o = o * _silu(r_ref[rows, vs].astype(F32))
            y_ref[rows, vs] = o.astype(y_ref.dtype)
        st_ref[h] = st


def _gla(proj, gk, w_gk2_p, b_gk, norm_g, layer, batch, seq, cast=None):
    t = min(GLA_T, seq)
    while cast is not None and batch * (seq // t) < CAST_BLOCKS:
        t //= 2
    nt = seq // t
    hk, hv = GLA_H * GLA_DK, GLA_H * GLA_DV
    vmem = (2 * (2 * t * hk * 2 + 3 * t * hv * 2 + t * GK_PAD * 4) + t * hk * 4
            + GLA_H * GLA_DV * GLA_DK * 4 + 8 * t * hk * 4)
    out_specs = pl.BlockSpec((t, hv), lambda b, s: (b * nt + s, 0))
    out_shape = jax.ShapeDtypeStruct((batch * seq, hv), BF16)
    args = [proj, proj, proj, proj, gk, w_gk2_p, b_gk, norm_g]
    cast_in = []
    if cast is not None:
        cast_in, cast_out, cast_shape, cast_vmem = _cast_payload(
            cast[0], cast[1], cast[2], lambda b, s: b * nt + s)
        out_specs, out_shape = [out_specs] + cast_out, [out_shape] + cast_shape
        args += [cast[0], cast[1]]
        vmem += cast_vmem
    return pl.pallas_call(
        functools.partial(_gla_kernel, cast=cast is not None),
        grid=(batch, nt),
        in_specs=[
            pl.BlockSpec((t, hk), lambda b, s: (b * nt + s, COL_Q)),
            pl.BlockSpec((t, hk), lambda b, s: (b * nt + s, COL_K)),
            pl.BlockSpec((t, hv), lambda b, s: (b * nt + s, COL_V)),
            pl.BlockSpec((t, hv), lambda b, s: (b * nt + s, COL_R)),
            pl.BlockSpec((t, GK_PAD), lambda b, s: (b * nt + s, 0)),
            _layer_resident((GK_PAD, hk), layer),
            _layer_resident((1, hk), layer),
            _layer_resident((1, GLA_DV), layer),
        ] + cast_in,
        out_specs=out_specs,
        out_shape=out_shape,
        scratch_shapes=[pltpu.VMEM((GLA_H, GLA_DV, GLA_DK), F32)],
        compiler_params=_params(("arbitrary", "arbitrary"), vmem + (8 << 20)),
        name="gla",
    )(*args)


def _merge_kernel(x_ref, ya_ref, yb_ref, gl_ref, bg_ref, wb_ref, wo_ref, post_g_ref, o_ref):
    d = x_ref.shape[1]
    z_a = jnp.dot(ya_ref[...], wb_ref[:W_LRU, :], preferred_element_type=F32)
    z_b = jnp.dot(yb_ref[...], wb_ref[W_LRU:, :], preferred_element_type=F32)
    g_a = jax.nn.sigmoid(gl_ref[:, :d].astype(F32) + bg_ref[:, :d])
    g_b = jax.nn.sigmoid(gl_ref[:, d:].astype(F32) + bg_ref[:, d:])
    merged = (g_a * z_a + g_b * z_b).astype(BF16)
    h = jnp.dot(merged, wo_ref[...], preferred_element_type=F32)
    o_ref[...] = x_ref[...] + _rms(h, post_g_ref[...])


def _merge(x2, y_a, y_b, proj, b_gate, w_branch, w_out, post_g, layer):
    m, d = x2.shape
    tm = min(MERGE_TM, m)
    wb_rows = w_branch.shape[1]
    vmem = (2 * 2 * tm * d * 4 + 2 * 2 * tm * W_LRU * 2 + 2 * tm * 2 * d * 2
            + wb_rows * d * 2 + d * d * 2 + 6 * tm * d * 4)
    return pl.pallas_call(
        _merge_kernel,
        grid=(m // tm,),
        in_specs=[
            pl.BlockSpec((tm, d), lambda i: (i, 0)),
            pl.BlockSpec((tm, W_LRU), lambda i: (i, 0)),
            pl.BlockSpec((tm, GLA_H * GLA_DV), lambda i: (i, 0)),
            pl.BlockSpec((tm, N_BRANCH * d), lambda i: (i, COL_GATE)),
            _layer_resident((1, N_BRANCH * d), layer),
            _layer_resident((wb_rows, d), layer),
            _layer_resident((d, d), layer),
            _layer_resident((1, d), layer),
        ],
        out_specs=pl.BlockSpec((tm, d), lambda i: (i, 0)),
        out_shape=jax.ShapeDtypeStruct((m, d), F32),
        compiler_params=_params(("arbitrary",), vmem),
        name="merge",
    )(x2, y_a, y_b, proj, b_gate, w_branch, w_out, post_g)


def _xattn_kernel(x_ref, pre_g_ref, wq_ref, kk_ref, vv_ref, wo_ref, post_g_ref, o_ref):
    xn = _rms(x_ref[...], pre_g_ref[...]).astype(BF16)
    q = (jnp.dot(xn, wq_ref[...], preferred_element_type=F32) * (XA_DH ** -0.5)).astype(BF16)
    nt_dims = (((1,), (1,)), ((), ()))
    outs = []
    for h in range(XA_H):
        hs = slice(h * XA_DH, (h + 1) * XA_DH)
        s = lax.dot_general(q[:, hs], kk_ref[:, hs], nt_dims,
                            preferred_element_type=F32)
        p = jnp.exp(s - jnp.max(s, axis=-1, keepdims=True))
        pv = jnp.dot(p.astype(BF16), vv_ref[:, hs], preferred_element_type=F32)
        outs.append(pv / jnp.sum(p, axis=-1, keepdims=True))
    o = jnp.concatenate(outs, axis=1).astype(BF16)
    h_out = jnp.dot(o, wo_ref[...], preferred_element_type=F32)
    o_ref[...] = x_ref[...] + _rms(h_out, post_g_ref[...])


def _xattn(x2, pre_g, w_q, kv, w_o, post_g, layer, batch, seq, mem_len):
    m, d = x2.shape
    tm = min(XA_TM, seq)
    nt = seq // tm
    hd = XA_H * XA_DH
    vmem = (2 * 2 * tm * d * 4 + 2 * d * hd * 2 + 2 * 2 * mem_len * hd * 2
            + 4 * tm * d * 4 + 8 * tm * mem_len * 4)
    return pl.pallas_call(
        _xattn_kernel,
        grid=(batch, nt),
        in_specs=[
            pl.BlockSpec((tm, d), lambda b, s: (b * nt + s, 0)),
            _layer_resident((1, d), layer),
            _layer_resident((d, hd), layer),
            pl.BlockSpec((mem_len, hd), lambda b, s: (b, 0)),
            pl.BlockSpec((mem_len, hd), lambda b, s: (b, 1)),
            _layer_resident((hd, d), layer),
            _layer_resident((1, d), layer),
        ],
        out_specs=pl.BlockSpec((tm, d), lambda b, s: (b * nt + s, 0)),
        out_shape=jax.ShapeDtypeStruct((m, d), F32),
        compiler_params=_params(("arbitrary", "arbitrary"), vmem),
        name="xattn",
    )(x2, pre_g, w_q, kv, kv, w_o, post_g)


def _pack_w_in(w_in):
    n_head = 2 * W_LRU + 2 * GLA_H * GLA_DK + 2 * GLA_H * GLA_DV
    head = w_in[..., :n_head]
    gate = w_in[..., n_head + GLA_RANK:]
    w_main = jnp.concatenate([gate, head], axis=-1).astype(BF16)
    w_gk = w_in[..., n_head:n_head + GK_PAD].astype(BF16)
    return w_main, w_gk


def _rows(v):
    return v.reshape(v.shape[0], 1, -1)


def kernel(x, mem, ffn1_pre_g, ffn1_post_g, ffn1_w_up, ffn1_w_down, mix_pre_g, mix_post_g, w_in,
           conv_w, conv_b, lru_w_a, lru_b_a, lru_w_i, lru_b_i, lru_lambda, gla_w_gk2, gla_b_gk,
           gla_norm_g, b_gate, w_branch, w_out, xa_pre_g, xa_post_g, mem_g, xa_w_q, xa_w_kv,
           xa_w_o, ffn2_pre_g, ffn2_post_g, ffn2_w_up, ffn2_w_down):
    batch, seq, d = x.shape
    mem_len = mem.shape[1]
    depth = ffn1_w_up.shape[0]
    x2 = x.reshape(batch * seq, d)
    mem2 = mem.reshape(batch * mem_len, d)

    ffn1_up, ffn1_down = ffn1_w_up[0].astype(BF16), ffn1_w_down[0].astype(BF16)
    w_main, w_gk = _pack_w_in(w_in)
    w_ai = jnp.concatenate([lru_w_a, lru_w_i], axis=-1).astype(BF16)
    w_gk2_p = jnp.pad(gla_w_gk2, ((0, 0), (0, GK_PAD - GLA_RANK), (0, 0)))
    wb, wo = w_branch.astype(BF16), w_out.astype(BF16)
    xa_q, xa_kv, xa_o = xa_w_q.astype(BF16), xa_w_kv.astype(BF16), xa_w_o.astype(BF16)
    ffn1_pre, ffn1_post = _rows(ffn1_pre_g), _rows(ffn1_post_g)
    ffn2_pre, ffn2_post = _rows(ffn2_pre_g), _rows(ffn2_post_g)
    mix_pre, mix_post = _rows(mix_pre_g), _rows(mix_post_g)
    xa_pre, xa_post, mem_gain = _rows(xa_pre_g), _rows(xa_post_g), _rows(mem_g)
    conv_bias, b_a, b_i, lam = _rows(conv_b), _rows(lru_b_a), _rows(lru_b_i), _rows(lru_lambda)
    b_gk, norm_g, b_gate_r = _rows(gla_b_gk), _rows(gla_norm_g), _rows(b_gate)

    for l in range(depth):
        x2 = _ffn(x2, ffn1_pre, ffn1_up, ffn1_down, ffn1_post, l)

        proj, gk = _norm_matmul(x2, mix_pre, w_main, l, w_side=w_gk, name="mix_in_proj")
        y_a, ffn2_up, ffn2_down = _lru(proj, conv_w, conv_bias, w_ai, b_a, b_i, lam, l, batch, seq,
                                       cast=(ffn2_w_up, ffn2_w_down, l))
        if l + 1 < depth:
            y_b, ffn1_up, ffn1_down = _gla(proj, gk, w_gk2_p, b_gk, norm_g, l, batch, seq,
                                           cast=(ffn1_w_up, ffn1_w_down, l + 1))
        else:
            y_b = _gla(proj, gk, w_gk2_p, b_gk, norm_g, l, batch, seq)
        x2 = _merge(x2, y_a, y_b, proj, b_gate_r, wb, wo, mix_post, l)

        kv = _norm_matmul(mem2, mem_gain, xa_kv, l, name="mem_kv_proj")
        x2 = _xattn(x2, xa_pre, xa_q, kv, xa_o, xa_post, l, batch, seq, mem_len)

        x2 = _ffn(x2, ffn2_pre, ffn2_up, ffn2_down, ffn2_post, l)

    return x2.reshape(batch, seq, d)
```

```python
import functools
import math

import jax
import jax.numpy as jnp
from jax import lax
from jax.experimental import pallas as pl
from jax.experimental.pallas import tpu as pltpu

F32 = jnp.float32
BF16 = jnp.bfloat16

D_MODEL = 2048
D_FF = 5504
FFN_RES_SCALE = 0.5
W_LRU = D_MODEL // 2
LRU_BLOCKS = 8
LRU_BW = W_LRU // LRU_BLOCKS
CONV_W = 4
LRU_C = 8.0
GLA_H = 4
GLA_DK = 128
GLA_DV = 256
GLA_RANK = 16
GLA_NORMALIZER = 16.0
GLA_CHUNK = 64
XA_H = 4
XA_DH = 128
N_BRANCH = 2
EPS = 1e-6

LANES = 128
SUBLANES = 8
V7X_VMEM_BYTES = 64 * 1024 * 1024
VMEM_LIMIT_CAP = V7X_VMEM_BYTES - 6 * 1024 * 1024

FFN_TF = 512
FFN_NF = -(-D_FF // FFN_TF)
FFN_TM = 1024
FFN_SLABS = 8
PROJ_TM = 1024
PROJ_TN = 2304
N_PROJ = N_BRANCH * D_MODEL + 2 * W_LRU + 2 * GLA_H * GLA_DV + 2 * GLA_H * GLA_DK
GK_PAD = LANES
LRU_T = 256
GLA_T = 512
MERGE_TM = 512
XA_TM = 512

COL_GATE = 0
COL_XLRU = (N_BRANCH * D_MODEL) // W_LRU
COL_GLRU = COL_XLRU + 1
COL_Q = (N_BRANCH * D_MODEL + 2 * W_LRU) // (GLA_H * GLA_DK)
COL_K = COL_Q + 1
COL_V = (N_BRANCH * D_MODEL + 2 * W_LRU + 2 * GLA_H * GLA_DK) // (GLA_H * GLA_DV)
COL_R = COL_V + 1


def _params(semantics, vmem_bytes):
    return pltpu.CompilerParams(dimension_semantics=semantics,
                                vmem_limit_bytes=int(min(VMEM_LIMIT_CAP, vmem_bytes)))


def _layer_resident(tail, layer):
    zeros = (0,) * len(tail)
    return pl.BlockSpec((None,) + tuple(tail), lambda *_: (layer,) + zeros,
                        pipeline_mode=pl.Buffered(1))


def _rms(x, g):
    ms = jnp.mean(x * x, axis=-1, keepdims=True)
    return x * lax.rsqrt(ms + EPS) * g


def _silu(x):
    return x * jax.nn.sigmoid(x)


def _ffn_window_start(j, base=0):
    return LANES * (base // LANES + jnp.minimum(j * (FFN_TF // LANES), (D_FF - FFN_TF) // LANES))


def _ffn_kernel(xnext_ref, xprev_ref, pre_g_ref, wg_ref, wu_ref, wd_ref, post_g_ref, o_ref,
                xn_even, xn_odd, acc_even, acc_odd, *, n_tiles):
    r = pl.program_id(0)
    j = pl.program_id(1)
    nf = pl.num_programs(1)
    slab = xnext_ref.shape[0]
    n_slabs = xn_even.shape[0] // slab
    row0 = pl.multiple_of(jnp.minimum(j, n_slabs - 1) * slab, slab)
    group = 2 * SUBLANES

    def pre_norm(xn_dst):
        for g0 in range(0, slab, group):
            y = _rms(xnext_ref[g0:g0 + group, :], pre_g_ref[...])
            xn_dst[pl.ds(row0 + g0, group), :] = y.astype(BF16)

    def matmul_step(xn_src, acc):
        xn = xn_src[...]
        gate = jnp.dot(xn, wg_ref[...], preferred_element_type=F32)
        up = jnp.dot(xn, wu_ref[...], preferred_element_type=F32)
        act = _silu(gate) * up
        covered = jnp.where(j == nf - 1, nf * FFN_TF - D_FF, 0)
        col = lax.broadcasted_iota(jnp.int32, act.shape, 1)
        act = jnp.where(col >= covered, act, 0.0).astype(BF16)
        prev = jnp.where(j == 0, 0.0, acc[...])
        acc[...] = prev + jnp.dot(act, wd_ref[...], preferred_element_type=F32)

    def post_norm(acc_src):
        for g0 in range(0, slab, group):
            h = acc_src[pl.ds(row0 + g0, group), :]
            o_ref[g0:g0 + group, :] = (xprev_ref[g0:g0 + group, :]
                                       + FFN_RES_SCALE * _rms(h, post_g_ref[...]))

    @pl.when(r == 0)
    def _():
        @pl.when(j == 0)
        def _():
            acc_even[...] = jnp.zeros_like(acc_even)
            acc_odd[...] = jnp.zeros_like(acc_odd)

        pre_norm(xn_even)

    steady = (r >= 1) & (r <= n_tiles)

    @pl.when(steady & (r % 2 == 1))
    def _():
        pre_norm(xn_odd)
        post_norm(acc_odd)
        matmul_step(xn_even, acc_even)

    @pl.when(steady & (r % 2 == 0))
    def _():
        pre_norm(xn_even)
        post_norm(acc_even)
        matmul_step(xn_odd, acc_odd)

    @pl.when(r == n_tiles + 1)
    def _():
        post_norm(acc_odd if (n_tiles - 1) % 2 else acc_even)


def _ffn(x2, pre_g, w_up, w_down, post_g, layer):
    m, d = x2.shape
    tm, tf = min(FFN_TM, m), FFN_TF
    n_tiles = m // tm
    slab = tm // FFN_SLABS
    vmem = (2 * tm * d * (2 + 4)
            + 2 * 3 * d * tf * 2
            + 3 * 2 * slab * d * 4
            + 2 * tm * d * 4 + 4 * tm * tf * 4)

    def next_slab(r, j):
        return (jnp.minimum(r, n_tiles - 1) * FFN_SLABS + jnp.minimum(j, FFN_SLABS - 1), 0)

    def prev_slab(r, j):
        tile = jnp.minimum(r - 2, n_tiles - 1)
        return (jnp.where(r < 2, 0, tile * FFN_SLABS + jnp.minimum(j, FFN_SLABS - 1)), 0)

    def window(r, j):
        return jnp.where(r == 0, 0, jnp.where(r == n_tiles + 1, FFN_NF - 1, j))

    return pl.pallas_call(
        functools.partial(_ffn_kernel, n_tiles=n_tiles),
        grid=(n_tiles + 2, FFN_NF),
        in_specs=[
            pl.BlockSpec((slab, d), next_slab),
            pl.BlockSpec((slab, d), prev_slab),
            _layer_resident((1, d), layer),
            pl.BlockSpec((pl.Element(d), pl.Element(tf)),
                         lambda r, j: (0, _ffn_window_start(window(r, j)))),
            pl.BlockSpec((pl.Element(d), pl.Element(tf)),
                         lambda r, j: (0, _ffn_window_start(window(r, j), base=D_FF))),
            pl.BlockSpec((pl.Element(tf), pl.Element(d)),
                         lambda r, j: (_ffn_window_start(window(r, j)), 0)),
            _layer_resident((1, d), layer),
        ],
        out_specs=pl.BlockSpec((slab, d), prev_slab),
        out_shape=jax.ShapeDtypeStruct((m, d), F32),
        scratch_shapes=[pltpu.VMEM((tm, d), BF16), pltpu.VMEM((tm, d), BF16),
                        pltpu.VMEM((tm, d), F32), pltpu.VMEM((tm, d), F32)],
        compiler_params=_params(("arbitrary", "arbitrary"), vmem),
        name="ffn",
    )(x2, x2, pre_g, w_up, w_up, w_down, post_g)


CAST_BLOCKS = D_FF // LANES
CAST_UP_COLS = 2 * D_FF // CAST_BLOCKS
CAST_DOWN_ROWS = D_FF // CAST_BLOCKS


def _cast_payload(w_up, w_down, layer, step_of):
    d = w_up.shape[1]

    def blk(*idx):
        return jnp.minimum(step_of(*idx), CAST_BLOCKS - 1)

    in_specs = [pl.BlockSpec((None, d, CAST_UP_COLS), lambda *idx: (layer, 0, blk(*idx))),
                pl.BlockSpec((None, CAST_DOWN_ROWS, d), lambda *idx: (layer, blk(*idx), 0))]
    out_specs = [pl.BlockSpec((d, CAST_UP_COLS), lambda *idx: (0, blk(*idx))),
                 pl.BlockSpec((CAST_DOWN_ROWS, d), lambda *idx: (blk(*idx), 0))]
    out_shape = [jax.ShapeDtypeStruct(w_up.shape[1:], BF16),
                 jax.ShapeDtypeStruct(w_down.shape[1:], BF16)]
    vmem = 2 * (d * CAST_UP_COLS + CAST_DOWN_ROWS * d) * (4 + 2)
    return in_specs, out_specs, out_shape, vmem


def _cast_blocks(wu_in, wd_in, wu_out, wd_out):
    wu_out[...] = wu_in[...].astype(BF16)
    wd_out[...] = wd_in[...].astype(BF16)


def _norm_matmul_kernel(x_ref, g_ref, w_ref, o_ref, xn_ref):
    @pl.when(pl.program_id(1) == 0)
    def _():
        xn_ref[...] = _rms(x_ref[...], g_ref[...]).astype(BF16)

    o_ref[...] = jnp.dot(xn_ref[...], w_ref[...], preferred_element_type=F32).astype(o_ref.dtype)


def _norm_matmul_side_kernel(x_ref, g_ref, w_ref, ws_ref, o_ref, side_ref, xn_ref):
    @pl.when(pl.program_id(1) == 0)
    def _():
        xn = _rms(x_ref[...], g_ref[...]).astype(BF16)
        xn_ref[...] = xn
        side_ref[...] = jnp.dot(xn, ws_ref[...], preferred_element_type=F32)

    o_ref[...] = jnp.dot(xn_ref[...], w_ref[...], preferred_element_type=F32).astype(o_ref.dtype)


def _norm_matmul(x2, g, w, layer, w_side=None, name="norm_matmul"):
    m, d = x2.shape
    n = w.shape[2]
    tm, tn = min(PROJ_TM, m), min(PROJ_TN, n)
    vmem = (2 * tm * d * 4 + tm * d * 2 + 2 * d * tn * 2 + 2 * tm * tn * 2
            + 2 * tm * tn * 4 + tm * d * 4)
    in_specs = [
        pl.BlockSpec((tm, d), lambda i, j: (i, 0)),
        _layer_resident((1, d), layer),
        pl.BlockSpec((None, d, tn), lambda i, j: (layer, 0, j)),
    ]
    out_specs = pl.BlockSpec((tm, tn), lambda i, j: (i, j))
    out_shape = jax.ShapeDtypeStruct((m, n), BF16)
    args = [x2, g, w]
    kern = _norm_matmul_kernel
    if w_side is not None:
        ns = w_side.shape[2]
        in_specs.append(_layer_resident((d, ns), layer))
        out_specs = [out_specs, pl.BlockSpec((tm, ns), lambda i, j: (i, 0))]
        out_shape = [out_shape, jax.ShapeDtypeStruct((m, ns), F32)]
        args.append(w_side)
        kern = _norm_matmul_side_kernel
        vmem += d * ns * 2 + 2 * tm * ns * 4
    return pl.pallas_call(
        kern,
        grid=(m // tm, n // tn),
        in_specs=in_specs,
        out_specs=out_specs,
        out_shape=out_shape,
        scratch_shapes=[pltpu.VMEM((tm, d), BF16)],
        compiler_params=_params(("arbitrary", "arbitrary"), vmem),
        name=name,
    )(*args)


def _gelu_tanh(x):
    c = math.sqrt(2.0 / math.pi)
    return 0.5 * x * (1.0 + jnp.tanh(c * (x + 0.044715 * (x * x * x))))


def _softplus(x):
    return jnp.maximum(x, 0.0) + jnp.log1p(jnp.exp(-jnp.abs(x)))


def _lru_kernel(xl_ref, gl_ref, cw_ref, cb_ref, wai_ref, ba_ref, bi_ref, lam_ref, *rest, cast):
    if cast:
        wu_in, wd_in, y_ref, wu_out, wd_out, tail_ref, h_ref = rest
        _cast_blocks(wu_in, wd_in, wu_out, wd_out)
    else:
        y_ref, tail_ref, h_ref = rest
    t_rows = xl_ref.shape[0]
    nblk = t_rows // SUBLANES

    @pl.when(pl.program_id(1) == 0)
    def _():
        tail_ref[...] = jnp.zeros_like(tail_ref)
        h_ref[...] = jnp.zeros_like(h_ref)

    row8 = lax.broadcasted_iota(jnp.int32, (SUBLANES, LRU_BW), 0)
    sub3 = lax.broadcasted_iota(jnp.int32, (nblk, SUBLANES, LRU_BW), 1)

    for n in range(LRU_BLOCKS):
        cs = slice(n * LRU_BW, (n + 1) * LRU_BW)
        x = xl_ref[:, cs].astype(F32)
        prev8 = tail_ref[:, cs]
        xc = x * cw_ref[CONV_W - 1:CONV_W, cs] + cb_ref[:, cs]
        for s in range(1, CONV_W):
            xs = pltpu.roll(x, s, 0)
            ps = pltpu.roll(prev8, s, 0)
            head = jnp.where(row8 < s, ps, xs[:SUBLANES])
            xs = jnp.concatenate([head, xs[SUBLANES:]], axis=0)
            xc = xc + xs * cw_ref[CONV_W - 1 - s:CONV_W - s, cs]
        tail_ref[:, cs] = x[t_rows - SUBLANES:]

        pre = jnp.dot(xc.astype(BF16), wai_ref[n], preferred_element_type=F32)
        r = jax.nn.sigmoid(pre[:, :LRU_BW] + ba_ref[:, cs])
        i = jax.nn.sigmoid(pre[:, LRU_BW:] + bi_ref[:, cs])
        log_a = (-LRU_C * _softplus(-lam_ref[:, cs])) * r
        a = jnp.exp(log_a)
        u = jnp.sqrt(-jnp.tanh(log_a) * (1.0 + a * a)) * (i * xc)

        a3 = a.reshape(nblk, SUBLANES, LRU_BW)
        u3 = u.reshape(nblk, SUBLANES, LRU_BW)
        for dd in (1, 2, 4):
            keep = sub3 >= dd
            a_s = jnp.where(keep, pltpu.roll(a3, dd, 1), 1.0)
            u_s = jnp.where(keep, pltpu.roll(u3, dd, 1), 0.0)
            u3 = a3 * u_s + u3
            a3 = a3 * a_s
        carry = jnp.broadcast_to(h_ref[:, cs], (SUBLANES, LRU_BW))
        hs = []
        for b in range(nblk):
            hb = u3[b] + a3[b] * carry
            hs.append(hb)
            carry = jnp.broadcast_to(hb[SUBLANES - 1:SUBLANES], (SUBLANES, LRU_BW))
        h = jnp.concatenate(hs, axis=0)
        h_ref[:, cs] = carry[0:1]

        y = h * _gelu_tanh(gl_ref[:, cs].astype(F32))
        y_ref[:, cs] = y.astype(y_ref.dtype)


def _lru(proj, conv_w, conv_b, w_ai, b_a, b_i, lam, layer, batch, seq, cast=None):
    t = min(LRU_T, seq)
    nt = seq // t
    w = W_LRU
    vmem = 2 * 3 * t * w * 2 + LRU_BLOCKS * LRU_BW * 2 * LRU_BW * 2 + 64 * t * LRU_BW * 4
    out_specs = pl.BlockSpec((t, w), lambda b, s: (b * nt + s, 0))
    out_shape = jax.ShapeDtypeStruct((batch * seq, w), BF16)
    args = [proj, proj, conv_w, conv_b, w_ai, b_a, b_i, lam]
    cast_in = []
    if cast is not None:
        assert batch * nt >= CAST_BLOCKS
        cast_in, cast_out, cast_shape, cast_vmem = _cast_payload(
            cast[0], cast[1], cast[2], lambda b, s: b * nt + s)
        out_specs, out_shape = [out_specs] + cast_out, [out_shape] + cast_shape
        args += [cast[0], cast[1]]
        vmem += cast_vmem
    return pl.pallas_call(
        functools.partial(_lru_kernel, cast=cast is not None),
        grid=(batch, nt),
        in_specs=[
            pl.BlockSpec((t, w), lambda b, s: (b * nt + s, COL_XLRU)),
            pl.BlockSpec((t, w), lambda b, s: (b * nt + s, COL_GLRU)),
            _layer_resident((CONV_W, w), layer),
            _layer_resident((1, w), layer),
            _layer_resident((LRU_BLOCKS, LRU_BW, 2 * LRU_BW), layer),
            _layer_resident((1, w), layer),
            _layer_resident((1, w), layer),
            _layer_resident((1, w), layer),
        ] + cast_in,
        out_specs=out_specs,
        out_shape=out_shape,
        scratch_shapes=[pltpu.VMEM((SUBLANES, w), F32), pltpu.VMEM((1, w), F32)],
        compiler_params=_params(("arbitrary", "arbitrary"), vmem + (8 << 20)),
        name="rglru",
    )(*args)


def _log_sigmoid(x):
    return jnp.minimum(x, 0.0) - jnp.log1p(jnp.exp(-jnp.abs(x)))


def _gla_kernel(q_ref, k_ref, v_ref, r_ref, gk_ref, wgk2_ref, bgk_ref, ng_ref, *rest, cast):
    if cast:
        wu_in, wd_in, y_ref, wu_out, wd_out, st_ref = rest
        _cast_blocks(wu_in, wd_in, wu_out, wd_out)
    else:
        y_ref, st_ref = rest
    t_rows = q_ref.shape[0]
    c = GLA_CHUNK
    hk = GLA_H * GLA_DK

    @pl.when(pl.program_id(1) == 0)
    def _():
        st_ref[...] = jnp.zeros_like(st_ref)

    z = jnp.dot(gk_ref[...], wgk2_ref[...], preferred_element_type=F32,
                precision=lax.Precision.HIGHEST) + bgk_ref[...]
    la = _log_sigmoid(z) * (1.0 / GLA_NORMALIZER)
    pos = lax.broadcasted_iota(jnp.int32, (t_rows, hk), 0) & (c - 1)
    dd = 1
    while dd < c:
        la = la + jnp.where(pos >= dd, pltpu.roll(la, dd, 0), 0.0)
        dd *= 2
    bcum = la

    tril = (lax.broadcasted_iota(jnp.int32, (c, c), 0)
            >= lax.broadcasted_iota(jnp.int32, (c, c), 1))
    nt_dims = (((1,), (1,)), ((), ()))
    tn_dims = (((0,), (0,)), ((), ()))
    n_chunks = t_rows // c

    qe_c, g_c, o_intra, upd = [], [], [], []
    for ci in range(n_chunks):
        rows = slice(ci * c, (ci + 1) * c)
        bc = bcum[rows]
        b_last = bcum[(ci + 1) * c - 1:(ci + 1) * c]
        q = q_ref[rows, :].astype(F32) * (GLA_DK ** -0.5)
        k = k_ref[rows, :].astype(F32)
        qe = (q * jnp.exp(bc)).astype(BF16)
        ke = (k * jnp.exp(-bc)).astype(BF16)
        kd = (k * jnp.exp(b_last - bc)).astype(BF16)
        qe_c.append(qe)
        g_c.append(jnp.exp(b_last))
        o_h, upd_h = [], []
        for h in range(GLA_H):
            ks = slice(h * GLA_DK, (h + 1) * GLA_DK)
            v_h = v_ref[rows, h * GLA_DV:(h + 1) * GLA_DV]
            s = lax.dot_general(qe[:, ks], ke[:, ks], nt_dims, preferred_element_type=F32)
            s = jnp.where(tril, s, 0.0).astype(BF16)
            o_h.append(jnp.dot(s, v_h, preferred_element_type=F32))
            upd_h.append(lax.dot_general(v_h, kd[:, ks], tn_dims,
                                         preferred_element_type=F32))
        o_intra.append(o_h)
        upd.append(upd_h)

    for h in range(GLA_H):
        ks = slice(h * GLA_DK, (h + 1) * GLA_DK)
        vs = slice(h * GLA_DV, (h + 1) * GLA_DV)
        st = st_ref[h]
        for ci in range(n_chunks):
            rows = slice(ci * c, (ci + 1) * c)
            o = o_intra[ci][h] + lax.dot_general(qe_c[ci][:, ks], st.astype(BF16), nt_dims,
                                                 preferred_element_type=F32)
            st = st * g_c[ci][:, ks] + upd[ci][h]
            o = o * lax.rsqrt(jnp.mean(o * o, axis=-1, keepdims=True) + EPS) * ng_ref[...]
            o = o * _silu(r_ref[rows, vs].astype(F32))
            y_ref[rows, vs] = o.astype(y_ref.dtype)
        st_ref[h] = st


def _gla(proj, gk, w_gk2_p, b_gk, norm_g, layer, batch, seq, cast=None):
    t = min(GLA_T, seq)
    while cast is not None and batch * (seq // t) < CAST_BLOCKS:
        t //= 2
    nt = seq // t
    hk, hv = GLA_H * GLA_DK, GLA_H * GLA_DV
    vmem = (2 * (2 * t * hk * 2 + 3 * t * hv * 2 + t * GK_PAD * 4) + t * hk * 4
            + GLA_H * GLA_DV * GLA_DK * 4 + 8 * t * hk * 4)
    out_specs = pl.BlockSpec((t, hv), lambda b, s: (b * nt + s, 0))
    out_shape = jax.ShapeDtypeStruct((batch * seq, hv), BF16)
    args = [proj, proj, proj, proj, gk, w_gk2_p, b_gk, norm_g]
    cast_in = []
    if cast is not None:
        cast_in, cast_out, cast_shape, cast_vmem = _cast_payload(
            cast[0], cast[1], cast[2], lambda b, s: b * nt + s)
        out_specs, out_shape = [out_specs] + cast_out, [out_shape] + cast_shape
        args += [cast[0], cast[1]]
        vmem += cast_vmem
    return pl.pallas_call(
        functools.partial(_gla_kernel, cast=cast is not None),
        grid=(batch, nt),
        in_specs=[
            pl.BlockSpec((t, hk), lambda b, s: (b * nt + s, COL_Q)),
            pl.BlockSpec((t, hk), lambda b, s: (b * nt + s, COL_K)),
            pl.BlockSpec((t, hv), lambda b, s: (b * nt + s, COL_V)),
            pl.BlockSpec((t, hv), lambda b, s: (b * nt + s, COL_R)),
            pl.BlockSpec((t, GK_PAD), lambda b, s: (b * nt + s, 0)),
            _layer_resident((GK_PAD, hk), layer),
            _layer_resident((1, hk), layer),
            _layer_resident((1, GLA_DV), layer),
        ] + cast_in,
        out_specs=out_specs,
        out_shape=out_shape,
        scratch_shapes=[pltpu.VMEM((GLA_H, GLA_DV, GLA_DK), F32)],
        compiler_params=_params(("arbitrary", "arbitrary"), vmem + (8 << 20)),
        name="gla",
    )(*args)


def _merge_kernel(x_ref, ya_ref, yb_ref, gl_ref, bg_ref, wb_ref, wo_ref, post_g_ref, o_ref):
    d = x_ref.shape[1]
    z_a = jnp.dot(ya_ref[...], wb_ref[:W_LRU, :], preferred_element_type=F32)
    z_b = jnp.dot(yb_ref[...], wb_ref[W_LRU:, :], preferred_element_type=F32)
    g_a = jax.nn.sigmoid(gl_ref[:, :d].astype(F32) + bg_ref[:, :d])
    g_b = jax.nn.sigmoid(gl_ref[:, d:].astype(F32) + bg_ref[:, d:])
    merged = (g_a * z_a + g_b * z_b).astype(BF16)
    h = jnp.dot(merged, wo_ref[...], preferred_element_type=F32)
    o_ref[...] = x_ref[...] + _rms(h, post_g_ref[...])


def _merge(x2, y_a, y_b, proj, b_gate, w_branch, w_out, post_g, layer):
    m, d = x2.shape
    tm = min(MERGE_TM, m)
    wb_rows = w_branch.shape[1]
    vmem = (2 * 2 * tm * d * 4 + 2 * 2 * tm * W_LRU * 2 + 2 * tm * 2 * d * 2
            + wb_rows * d * 2 + d * d * 2 + 6 * tm * d * 4)
    return pl.pallas_call(
        _merge_kernel,
        grid=(m // tm,),
        in_specs=[
            pl.BlockSpec((tm, d), lambda i: (i, 0)),
            pl.BlockSpec((tm, W_LRU), lambda i: (i, 0)),
            pl.BlockSpec((tm, GLA_H * GLA_DV), lambda i: (i, 0)),
            pl.BlockSpec((tm, N_BRANCH * d), lambda i: (i, COL_GATE)),
            _layer_resident((1, N_BRANCH * d), layer),
            _layer_resident((wb_rows, d), layer),
            _layer_resident((d, d), layer),
            _layer_resident((1, d), layer),
        ],
        out_specs=pl.BlockSpec((tm, d), lambda i: (i, 0)),
        out_shape=jax.ShapeDtypeStruct((m, d), F32),
        compiler_params=_params(("arbitrary",), vmem),
        name="merge",
    )(x2, y_a, y_b, proj, b_gate, w_branch, w_out, post_g)


def _xattn_kernel(x_ref, pre_g_ref, wq_ref, kk_ref, vv_ref, wo_ref, post_g_ref, o_ref):
    xn = _rms(x_ref[...], pre_g_ref[...]).astype(BF16)
    q = (jnp.dot(xn, wq_ref[...], preferred_element_type=F32) * (XA_DH ** -0.5)).astype(BF16)
    nt_dims = (((1,), (1,)), ((), ()))
    outs = []
    for h in range(XA_H):
        hs = slice(h * XA_DH, (h + 1) * XA_DH)
        s = lax.dot_general(q[:, hs], kk_ref[:, hs], nt_dims,
                            preferred_element_type=F32)
        p = jnp.exp(s - jnp.max(s, axis=-1, keepdims=True))
        pv = jnp.dot(p.astype(BF16), vv_ref[:, hs], preferred_element_type=F32)
        outs.append(pv / jnp.sum(p, axis=-1, keepdims=True))
    o = jnp.concatenate(outs, axis=1).astype(BF16)
    h_out = jnp.dot(o, wo_ref[...], preferred_element_type=F32)
    o_ref[...] = x_ref[...] + _rms(h_out, post_g_ref[...])


def _xattn(x2, pre_g, w_q, kv, w_o, post_g, layer, batch, seq, mem_len):
    m, d = x2.shape
    tm = min(XA_TM, seq)
    nt = seq // tm
    hd = XA_H * XA_DH
    vmem = (2 * 2 * tm * d * 4 + 2 * d * hd * 2 + 2 * 2 * mem_len * hd * 2
            + 4 * tm * d * 4 + 8 * tm * mem_len * 4)
    return pl.pallas_call(
        _xattn_kernel,
        grid=(batch, nt),
        in_specs=[
            pl.BlockSpec((tm, d), lambda b, s: (b * nt + s, 0)),
            _layer_resident((1, d), layer),
            _layer_resident((d, hd), layer),
            pl.BlockSpec((mem_len, hd), lambda b, s: (b, 0)),
            pl.BlockSpec((mem_len, hd), lambda b, s: (b, 1)),
            _layer_resident((hd, d), layer),
            _layer_resident((1, d), layer),
        ],
        out_specs=pl.BlockSpec((tm, d), lambda b, s: (b * nt + s, 0)),
        out_shape=jax.ShapeDtypeStruct((m, d), F32),
        compiler_params=_params(("arbitrary", "arbitrary"), vmem),
        name="xattn",
    )(x2, pre_g, w_q, kv, kv, w_o, post_g)


def _pack_w_in(w_in):
    n_head = 2 * W_LRU + 2 * GLA_H * GLA_DK + 2 * GLA_H * GLA_DV
    head = w_in[..., :n_head]
    gate = w_in[..., n_head + GLA_RANK:]
    w_main = jnp.concatenate([gate, head], axis=-1).astype(BF16)
    w_gk = w_in[..., n_head:n_head + GK_PAD].astype(BF16)
    return w_main, w_gk


def _rows(v):
    return v.reshape(v.shape[0], 1, -1)


def kernel(x, mem, ffn1_pre_g, ffn1_post_g, ffn1_w_up, ffn1_w_down, mix_pre_g, mix_post_g, w_in,
           conv_w, conv_b, lru_w_a, lru_b_a, lru_w_i, lru_b_i, lru_lambda, gla_w_gk2, gla_b_gk,
           gla_norm_g, b_gate, w_branch, w_out, xa_pre_g, xa_post_g, mem_g, xa_w_q, xa_w_kv,
           xa_w_o, ffn2_pre_g, ffn2_post_g, ffn2_w_up, ffn2_w_down):
    batch, seq, d = x.shape
    mem_len = mem.shape[1]
    depth = ffn1_w_up.shape[0]
    x2 = x.reshape(batch * seq, d)
    mem2 = mem.reshape(batch * mem_len, d)

    ffn1_up, ffn1_down = ffn1_w_up[0].astype(BF16), ffn1_w_down[0].astype(BF16)
    w_main, w_gk = _pack_w_in(w_in)
    w_ai = jnp.concatenate([lru_w_a, lru_w_i], axis=-1).astype(BF16)
    w_gk2_p = jnp.pad(gla_w_gk2, ((0, 0), (0, GK_PAD - GLA_RANK), (0, 0)))
    wb, wo = w_branch.astype(BF16), w_out.astype(BF16)
    xa_q, xa_kv, xa_o = xa_w_q.astype(BF16), xa_w_kv.astype(BF16), xa_w_o.astype(BF16)
    ffn1_pre, ffn1_post = _rows(ffn1_pre_g), _rows(ffn1_post_g)
    ffn2_pre, ffn2_post = _rows(ffn2_pre_g), _rows(ffn2_post_g)
    mix_pre, mix_post = _rows(mix_pre_g), _rows(mix_post_g)
    xa_pre, xa_post, mem_gain = _rows(xa_pre_g), _rows(xa_post_g), _rows(mem_g)
    conv_bias, b_a, b_i, lam = _rows(conv_b), _rows(lru_b_a), _rows(lru_b_i), _rows(lru_lambda)
    b_gk, norm_g, b_gate_r = _rows(gla_b_gk), _rows(gla_norm_g), _rows(b_gate)

    for l in range(depth):
        x2 = _ffn(x2, ffn1_pre, ffn1_up, ffn1_down, ffn1_post, l)

        proj, gk = _norm_matmul(x2, mix_pre, w_main, l, w_side=w_gk, name="mix_in_proj")
        y_a, ffn2_up, ffn2_down = _lru(proj, conv_w, conv_bias, w_ai, b_a, b_i, lam, l, batch, seq,
                                       cast=(ffn2_w_up, ffn2_w_down, l))
        if l + 1 < depth:
            y_b, ffn1_up, ffn1_down = _gla(proj, gk, w_gk2_p, b_gk, norm_g, l, batch, seq,
                                           cast=(ffn1_w_up, ffn1_w_down, l + 1))
        else:
            y_b = _gla(proj, gk, w_gk2_p, b_gk, norm_g, l, batch, seq)
        x2 = _merge(x2, y_a, y_b, proj, b_gate_r, wb, wo, mix_post, l)

        kv = _norm_matmul(mem2, mem_gain, xa_kv, l, name="mem_kv_proj")
        x2 = _xattn(x2, xa_pre, xa_q, kv, xa_o, xa_post, l, batch, seq, mem_len)

        x2 = _ffn(x2, ffn2_pre, ffn2_up, ffn2_down, ffn2_post, l)

    return x2.reshape(batch, seq, d)
```

```python
import functools
import math

import jax
import jax.numpy as jnp
from jax import lax
from jax.experimental import pallas as pl
from jax.experimental.pallas import tpu as pltpu

F32 = jnp.float32
BF16 = jnp.bfloat16

D_MODEL = 2048
D_FF = 5504
FFN_RES_SCALE = 0.5
W_LRU = D_MODEL // 2
LRU_BLOCKS = 8
LRU_BW = W_LRU // LRU_BLOCKS
CONV_W = 4
LRU_C = 8.0
GLA_H = 4
GLA_DK = 128
GLA_DV = 256
GLA_RANK = 16
GLA_NORMALIZER = 16.0
GLA_CHUNK = 64
XA_H = 4
XA_DH = 128
N_BRANCH = 2
EPS = 1e-6

LANES = 128
SUBLANES = 8
V7X_VMEM_BYTES = 64 * 1024 * 1024
VMEM_LIMIT_CAP = V7X_VMEM_BYTES - 6 * 1024 * 1024

FFN_TF = 512
FFN_NF = -(-D_FF // FFN_TF)
FFN_TM = 1024
FFN_SLABS = 8
PROJ_TM = 1024
PROJ_TN = 2304
N_PROJ = N_BRANCH * D_MODEL + 2 * W_LRU + 2 * GLA_H * GLA_DV + 2 * GLA_H * GLA_DK
GK_PAD = LANES
LRU_T = 256
GLA_T = 512
MERGE_TM = 512
XA_TM = 512

COL_GATE = 0
COL_XLRU = (N_BRANCH * D_MODEL) // W_LRU
COL_GLRU = COL_XLRU + 1
COL_Q = (N_BRANCH * D_MODEL + 2 * W_LRU) // (GLA_H * GLA_DK)
COL_K = COL_Q + 1
COL_V = (N_BRANCH * D_MODEL + 2 * W_LRU + 2 * GLA_H * GLA_DK) // (GLA_H * GLA_DV)
COL_R = COL_V + 1


def _params(semantics, vmem_bytes):
    return pltpu.CompilerParams(dimension_semantics=semantics,
                                vmem_limit_bytes=int(min(VMEM_LIMIT_CAP, vmem_bytes)))


def _layer_resident(tail, layer):
    zeros = (0,) * len(tail)
    return pl.BlockSpec((None,) + tuple(tail), lambda *_: (layer,) + zeros,
                        pipeline_mode=pl.Buffered(1))


def _resident(shape):
    return pl.BlockSpec(shape, lambda *_: (0,) * len(shape), pipeline_mode=pl.Buffered(1))


def _rms(x, g):
    ms = jnp.mean(x * x, axis=-1, keepdims=True)
    return x * lax.rsqrt(ms + EPS) * g


def _silu(x):
    return x * jax.nn.sigmoid(x)


def _ffn_window_start(j, base=0):
    return LANES * (base // LANES + jnp.minimum(j * (FFN_TF // LANES), (D_FF - FFN_TF) // LANES))


def _ffn_kernel(xnext_ref, xprev_ref, pre_g_ref, wg_ref, wu_ref, wd_ref, post_g_ref, *rest,
                n_tiles, n_cast):
    n_in = 1 + n_cast if n_cast else 0
    cast_in, o_ref = rest[:n_in], rest[n_in]
    cast_out = rest[n_in + 1:len(rest) - 4]
    xn_even, xn_odd, acc_even, acc_odd = rest[len(rest) - 4:]
    r = pl.program_id(0)
    j = pl.program_id(1)
    nf = pl.num_programs(1)
    slab = xnext_ref.shape[0]
    n_slabs = xn_even.shape[0] // slab
    row0 = pl.multiple_of(jnp.minimum(j, n_slabs - 1) * slab, slab)
    group = 2 * SUBLANES

    def pre_norm(xn_dst):
        for g0 in range(0, slab, group):
            y = _rms(xnext_ref[g0:g0 + group, :], pre_g_ref[...])
            xn_dst[pl.ds(row0 + g0, group), :] = y.astype(BF16)

    def matmul_step(xn_src, acc):
        xn = xn_src[...]
        gate = jnp.dot(xn, wg_ref[...], preferred_element_type=F32)
        up = jnp.dot(xn, wu_ref[...], preferred_element_type=F32)
        act = _silu(gate) * up
        covered = jnp.where(j == nf - 1, nf * FFN_TF - D_FF, 0)
        col = lax.broadcasted_iota(jnp.int32, act.shape, 1)
        act = jnp.where(col >= covered, act, 0.0).astype(BF16)
        prev = jnp.where(j == 0, 0.0, acc[...])
        acc[...] = prev + jnp.dot(act, wd_ref[...], preferred_element_type=F32)
        if cast_in:
            _cast_w_in_block(cast_in[0], cast_out[0], cast_out[1])
            for src, dst in zip(cast_in[1:], cast_out[2:]):
                dst[...] = src[...].astype(BF16)

    def post_norm(acc_src):
        for g0 in range(0, slab, group):
            h = acc_src[pl.ds(row0 + g0, group), :]
            o_ref[g0:g0 + group, :] = (xprev_ref[g0:g0 + group, :]
                                       + FFN_RES_SCALE * _rms(h, post_g_ref[...]))

    @pl.when(r == 0)
    def _():
        @pl.when(j == 0)
        def _():
            acc_even[...] = jnp.zeros_like(acc_even)
            acc_odd[...] = jnp.zeros_like(acc_odd)

        pre_norm(xn_even)

    steady = (r >= 1) & (r <= n_tiles)

    @pl.when(steady & (r % 2 == 1))
    def _():
        pre_norm(xn_odd)
        post_norm(acc_odd)
        matmul_step(xn_even, acc_even)

    @pl.when(steady & (r % 2 == 0))
    def _():
        pre_norm(xn_even)
        post_norm(acc_even)
        matmul_step(xn_odd, acc_odd)

    @pl.when(r == n_tiles + 1)
    def _():
        post_norm(acc_odd if (n_tiles - 1) % 2 else acc_even)


def _ffn(x2, pre_g, w_up, w_down, post_g, layer, cast=None):
    m, d = x2.shape
    tm, tf = min(FFN_TM, m), FFN_TF
    n_tiles = m // tm
    slab = tm // FFN_SLABS
    vmem = (2 * tm * d * (2 + 4)
            + 2 * 3 * d * tf * 2
            + 3 * 2 * slab * d * 4
            + 2 * tm * d * 4 + 4 * tm * tf * 4)

    def next_slab(r, j):
        return (jnp.minimum(r, n_tiles - 1) * FFN_SLABS + jnp.minimum(j, FFN_SLABS - 1), 0)

    def prev_slab(r, j):
        tile = jnp.minimum(r - 2, n_tiles - 1)
        return (jnp.where(r < 2, 0, tile * FFN_SLABS + jnp.minimum(j, FFN_SLABS - 1)), 0)

    def window(r, j):
        return jnp.where(r == 0, 0, jnp.where(r == n_tiles + 1, FFN_NF - 1, j))

    out_specs = pl.BlockSpec((slab, d), prev_slab)
    out_shape = jax.ShapeDtypeStruct((m, d), F32)
    args = [x2, x2, pre_g, w_up, w_up, w_down, post_g]
    cast_in = []
    if cast is not None:
        cast_in, cast_out, cast_shape, cast_vmem = _cast_rows_payload(
            cast[0], cast[1], layer, n_tiles * FFN_NF,
            lambda r, j: jnp.clip((r - 1) * FFN_NF + j, 0, n_tiles * FFN_NF - 1))
        out_specs, out_shape = [out_specs] + cast_out, [out_shape] + cast_shape
        args += [cast[0]] + list(cast[1])
        vmem += cast_vmem
    return pl.pallas_call(
        functools.partial(_ffn_kernel, n_tiles=n_tiles,
                          n_cast=len(cast[1]) if cast is not None else 0),
        grid=(n_tiles + 2, FFN_NF),
        in_specs=[
            pl.BlockSpec((slab, d), next_slab),
            pl.BlockSpec((slab, d), prev_slab),
            _layer_resident((1, d), layer),
            pl.BlockSpec((pl.Element(d), pl.Element(tf)),
                         lambda r, j: (0, _ffn_window_start(window(r, j)))),
            pl.BlockSpec((pl.Element(d), pl.Element(tf)),
                         lambda r, j: (0, _ffn_window_start(window(r, j), base=D_FF))),
            pl.BlockSpec((pl.Element(tf), pl.Element(d)),
                         lambda r, j: (_ffn_window_start(window(r, j)), 0)),
            _layer_resident((1, d), layer),
        ] + cast_in,
        out_specs=out_specs,
        out_shape=out_shape,
        scratch_shapes=[pltpu.VMEM((tm, d), BF16), pltpu.VMEM((tm, d), BF16),
                        pltpu.VMEM((tm, d), F32), pltpu.VMEM((tm, d), F32)],
        compiler_params=_params(("arbitrary", "arbitrary"), vmem),
        name="ffn",
    )(*args)


CAST_BLOCKS = D_FF // LANES
CAST_UP_COLS = 2 * D_FF // CAST_BLOCKS
CAST_DOWN_ROWS = D_FF // CAST_BLOCKS


def _cast_payload(w_up, w_down, layer, step_of):
    d = w_up.shape[1]

    def blk(*idx):
        return jnp.minimum(step_of(*idx), CAST_BLOCKS - 1)

    in_specs = [pl.BlockSpec((None, d, CAST_UP_COLS), lambda *idx: (layer, 0, blk(*idx))),
                pl.BlockSpec((None, CAST_DOWN_ROWS, d), lambda *idx: (layer, blk(*idx), 0))]
    out_specs = [pl.BlockSpec((d, CAST_UP_COLS), lambda *idx: (0, blk(*idx))),
                 pl.BlockSpec((CAST_DOWN_ROWS, d), lambda *idx: (blk(*idx), 0))]
    out_shape = [jax.ShapeDtypeStruct(w_up.shape[1:], BF16),
                 jax.ShapeDtypeStruct(w_down.shape[1:], BF16)]
    vmem = 2 * (d * CAST_UP_COLS + CAST_DOWN_ROWS * d) * (4 + 2)
    return in_specs, out_specs, out_shape, vmem


def _cast_blocks(wu_in, wd_in, wu_out, wd_out):
    wu_out[...] = wu_in[...].astype(BF16)
    wd_out[...] = wd_in[...].astype(BF16)


CAST_ROWS = 2 * SUBLANES
N_HEAD = 2 * W_LRU + 2 * GLA_H * GLA_DK + 2 * GLA_H * GLA_DV


def _cast_rows_payload(w_in, others, layer, n_steps, step_of):
    in_specs, out_specs, out_shape, vmem = [], [], [], 0
    for k, a in enumerate([w_in] + list(others)):
        rows, cols = a.shape[1], a.shape[2]
        n_blocks = rows // CAST_ROWS
        assert rows % CAST_ROWS == 0 and n_blocks <= n_steps

        def idx(*g, n_blocks=n_blocks):
            return jnp.minimum(step_of(*g), n_blocks - 1)

        in_specs.append(pl.BlockSpec((None, CAST_ROWS, cols),
                                     lambda *g, idx=idx: (layer, idx(*g), 0)))
        out_cols = [N_PROJ, GK_PAD] if k == 0 else [cols]
        for c in out_cols:
            out_specs.append(pl.BlockSpec((CAST_ROWS, c), lambda *g, idx=idx: (idx(*g), 0)))
            out_shape.append(jax.ShapeDtypeStruct((rows, c), BF16))
        vmem += 2 * CAST_ROWS * cols * (4 + 2)
    return in_specs, out_specs, out_shape, vmem


def _cast_w_in_block(w_ref, main_ref, gk_ref):
    w = w_ref[...]
    n_gate = N_BRANCH * D_MODEL
    main_ref[:, :n_gate] = w[:, N_HEAD + GLA_RANK:].astype(BF16)
    main_ref[:, n_gate:] = w[:, :N_HEAD].astype(BF16)
    gk_ref[...] = w[:, N_HEAD:N_HEAD + GK_PAD].astype(BF16)


def _norm_matmul_kernel(x_ref, g_ref, w_ref, o_ref, xn_ref):
    @pl.when(pl.program_id(1) == 0)
    def _():
        xn_ref[...] = _rms(x_ref[...], g_ref[...]).astype(BF16)

    o_ref[...] = jnp.dot(xn_ref[...], w_ref[...], preferred_element_type=F32).astype(o_ref.dtype)


def _norm_matmul_side_kernel(x_ref, g_ref, w_ref, ws_ref, o_ref, side_ref, xn_ref):
    @pl.when(pl.program_id(1) == 0)
    def _():
        xn = _rms(x_ref[...], g_ref[...]).astype(BF16)
        xn_ref[...] = xn
        side_ref[...] = jnp.dot(xn, ws_ref[...], preferred_element_type=F32)

    o_ref[...] = jnp.dot(xn_ref[...], w_ref[...], preferred_element_type=F32).astype(o_ref.dtype)


def _norm_matmul(x2, g, w, layer, w_side=None, name="norm_matmul"):
    m, d = x2.shape
    n = w.shape[1]
    tm, tn = min(PROJ_TM, m), min(PROJ_TN, n)
    vmem = (2 * tm * d * 4 + tm * d * 2 + 2 * d * tn * 2 + 2 * tm * tn * 2
            + 2 * tm * tn * 4 + tm * d * 4)
    in_specs = [
        pl.BlockSpec((tm, d), lambda i, j: (i, 0)),
        _layer_resident((1, d), layer),
        pl.BlockSpec((d, tn), lambda i, j: (0, j)),
    ]
    out_specs = pl.BlockSpec((tm, tn), lambda i, j: (i, j))
    out_shape = jax.ShapeDtypeStruct((m, n), BF16)
    args = [x2, g, w]
    kern = _norm_matmul_kernel
    if w_side is not None:
        ns = w_side.shape[1]
        in_specs.append(_resident((d, ns)))
        out_specs = [out_specs, pl.BlockSpec((tm, ns), lambda i, j: (i, 0))]
        out_shape = [out_shape, jax.ShapeDtypeStruct((m, ns), F32)]
        args.append(w_side)
        kern = _norm_matmul_side_kernel
        vmem += d * ns * 2 + 2 * tm * ns * 4
    return pl.pallas_call(
        kern,
        grid=(m // tm, n // tn),
        in_specs=in_specs,
        out_specs=out_specs,
        out_shape=out_shape,
        scratch_shapes=[pltpu.VMEM((tm, d), BF16)],
        compiler_params=_params(("arbitrary", "arbitrary"), vmem),
        name=name,
    )(*args)


def _gelu_tanh(x):
    c = math.sqrt(2.0 / math.pi)
    return 0.5 * x * (1.0 + jnp.tanh(c * (x + 0.044715 * (x * x * x))))


def _softplus(x):
    return jnp.maximum(x, 0.0) + jnp.log1p(jnp.exp(-jnp.abs(x)))


def _lru_kernel(xl_ref, gl_ref, cw_ref, cb_ref, wai_ref, ba_ref, bi_ref, lam_ref, *rest, cast):
    if cast:
        wu_in, wd_in, y_ref, wu_out, wd_out, tail_ref, h_ref = rest
        _cast_blocks(wu_in, wd_in, wu_out, wd_out)
    else:
        y_ref, tail_ref, h_ref = rest
    t_rows = xl_ref.shape[0]
    nblk = t_rows // SUBLANES

    @pl.when(pl.program_id(1) == 0)
    def _():
        tail_ref[...] = jnp.zeros_like(tail_ref)
        h_ref[...] = jnp.zeros_like(h_ref)

    row8 = lax.broadcasted_iota(jnp.int32, (SUBLANES, LRU_BW), 0)
    sub3 = lax.broadcasted_iota(jnp.int32, (nblk, SUBLANES, LRU_BW), 1)

    for n in range(LRU_BLOCKS):
        cs = slice(n * LRU_BW, (n + 1) * LRU_BW)
        x = xl_ref[:, cs].astype(F32)
        prev8 = tail_ref[:, cs]
        xc = x * cw_ref[CONV_W - 1:CONV_W, cs] + cb_ref[:, cs]
        for s in range(1, CONV_W):
            xs = pltpu.roll(x, s, 0)
            ps = pltpu.roll(prev8, s, 0)
            head = jnp.where(row8 < s, ps, xs[:SUBLANES])
            xs = jnp.concatenate([head, xs[SUBLANES:]], axis=0)
            xc = xc + xs * cw_ref[CONV_W - 1 - s:CONV_W - s, cs]
        tail_ref[:, cs] = x[t_rows - SUBLANES:]

        pre = jnp.dot(xc.astype(BF16), wai_ref[n], preferred_element_type=F32)
        r = jax.nn.sigmoid(pre[:, :LRU_BW] + ba_ref[:, cs])
        i = jax.nn.sigmoid(pre[:, LRU_BW:] + bi_ref[:, cs])
        log_a = (-LRU_C * _softplus(-lam_ref[:, cs])) * r
        a = jnp.exp(log_a)
        u = jnp.sqrt(-jnp.tanh(log_a) * (1.0 + a * a)) * (i * xc)

        a3 = a.reshape(nblk, SUBLANES, LRU_BW)
        u3 = u.reshape(nblk, SUBLANES, LRU_BW)
        for dd in (1, 2, 4):
            keep = sub3 >= dd
            a_s = jnp.where(keep, pltpu.roll(a3, dd, 1), 1.0)
            u_s = jnp.where(keep, pltpu.roll(u3, dd, 1), 0.0)
            u3 = a3 * u_s + u3
            a3 = a3 * a_s
        carry = jnp.broadcast_to(h_ref[:, cs], (SUBLANES, LRU_BW))
        hs = []
        for b in range(nblk):
            hb = u3[b] + a3[b] * carry
            hs.append(hb)
            carry = jnp.broadcast_to(hb[SUBLANES - 1:SUBLANES], (SUBLANES, LRU_BW))
        h = jnp.concatenate(hs, axis=0)
        h_ref[:, cs] = carry[0:1]

        y = h * _gelu_tanh(gl_ref[:, cs].astype(F32))
        y_ref[:, cs] = y.astype(y_ref.dtype)


def _lru(proj, conv_w, conv_b, w_ai, b_a, b_i, lam, layer, batch, seq, cast=None):
    t = min(LRU_T, seq)
    nt = seq // t
    w = W_LRU
    vmem = 2 * 3 * t * w * 2 + LRU_BLOCKS * LRU_BW * 2 * LRU_BW * 2 + 64 * t * LRU_BW * 4
    out_specs = pl.BlockSpec((t, w), lambda b, s: (b * nt + s, 0))
    out_shape = jax.ShapeDtypeStruct((batch * seq, w), BF16)
    args = [proj, proj, conv_w, conv_b, w_ai, b_a, b_i, lam]
    cast_in = []
    if cast is not None:
        assert batch * nt >= CAST_BLOCKS
        cast_in, cast_out, cast_shape, cast_vmem = _cast_payload(
            cast[0], cast[1], cast[2], lambda b, s: b * nt + s)
        out_specs, out_shape = [out_specs] + cast_out, [out_shape] + cast_shape
        args += [cast[0], cast[1]]
        vmem += cast_vmem
    return pl.pallas_call(
        functools.partial(_lru_kernel, cast=cast is not None),
        grid=(batch, nt),
        in_specs=[
            pl.BlockSpec((t, w), lambda b, s: (b * nt + s, COL_XLRU)),
            pl.BlockSpec((t, w), lambda b, s: (b * nt + s, COL_GLRU)),
            _layer_resident((CONV_W, w), layer),
            _layer_resident((1, w), layer),
            _layer_resident((LRU_BLOCKS, LRU_BW, 2 * LRU_BW), layer),
            _layer_resident((1, w), layer),
            _layer_resident((1, w), layer),
            _layer_resident((1, w), layer),
        ] + cast_in,
        out_specs=out_specs,
        out_shape=out_shape,
        scratch_shapes=[pltpu.VMEM((SUBLANES, w), F32), pltpu.VMEM((1, w), F32)],
        compiler_params=_params(("arbitrary", "arbitrary"), vmem + (8 << 20)),
        name="rglru",
    )(*args)


def _log_sigmoid(x):
    return jnp.minimum(x, 0.0) - jnp.log1p(jnp.exp(-jnp.abs(x)))


def _gla_kernel(q_ref, k_ref, v_ref, r_ref, gk_ref, wgk2_ref, bgk_ref, ng_ref, *rest, cast):
    if cast:
        wu_in, wd_in, y_ref, wu_out, wd_out, st_ref = rest
        _cast_blocks(wu_in, wd_in, wu_out, wd_out)
    else:
        y_ref, st_ref = rest
    t_rows = q_ref.shape[0]
    c = GLA_CHUNK
    hk = GLA_H * GLA_DK

    @pl.when(pl.program_id(1) == 0)
    def _():
        st_ref[...] = jnp.zeros_like(st_ref)

    z = jnp.dot(gk_ref[...], wgk2_ref[...], preferred_element_type=F32,
                precision=lax.Precision.HIGHEST) + bgk_ref[...]
    la = _log_sigmoid(z) * (1.0 / GLA_NORMALIZER)
    pos = lax.broadcasted_iota(jnp.int32, (t_rows, hk), 0) & (c - 1)
    dd = 1
    while dd < c:
        la = la + jnp.where(pos >= dd, pltpu.roll(la, dd, 0), 0.0)
        dd *= 2
    bcum = la

    tril = (lax.broadcasted_iota(jnp.int32, (c, c), 0)
            >= lax.broadcasted_iota(jnp.int32, (c, c), 1))
    nt_dims = (((1,), (1,)), ((), ()))
    tn_dims = (((0,), (0,)), ((), ()))
    n_chunks = t_rows // c

    qe_c, g_c, o_intra, upd = [], [], [], []
    for ci in range(n_chunks):
        rows = slice(ci * c, (ci + 1) * c)
        bc = bcum[rows]
        b_last = bcum[(ci + 1) * c - 1:(ci + 1) * c]
        q = q_ref[rows, :].astype(F32) * (GLA_DK ** -0.5)
        k = k_ref[rows, :].astype(F32)
        qe = (q * jnp.exp(bc)).astype(BF16)
        ke = (k * jnp.exp(-bc)).astype(BF16)
        kd = (k * jnp.exp(b_last - bc)).astype(BF16)
        qe_c.append(qe)
        g_c.append(jnp.exp(b_last))
        o_h, upd_h = [], []
        for h in range(GLA_H):
            ks = slice(h * GLA_DK, (h + 1) * GLA_DK)
            v_h = v_ref[rows, h * GLA_DV:(h + 1) * GLA_DV]
            s = lax.dot_general(qe[:, ks], ke[:, ks], nt_dims, preferred_element_type=F32)
            s = jnp.where(tril, s, 0.0).astype(BF16)
            o_h.append(jnp.dot(s, v_h, preferred_element_type=F32))
            upd_h.append(lax.dot_general(v_h, kd[:, ks], tn_dims,
                                         preferred_element_type=F32))
        o_intra.append(o_h)
        upd.append(upd_h)

    for h in range(GLA_H):
        ks = slice(h * GLA_DK, (h + 1) * GLA_DK)
        vs = slice(h * GLA_DV, (h + 1) * GLA_DV)
        st = st_ref[h]
        for ci in range(n_chunks):
            rows = slice(ci * c, (ci + 1) * c)
            o = o_intra[ci][h] + lax.dot_general(qe_c[ci][:, ks], st.astype(BF16), nt_dims,
                                                 preferred_element_type=F32)
            st = st * g_c[ci][:, ks] + upd[ci][h]
            o = o * lax.rsqrt(jnp.mean(o * o, axis=-1, keepdims=True) + EPS) * ng_ref[...]
            o = o * _silu(r_ref[rows, vs].astype(F32))
            y_ref[rows, vs] = o.astype(y_ref.dtype)
        st_ref[h] = st


def _gla(proj, gk, w_gk2_p, b_gk, norm_g, layer, batch, seq, cast=None):
    t = min(GLA_T, seq)
    while cast is not None and batch * (seq // t) < CAST_BLOCKS:
        t //= 2
    nt = seq // t
    hk, hv = GLA_H * GLA_DK, GLA_H * GLA_DV
    vmem = (2 * (2 * t * hk * 2 + 3 * t * hv * 2 + t * GK_PAD * 4) + t * hk * 4
            + GLA_H * GLA_DV * GLA_DK * 4 + 8 * t * hk * 4)
    out_specs = pl.BlockSpec((t, hv), lambda b, s: (b * nt + s, 0))
    out_shape = jax.ShapeDtypeStruct((batch * seq, hv), BF16)
    args = [proj, proj, proj, proj, gk, w_gk2_p, b_gk, norm_g]
    cast_in = []
    if cast is not None:
        cast_in, cast_out, cast_shape, cast_vmem = _cast_payload(
            cast[0], cast[1], cast[2], lambda b, s: b * nt + s)
        out_specs, out_shape = [out_specs] + cast_out, [out_shape] + cast_shape
        args += [cast[0], cast[1]]
        vmem += cast_vmem
    return pl.pallas_call(
        functools.partial(_gla_kernel, cast=cast is not None),
        grid=(batch, nt),
        in_specs=[
            pl.BlockSpec((t, hk), lambda b, s: (b * nt + s, COL_Q)),
            pl.BlockSpec((t, hk), lambda b, s: (b * nt + s, COL_K)),
            pl.BlockSpec((t, hv), lambda b, s: (b * nt + s, COL_V)),
            pl.BlockSpec((t, hv), lambda b, s: (b * nt + s, COL_R)),
            pl.BlockSpec((t, GK_PAD), lambda b, s: (b * nt + s, 0)),
            _layer_resident((GK_PAD, hk), layer),
            _layer_resident((1, hk), layer),
            _layer_resident((1, GLA_DV), layer),
        ] + cast_in,
        out_specs=out_specs,
        out_shape=out_shape,
        scratch_shapes=[pltpu.VMEM((GLA_H, GLA_DV, GLA_DK), F32)],
        compiler_params=_params(("arbitrary", "arbitrary"), vmem + (8 << 20)),
        name="gla",
    )(*args)


def _merge_kernel(x_ref, ya_ref, yb_ref, gl_ref, bg_ref, wb_ref, wo_ref, post_g_ref, o_ref):
    d = x_ref.shape[1]
    z_a = jnp.dot(ya_ref[...], wb_ref[:W_LRU, :], preferred_element_type=F32)
    z_b = jnp.dot(yb_ref[...], wb_ref[W_LRU:, :], preferred_element_type=F32)
    g_a = jax.nn.sigmoid(gl_ref[:, :d].astype(F32) + bg_ref[:, :d])
    g_b = jax.nn.sigmoid(gl_ref[:, d:].astype(F32) + bg_ref[:, d:])
    merged = (g_a * z_a + g_b * z_b).astype(BF16)
    h = jnp.dot(merged, wo_ref[...], preferred_element_type=F32)
    o_ref[...] = x_ref[...] + _rms(h, post_g_ref[...])


def _merge(x2, y_a, y_b, proj, b_gate, w_branch, w_out, post_g, layer):
    m, d = x2.shape
    tm = min(MERGE_TM, m)
    wb_rows = w_branch.shape[0]
    vmem = (2 * 2 * tm * d * 4 + 2 * 2 * tm * W_LRU * 2 + 2 * tm * 2 * d * 2
            + wb_rows * d * 2 + d * d * 2 + 6 * tm * d * 4)
    return pl.pallas_call(
        _merge_kernel,
        grid=(m // tm,),
        in_specs=[
            pl.BlockSpec((tm, d), lambda i: (i, 0)),
            pl.BlockSpec((tm, W_LRU), lambda i: (i, 0)),
            pl.BlockSpec((tm, GLA_H * GLA_DV), lambda i: (i, 0)),
            pl.BlockSpec((tm, N_BRANCH * d), lambda i: (i, COL_GATE)),
            _layer_resident((1, N_BRANCH * d), layer),
            _resident((wb_rows, d)),
            _resident((d, d)),
            _layer_resident((1, d), layer),
        ],
        out_specs=pl.BlockSpec((tm, d), lambda i: (i, 0)),
        out_shape=jax.ShapeDtypeStruct((m, d), F32),
        compiler_params=_params(("arbitrary",), vmem),
        name="merge",
    )(x2, y_a, y_b, proj, b_gate, w_branch, w_out, post_g)


def _xattn_kernel(x_ref, pre_g_ref, wq_ref, kk_ref, vv_ref, wo_ref, post_g_ref, o_ref):
    xn = _rms(x_ref[...], pre_g_ref[...]).astype(BF16)
    q = (jnp.dot(xn, wq_ref[...], preferred_element_type=F32) * (XA_DH ** -0.5)).astype(BF16)
    nt_dims = (((1,), (1,)), ((), ()))
    outs = []
    for h in range(XA_H):
        hs = slice(h * XA_DH, (h + 1) * XA_DH)
        s = lax.dot_general(q[:, hs], kk_ref[:, hs], nt_dims,
                            preferred_element_type=F32)
        p = jnp.exp(s - jnp.max(s, axis=-1, keepdims=True))
        pv = jnp.dot(p.astype(BF16), vv_ref[:, hs], preferred_element_type=F32)
        outs.append(pv / jnp.sum(p, axis=-1, keepdims=True))
    o = jnp.concatenate(outs, axis=1).astype(BF16)
    h_out = jnp.dot(o, wo_ref[...], preferred_element_type=F32)
    o_ref[...] = x_ref[...] + _rms(h_out, post_g_ref[...])


def _xattn(x2, pre_g, w_q, kv, w_o, post_g, layer, batch, seq, mem_len):
    m, d = x2.shape
    tm = min(XA_TM, seq)
    nt = seq // tm
    hd = XA_H * XA_DH
    vmem = (2 * 2 * tm * d * 4 + 2 * d * hd * 2 + 2 * 2 * mem_len * hd * 2
            + 4 * tm * d * 4 + 8 * tm * mem_len * 4)
    return pl.pallas_call(
        _xattn_kernel,
        grid=(batch, nt),
        in_specs=[
            pl.BlockSpec((tm, d), lambda b, s: (b * nt + s, 0)),
            _layer_resident((1, d), layer),
            _resident((d, hd)),
            pl.BlockSpec((mem_len, hd), lambda b, s: (b, 0)),
            pl.BlockSpec((mem_len, hd), lambda b, s: (b, 1)),
            _resident((hd, d)),
            _layer_resident((1, d), layer),
        ],
        out_specs=pl.BlockSpec((tm, d), lambda b, s: (b * nt + s, 0)),
        out_shape=jax.ShapeDtypeStruct((m, d), F32),
        compiler_params=_params(("arbitrary", "arbitrary"), vmem),
        name="xattn",
    )(x2, pre_g, w_q, kv, kv, w_o, post_g)


def _rows(v):
    return v.reshape(v.shape[0], 1, -1)


def kernel(x, mem, ffn1_pre_g, ffn1_post_g, ffn1_w_up, ffn1_w_down, mix_pre_g, mix_post_g, w_in,
           conv_w, conv_b, lru_w_a, lru_b_a, lru_w_i, lru_b_i, lru_lambda, gla_w_gk2, gla_b_gk,
           gla_norm_g, b_gate, w_branch, w_out, xa_pre_g, xa_post_g, mem_g, xa_w_q, xa_w_kv,
           xa_w_o, ffn2_pre_g, ffn2_post_g, ffn2_w_up, ffn2_w_down):
    batch, seq, d = x.shape
    mem_len = mem.shape[1]
    depth = ffn1_w_up.shape[0]
    x2 = x.reshape(batch * seq, d)
    mem2 = mem.reshape(batch * mem_len, d)

    ffn1_up, ffn1_down = ffn1_w_up[0].astype(BF16), ffn1_w_down[0].astype(BF16)
    w_ai = jnp.concatenate([lru_w_a, lru_w_i], axis=-1).astype(BF16)
    w_gk2_p = jnp.pad(gla_w_gk2, ((0, 0), (0, GK_PAD - GLA_RANK), (0, 0)))
    mixer_weights = [w_branch, w_out, xa_w_q, xa_w_kv, xa_w_o]
    ffn1_pre, ffn1_post = _rows(ffn1_pre_g), _rows(ffn1_post_g)
    ffn2_pre, ffn2_post = _rows(ffn2_pre_g), _rows(ffn2_post_g)
    mix_pre, mix_post = _rows(mix_pre_g), _rows(mix_post_g)
    xa_pre, xa_post, mem_gain = _rows(xa_pre_g), _rows(xa_post_g), _rows(mem_g)
    conv_bias, b_a, b_i, lam = _rows(conv_b), _rows(lru_b_a), _rows(lru_b_i), _rows(lru_lambda)
    b_gk, norm_g, b_gate_r = _rows(gla_b_gk), _rows(gla_norm_g), _rows(b_gate)

    for l in range(depth):
        x2, w_main, w_gk, wb, wo, xa_q, xa_kv, xa_o = _ffn(
            x2, ffn1_pre, ffn1_up, ffn1_down, ffn1_post, l, cast=(w_in, mixer_weights))

        proj, gk = _norm_matmul(x2, mix_pre, w_main, l, w_side=w_gk, name="mix_in_proj")
        y_a, ffn2_up, ffn2_down = _lru(proj, conv_w, conv_bias, w_ai, b_a, b_i, lam, l, batch, seq,
                                       cast=(ffn2_w_up, ffn2_w_down, l))
        if l + 1 < depth:
            y_b, ffn1_up, ffn1_down = _gla(proj, gk, w_gk2_p, b_gk, norm_g, l, batch, seq,
                                           cast=(ffn1_w_up, ffn1_w_down, l + 1))
        else:
            y_b = _gla(proj, gk, w_gk2_p, b_gk, norm_g, l, batch, seq)
        x2 = _merge(x2, y_a, y_b, proj, b_gate_r, wb, wo, mix_post, l)

        kv = _norm_matmul(mem2, mem_gain, xa_kv, l, name="mem_kv_proj")
        x2 = _xattn(x2, xa_pre, xa_q, kv, xa_o, xa_post, l, batch, seq, mem_len)

        x2 = _ffn(x2, ffn2_pre, ffn2_up, ffn2_down, ffn2_post, l)

    return x2.reshape(batch, seq, d)
```

```python
import functools
import math

import jax
import jax.numpy as jnp
from jax import lax
from jax.experimental import pallas as pl
from jax.experimental.pallas import tpu as pltpu

F32 = jnp.float32
BF16 = jnp.bfloat16

D_MODEL = 2048
D_FF = 5504
FFN_RES_SCALE = 0.5
W_LRU = D_MODEL // 2
LRU_BLOCKS = 8
LRU_BW = W_LRU // LRU_BLOCKS
CONV_W = 4
LRU_C = 8.0
GLA_H = 4
GLA_DK = 128
GLA_DV = 256
GLA_RANK = 16
GLA_NORMALIZER = 16.0
GLA_CHUNK = 64
XA_H = 4
XA_DH = 128
N_BRANCH = 2
EPS = 1e-6

LANES = 128
SUBLANES = 8
V7X_VMEM_BYTES = 64 * 1024 * 1024
VMEM_LIMIT_CAP = V7X_VMEM_BYTES - 6 * 1024 * 1024

FFN_TF = 512
FFN_NF = -(-D_FF // FFN_TF)
FFN_TM = 1024
FFN_SLABS = 8
PROJ_TM = 1024
PROJ_TN = 2304
N_PROJ = N_BRANCH * D_MODEL + 2 * W_LRU + 2 * GLA_H * GLA_DV + 2 * GLA_H * GLA_DK
GK_PAD = LANES
LRU_T = 256
GLA_T = 512
MERGE_TM = 512
XA_TM = 512

COL_GATE = 0
COL_XLRU = (N_BRANCH * D_MODEL) // W_LRU
COL_GLRU = COL_XLRU + 1
COL_Q = (N_BRANCH * D_MODEL + 2 * W_LRU) // (GLA_H * GLA_DK)
COL_K = COL_Q + 1
COL_V = (N_BRANCH * D_MODEL + 2 * W_LRU + 2 * GLA_H * GLA_DK) // (GLA_H * GLA_DV)
COL_R = COL_V + 1


def _params(semantics, vmem_bytes):
    return pltpu.CompilerParams(dimension_semantics=semantics,
                                vmem_limit_bytes=int(min(VMEM_LIMIT_CAP, vmem_bytes)))


def _layer_resident(tail, layer):
    zeros = (0,) * len(tail)
    return pl.BlockSpec((None,) + tuple(tail), lambda *_: (layer,) + zeros,
                        pipeline_mode=pl.Buffered(1))


def _rms(x, g):
    ms = jnp.mean(x * x, axis=-1, keepdims=True)
    return x * lax.rsqrt(ms + EPS) * g


def _silu(x):
    return x * jax.nn.sigmoid(x)


def _ffn_window_start(j, base=0):
    return LANES * (base // LANES + jnp.minimum(j * (FFN_TF // LANES), (D_FF - FFN_TF) // LANES))


def _ffn_kernel(xnext_ref, xprev_ref, pre_g_ref, wg_ref, wu_ref, wd_ref, post_g_ref, o_ref,
                xn_even, xn_odd, acc_even, acc_odd, *, n_tiles):
    r = pl.program_id(0)
    j = pl.program_id(1)
    nf = pl.num_programs(1)
    slab = xnext_ref.shape[0]
    n_slabs = xn_even.shape[0] // slab
    row0 = pl.multiple_of(jnp.minimum(j, n_slabs - 1) * slab, slab)
    group = 2 * SUBLANES

    def pre_norm(xn_dst):
        for g0 in range(0, slab, group):
            y = _rms(xnext_ref[g0:g0 + group, :], pre_g_ref[...])
            xn_dst[pl.ds(row0 + g0, group), :] = y.astype(BF16)

    def matmul_step(xn_src, acc):
        xn = xn_src[...]
        gate = jnp.dot(xn, wg_ref[...], preferred_element_type=F32)
        up = jnp.dot(xn, wu_ref[...], preferred_element_type=F32)
        act = _silu(gate) * up
        covered = jnp.where(j == nf - 1, nf * FFN_TF - D_FF, 0)
        col = lax.broadcasted_iota(jnp.int32, act.shape, 1)
        act = jnp.where(col >= covered, act, 0.0).astype(BF16)
        prev = jnp.where(j == 0, 0.0, acc[...])
        acc[...] = prev + jnp.dot(act, wd_ref[...], preferred_element_type=F32)

    def post_norm(acc_src):
        for g0 in range(0, slab, group):
            h = acc_src[pl.ds(row0 + g0, group), :]
            o_ref[g0:g0 + group, :] = (xprev_ref[g0:g0 + group, :]
                                       + FFN_RES_SCALE * _rms(h, post_g_ref[...]))

    @pl.when(r == 0)
    def _():
        @pl.when(j == 0)
        def _():
            acc_even[...] = jnp.zeros_like(acc_even)
            acc_odd[...] = jnp.zeros_like(acc_odd)

        pre_norm(xn_even)

    steady = (r >= 1) & (r <= n_tiles)

    @pl.when(steady & (r % 2 == 1))
    def _():
        pre_norm(xn_odd)
        post_norm(acc_odd)
        matmul_step(xn_even, acc_even)

    @pl.when(steady & (r % 2 == 0))
    def _():
        pre_norm(xn_even)
        post_norm(acc_even)
        matmul_step(xn_odd, acc_odd)

    @pl.when(r == n_tiles + 1)
    def _():
        post_norm(acc_odd if (n_tiles - 1) % 2 else acc_even)


def _ffn(x2, pre_g, w_up, w_down, post_g, layer):
    m, d = x2.shape
    tm, tf = min(FFN_TM, m), FFN_TF
    n_tiles = m // tm
    slab = tm // FFN_SLABS
    vmem = (2 * tm * d * (2 + 4)
            + 2 * 3 * d * tf * 2
            + 3 * 2 * slab * d * 4
            + 2 * tm * d * 4 + 4 * tm * tf * 4)

    def next_slab(r, j):
        return (jnp.minimum(r, n_tiles - 1) * FFN_SLABS + jnp.minimum(j, FFN_SLABS - 1), 0)

    def prev_slab(r, j):
        tile = jnp.minimum(r - 2, n_tiles - 1)
        return (jnp.where(r < 2, 0, tile * FFN_SLABS + jnp.minimum(j, FFN_SLABS - 1)), 0)

    def window(r, j):
        return jnp.where(r == 0, 0, jnp.where(r == n_tiles + 1, FFN_NF - 1, j))

    return pl.pallas_call(
        functools.partial(_ffn_kernel, n_tiles=n_tiles),
        grid=(n_tiles + 2, FFN_NF),
        in_specs=[
            pl.BlockSpec((slab, d), next_slab),
            pl.BlockSpec((slab, d), prev_slab),
            _layer_resident((1, d), layer),
            pl.BlockSpec((pl.Element(d), pl.Element(tf)),
                         lambda r, j: (0, _ffn_window_start(window(r, j)))),
            pl.BlockSpec((pl.Element(d), pl.Element(tf)),
                         lambda r, j: (0, _ffn_window_start(window(r, j), base=D_FF))),
            pl.BlockSpec((pl.Element(tf), pl.Element(d)),
                         lambda r, j: (_ffn_window_start(window(r, j)), 0)),
            _layer_resident((1, d), layer),
        ],
        out_specs=pl.BlockSpec((slab, d), prev_slab),
        out_shape=jax.ShapeDtypeStruct((m, d), F32),
        scratch_shapes=[pltpu.VMEM((tm, d), BF16), pltpu.VMEM((tm, d), BF16),
                        pltpu.VMEM((tm, d), F32), pltpu.VMEM((tm, d), F32)],
        compiler_params=_params(("arbitrary", "arbitrary"), vmem),
        name="ffn",
    )(x2, x2, pre_g, w_up, w_up, w_down, post_g)


CAST_BLOCKS = D_FF // LANES
CAST_UP_COLS = 2 * D_FF // CAST_BLOCKS
CAST_DOWN_ROWS = D_FF // CAST_BLOCKS


def _cast_payload(w_up, w_down, layer, step_of):
    d = w_up.shape[1]

    def blk(*idx):
        return jnp.minimum(step_of(*idx), CAST_BLOCKS - 1)

    in_specs = [pl.BlockSpec((None, d, CAST_UP_COLS), lambda *idx: (layer, 0, blk(*idx))),
                pl.BlockSpec((None, CAST_DOWN_ROWS, d), lambda *idx: (layer, blk(*idx), 0))]
    out_specs = [pl.BlockSpec((d, CAST_UP_COLS), lambda *idx: (0, blk(*idx))),
                 pl.BlockSpec((CAST_DOWN_ROWS, d), lambda *idx: (blk(*idx), 0))]
    out_shape = [jax.ShapeDtypeStruct(w_up.shape[1:], BF16),
                 jax.ShapeDtypeStruct(w_down.shape[1:], BF16)]
    vmem = 2 * (d * CAST_UP_COLS + CAST_DOWN_ROWS * d) * (4 + 2)
    return in_specs, out_specs, out_shape, vmem


def _cast_blocks(wu_in, wd_in, wu_out, wd_out):
    wu_out[...] = wu_in[...].astype(BF16)
    wd_out[...] = wd_in[...].astype(BF16)


def _norm_matmul_kernel(x_ref, g_ref, w_ref, o_ref, xn_ref):
    @pl.when(pl.program_id(1) == 0)
    def _():
        xn_ref[...] = _rms(x_ref[...], g_ref[...]).astype(BF16)

    o_ref[...] = jnp.dot(xn_ref[...], w_ref[...], preferred_element_type=F32).astype(o_ref.dtype)


def _norm_matmul_side_kernel(x_ref, g_ref, w_ref, ws_ref, o_ref, side_ref, xn_ref):
    @pl.when(pl.program_id(1) == 0)
    def _():
        xn = _rms(x_ref[...], g_ref[...]).astype(BF16)
        xn_ref[...] = xn
        side_ref[...] = jnp.dot(xn, ws_ref[...], preferred_element_type=F32)

    o_ref[...] = jnp.dot(xn_ref[...], w_ref[...], preferred_element_type=F32).astype(o_ref.dtype)


def _norm_matmul(x2, g, w, layer, w_side=None, name="norm_matmul"):
    m, d = x2.shape
    n = w.shape[2]
    tm, tn = min(PROJ_TM, m), min(PROJ_TN, n)
    vmem = (2 * tm * d * 4 + tm * d * 2 + 2 * d * tn * 2 + 2 * tm * tn * 2
            + 2 * tm * tn * 4 + tm * d * 4)
    in_specs = [
        pl.BlockSpec((tm, d), lambda i, j: (i, 0)),
        _layer_resident((1, d), layer),
        pl.BlockSpec((None, d, tn), lambda i, j: (layer, 0, j)),
    ]
    out_specs = pl.BlockSpec((tm, tn), lambda i, j: (i, j))
    out_shape = jax.ShapeDtypeStruct((m, n), BF16)
    args = [x2, g, w]
    kern = _norm_matmul_kernel
    if w_side is not None:
        ns = w_side.shape[2]
        in_specs.append(_layer_resident((d, ns), layer))
        out_specs = [out_specs, pl.BlockSpec((tm, ns), lambda i, j: (i, 0))]
        out_shape = [out_shape, jax.ShapeDtypeStruct((m, ns), F32)]
        args.append(w_side)
        kern = _norm_matmul_side_kernel
        vmem += d * ns * 2 + 2 * tm * ns * 4
    return pl.pallas_call(
        kern,
        grid=(m // tm, n // tn),
        in_specs=in_specs,
        out_specs=out_specs,
        out_shape=out_shape,
        scratch_shapes=[pltpu.VMEM((tm, d), BF16)],
        compiler_params=_params(("arbitrary", "arbitrary"), vmem),
        name=name,
    )(*args)


def _gelu_tanh(x):
    c = math.sqrt(2.0 / math.pi)
    return 0.5 * x * (1.0 + jnp.tanh(c * (x + 0.044715 * (x * x * x))))


def _softplus(x):
    return jnp.maximum(x, 0.0) + jnp.log1p(jnp.exp(-jnp.abs(x)))


LRU_SEG = 32
LRU_PITCH = LRU_SEG + 4


def _lru_kernel(xl_ref, gl_ref, cw_ref, cb_ref, wai_ref, ba_ref, bi_ref, lam_ref, *rest, cast):
    if cast:
        wu_in, wd_in, y_ref, wu_out, wd_out = rest[:5]
        _cast_blocks(wu_in, wd_in, wu_out, wd_out)
        scratch = rest[5:]
    else:
        y_ref, scratch = rest[0], rest[1:]
    tail_ref, h_ref, sx_ref, sxc_ref, sr_ref, si_ref, sh_ref = scratch
    t_rows = xl_ref.shape[0]
    n_seg = t_rows // LRU_SEG
    assert n_seg == SUBLANES

    @pl.when(pl.program_id(1) == 0)
    def _():
        tail_ref[...] = jnp.zeros_like(tail_ref)
        h_ref[...] = jnp.zeros_like(h_ref)

    sub = lax.broadcasted_iota(jnp.int32, (SUBLANES, LRU_BW), 0)

    def to_segments(ref, v):
        for sg in range(n_seg):
            ref[sg * LRU_PITCH:sg * LRU_PITCH + LRU_SEG, :] = v[sg * LRU_SEG:(sg + 1) * LRU_SEG]
        return [ref[pl.ds(j, SUBLANES, stride=LRU_PITCH), :] for j in range(LRU_SEG)]

    def to_rows(ref, vs):
        for j, v in enumerate(vs):
            ref[pl.ds(j, SUBLANES, stride=LRU_PITCH), :] = v
        return jnp.concatenate([ref[sg * LRU_PITCH:sg * LRU_PITCH + LRU_SEG, :]
                                for sg in range(n_seg)], axis=0)

    def shift_in(v, first):
        return jnp.where(sub == 0, jnp.broadcast_to(first, v.shape), pltpu.roll(v, 1, 0))

    for n in range(LRU_BLOCKS):
        cs = slice(n * LRU_BW, (n + 1) * LRU_BW)
        x = xl_ref[:, cs].astype(F32)
        xm = to_segments(sx_ref, x)
        tail = tail_ref[:, cs]
        before = {k: shift_in(xm[LRU_SEG - k], tail[SUBLANES - k:SUBLANES - k + 1])
                  for k in range(1, CONV_W)}
        tail_ref[:, cs] = x[t_rows - SUBLANES:]
        xcm = []
        for j in range(LRU_SEG):
            acc = xm[j] * cw_ref[CONV_W - 1:CONV_W, cs] + cb_ref[:, cs]
            for k in range(1, CONV_W):
                src = xm[j - k] if j >= k else before[k - j]
                acc = acc + src * cw_ref[CONV_W - 1 - k:CONV_W - k, cs]
            xcm.append(acc)

        xc_rows = to_rows(sxc_ref, xcm)
        pre = jnp.dot(xc_rows.astype(BF16), wai_ref[n], preferred_element_type=F32)
        rm = to_segments(sr_ref, pre[:, :LRU_BW])
        im = to_segments(si_ref, pre[:, LRU_BW:])
        decay = -LRU_C * _softplus(-lam_ref[:, cs])

        hloc, aprod = [], []
        h = None
        for j in range(LRU_SEG):
            r = jax.nn.sigmoid(rm[j] + ba_ref[:, cs])
            i = jax.nn.sigmoid(im[j] + bi_ref[:, cs])
            log_a = decay * r
            a = jnp.exp(log_a)
            u = jnp.sqrt(-jnp.tanh(log_a) * (1.0 + a * a)) * (i * xcm[j])
            h = u if j == 0 else a * h + u
            ap = a if j == 0 else a * aprod[-1]
            hloc.append(h)
            aprod.append(ap)

        a_end, h_end = aprod[-1], hloc[-1]
        for dd in (1, 2, 4):
            keep = sub >= dd
            a_s = jnp.where(keep, pltpu.roll(a_end, dd, 0), 1.0)
            h_s = jnp.where(keep, pltpu.roll(h_end, dd, 0), 0.0)
            h_end = a_end * h_s + h_end
            a_end = a_end * a_s
        seg_state = h_end + a_end * h_ref[:, cs]
        state_in = shift_in(seg_state, h_ref[:, cs])
        h_ref[:, cs] = seg_state[SUBLANES - 1:SUBLANES]
        hm = [hloc[j] + aprod[j] * state_in for j in range(LRU_SEG)]

        h_rows = to_rows(sh_ref, hm)
        y = h_rows * _gelu_tanh(gl_ref[:, cs].astype(F32))
        y_ref[:, cs] = y.astype(y_ref.dtype)


def _lru(proj, conv_w, conv_b, w_ai, b_a, b_i, lam, layer, batch, seq, cast=None):
    t = min(LRU_T, seq)
    nt = seq // t
    w = W_LRU
    vmem = 2 * 3 * t * w * 2 + LRU_BLOCKS * LRU_BW * 2 * LRU_BW * 2 + 64 * t * LRU_BW * 4
    out_specs = pl.BlockSpec((t, w), lambda b, s: (b * nt + s, 0))
    out_shape = jax.ShapeDtypeStruct((batch * seq, w), BF16)
    args = [proj, proj, conv_w, conv_b, w_ai, b_a, b_i, lam]
    cast_in = []
    if cast is not None:
        assert batch * nt >= CAST_BLOCKS
        cast_in, cast_out, cast_shape, cast_vmem = _cast_payload(
            cast[0], cast[1], cast[2], lambda b, s: b * nt + s)
        out_specs, out_shape = [out_specs] + cast_out, [out_shape] + cast_shape
        args += [cast[0], cast[1]]
        vmem += cast_vmem
    return pl.pallas_call(
        functools.partial(_lru_kernel, cast=cast is not None),
        grid=(batch, nt),
        in_specs=[
            pl.BlockSpec((t, w), lambda b, s: (b * nt + s, COL_XLRU)),
            pl.BlockSpec((t, w), lambda b, s: (b * nt + s, COL_GLRU)),
            _layer_resident((CONV_W, w), layer),
            _layer_resident((1, w), layer),
            _layer_resident((LRU_BLOCKS, LRU_BW, 2 * LRU_BW), layer),
            _layer_resident((1, w), layer),
            _layer_resident((1, w), layer),
            _layer_resident((1, w), layer),
        ] + cast_in,
        out_specs=out_specs,
        out_shape=out_shape,
        scratch_shapes=[pltpu.VMEM((SUBLANES, w), F32), pltpu.VMEM((1, w), F32)]
        + [pltpu.VMEM((t // LRU_SEG * LRU_PITCH, LRU_BW), F32)] * 5,
        compiler_params=_params(("arbitrary", "arbitrary"), vmem + (8 << 20)),
        name="rglru",
    )(*args)


def _log_sigmoid(x):
    return jnp.minimum(x, 0.0) - jnp.log1p(jnp.exp(-jnp.abs(x)))


def _gla_kernel(q_ref, k_ref, v_ref, r_ref, gk_ref, wgk2_ref, bgk_ref, ng_ref, *rest, cast):
    if cast:
        wu_in, wd_in, y_ref, wu_out, wd_out, st_ref = rest
        _cast_blocks(wu_in, wd_in, wu_out, wd_out)
    else:
        y_ref, st_ref = rest
    t_rows = q_ref.shape[0]
    c = GLA_CHUNK
    hk = GLA_H * GLA_DK

    @pl.when(pl.program_id(1) == 0)
    def _():
        st_ref[...] = jnp.zeros_like(st_ref)

    z = jnp.dot(gk_ref[...], wgk2_ref[...], preferred_element_type=F32,
                precision=lax.Precision.HIGHEST) + bgk_ref[...]
    la = _log_sigmoid(z) * (1.0 / GLA_NORMALIZER)
    pos = lax.broadcasted_iota(jnp.int32, (t_rows, hk), 0) & (c - 1)
    dd = 1
    while dd < c:
        la = la + jnp.where(pos >= dd, pltpu.roll(la, dd, 0), 0.0)
        dd *= 2
    bcum = la

    tril = (lax.broadcasted_iota(jnp.int32, (c, c), 0)
            >= lax.broadcasted_iota(jnp.int32, (c, c), 1))
    nt_dims = (((1,), (1,)), ((), ()))
    tn_dims = (((0,), (0,)), ((), ()))
    n_chunks = t_rows // c

    qe_c, g_c, o_intra, upd = [], [], [], []
    for ci in range(n_chunks):
        rows = slice(ci * c, (ci + 1) * c)
        bc = bcum[rows]
        b_last = bcum[(ci + 1) * c - 1:(ci + 1) * c]
        q = q_ref[rows, :].astype(F32) * (GLA_DK ** -0.5)
        k = k_ref[rows, :].astype(F32)
        qe = (q * jnp.exp(bc)).astype(BF16)
        ke = (k * jnp.exp(-bc)).astype(BF16)
        kd = (k * jnp.exp(b_last - bc)).astype(BF16)
        qe_c.append(qe)
        g_c.append(jnp.exp(b_last))
        o_h, upd_h = [], []
        for h in range(GLA_H):
            ks = slice(h * GLA_DK, (h + 1) * GLA_DK)
            v_h = v_ref[rows, h * GLA_DV:(h + 1) * GLA_DV]
            s = lax.dot_general(qe[:, ks], ke[:, ks], nt_dims, preferred_element_type=F32)
            s = jnp.where(tril, s, 0.0).astype(BF16)
            o_h.append(jnp.dot(s, v_h, preferred_element_type=F32))
            upd_h.append(lax.dot_general(v_h, kd[:, ks], tn_dims,
                                         preferred_element_type=F32))
        o_intra.append(o_h)
        upd.append(upd_h)

    for h in range(GLA_H):
        ks = slice(h * GLA_DK, (h + 1) * GLA_DK)
        vs = slice(h * GLA_DV, (h + 1) * GLA_DV)
        st = st_ref[h]
        for ci in range(n_chunks):
            rows = slice(ci * c, (ci + 1) * c)
            o = o_intra[ci][h] + lax.dot_general(qe_c[ci][:, ks], st.astype(BF16), nt_dims,
                                                 preferred_element_type=F32)
            st = st * g_c[ci][:, ks] + upd[ci][h]
            o = o * lax.rsqrt(jnp.mean(o * o, axis=-1, keepdims=True) + EPS) * ng_ref[...]
            o = o * _silu(r_ref[rows, vs].astype(F32))
            y_ref[rows, vs] = o.astype(y_ref.dtype)
        st_ref[h] = st


def _gla(proj, gk, w_gk2_p, b_gk, norm_g, layer, batch, seq, cast=None):
    t = min(GLA_T, seq)
    while cast is not None and batch * (seq // t) < CAST_BLOCKS:
        t //= 2
    nt = seq // t
    hk, hv = GLA_H * GLA_DK, GLA_H * GLA_DV
    vmem = (2 * (2 * t * hk * 2 + 3 * t * hv * 2 + t * GK_PAD * 4) + t * hk * 4
            + GLA_H * GLA_DV * GLA_DK * 4 + 8 * t * hk * 4)
    out_specs = pl.BlockSpec((t, hv), lambda b, s: (b * nt + s, 0))
    out_shape = jax.ShapeDtypeStruct((batch * seq, hv), BF16)
    args = [proj, proj, proj, proj, gk, w_gk2_p, b_gk, norm_g]
    cast_in = []
    if cast is not None:
        cast_in, cast_out, cast_shape, cast_vmem = _cast_payload(
            cast[0], cast[1], cast[2], lambda b, s: b * nt + s)
        out_specs, out_shape = [out_specs] + cast_out, [out_shape] + cast_shape
        args += [cast[0], cast[1]]
        vmem += cast_vmem
    return pl.pallas_call(
        functools.partial(_gla_kernel, cast=cast is not None),
        grid=(batch, nt),
        in_specs=[
            pl.BlockSpec((t, hk), lambda b, s: (b * nt + s, COL_Q)),
            pl.BlockSpec((t, hk), lambda b, s: (b * nt + s, COL_K)),
            pl.BlockSpec((t, hv), lambda b, s: (b * nt + s, COL_V)),
            pl.BlockSpec((t, hv), lambda b, s: (b * nt + s, COL_R)),
            pl.BlockSpec((t, GK_PAD), lambda b, s: (b * nt + s, 0)),
            _layer_resident((GK_PAD, hk), layer),
            _layer_resident((1, hk), layer),
            _layer_resident((1, GLA_DV), layer),
        ] + cast_in,
        out_specs=out_specs,
        out_shape=out_shape,
        scratch_shapes=[pltpu.VMEM((GLA_H, GLA_DV, GLA_DK), F32)],
        compiler_params=_params(("arbitrary", "arbitrary"), vmem + (8 << 20)),
        name="gla",
    )(*args)


def _merge_kernel(x_ref, ya_ref, yb_ref, gl_ref, bg_ref, wb_ref, wo_ref, post_g_ref, o_ref):
    d = x_ref.shape[1]
    z_a = jnp.dot(ya_ref[...], wb_ref[:W_LRU, :], preferred_element_type=F32)
    z_b = jnp.dot(yb_ref[...], wb_ref[W_LRU:, :], preferred_element_type=F32)
    g_a = jax.nn.sigmoid(gl_ref[:, :d].astype(F32) + bg_ref[:, :d])
    g_b = jax.nn.sigmoid(gl_ref[:, d:].astype(F32) + bg_ref[:, d:])
    merged = (g_a * z_a + g_b * z_b).astype(BF16)
    h = jnp.dot(merged, wo_ref[...], preferred_element_type=F32)
    o_ref[...] = x_ref[...] + _rms(h, post_g_ref[...])


def _merge(x2, y_a, y_b, proj, b_gate, w_branch, w_out, post_g, layer):
    m, d = x2.shape
    tm = min(MERGE_TM, m)
    wb_rows = w_branch.shape[1]
    vmem = (2 * 2 * tm * d * 4 + 2 * 2 * tm * W_LRU * 2 + 2 * tm * 2 * d * 2
            + wb_rows * d * 2 + d * d * 2 + 6 * tm * d * 4)
    return pl.pallas_call(
        _merge_kernel,
        grid=(m // tm,),
        in_specs=[
            pl.BlockSpec((tm, d), lambda i: (i, 0)),
            pl.BlockSpec((tm, W_LRU), lambda i: (i, 0)),
            pl.BlockSpec((tm, GLA_H * GLA_DV), lambda i: (i, 0)),
            pl.BlockSpec((tm, N_BRANCH * d), lambda i: (i, COL_GATE)),
            _layer_resident((1, N_BRANCH * d), layer),
            _layer_resident((wb_rows, d), layer),
            _layer_resident((d, d), layer),
            _layer_resident((1, d), layer),
        ],
        out_specs=pl.BlockSpec((tm, d), lambda i: (i, 0)),
        out_shape=jax.ShapeDtypeStruct((m, d), F32),
        compiler_params=_params(("arbitrary",), vmem),
        name="merge",
    )(x2, y_a, y_b, proj, b_gate, w_branch, w_out, post_g)


def _xattn_kernel(x_ref, pre_g_ref, wq_ref, kk_ref, vv_ref, wo_ref, post_g_ref, o_ref):
    xn = _rms(x_ref[...], pre_g_ref[...]).astype(BF16)
    q = (jnp.dot(xn, wq_ref[...], preferred_element_type=F32) * (XA_DH ** -0.5)).astype(BF16)
    nt_dims = (((1,), (1,)), ((), ()))
    outs = []
    for h in range(XA_H):
        hs = slice(h * XA_DH, (h + 1) * XA_DH)
        s = lax.dot_general(q[:, hs], kk_ref[:, hs], nt_dims,
                            preferred_element_type=F32)
        p = jnp.exp(s - jnp.max(s, axis=-1, keepdims=True))
        pv = jnp.dot(p.astype(BF16), vv_ref[:, hs], preferred_element_type=F32)
        outs.append(pv / jnp.sum(p, axis=-1, keepdims=True))
    o = jnp.concatenate(outs, axis=1).astype(BF16)
    h_out = jnp.dot(o, wo_ref[...], preferred_element_type=F32)
    o_ref[...] = x_ref[...] + _rms(h_out, post_g_ref[...])


def _xattn(x2, pre_g, w_q, kv, w_o, post_g, layer, batch, seq, mem_len):
    m, d = x2.shape
    tm = min(XA_TM, seq)
    nt = seq // tm
    hd = XA_H * XA_DH
    vmem = (2 * 2 * tm * d * 4 + 2 * d * hd * 2 + 2 * 2 * mem_len * hd * 2
            + 4 * tm * d * 4 + 8 * tm * mem_len * 4)
    return pl.pallas_call(
        _xattn_kernel,
        grid=(batch, nt),
        in_specs=[
            pl.BlockSpec((tm, d), lambda b, s: (b * nt + s, 0)),
            _layer_resident((1, d), layer),
            _layer_resident((d, hd), layer),
            pl.BlockSpec((mem_len, hd), lambda b, s: (b, 0)),
            pl.BlockSpec((mem_len, hd), lambda b, s: (b, 1)),
            _layer_resident((hd, d), layer),
            _layer_resident((1, d), layer),
        ],
        out_specs=pl.BlockSpec((tm, d), lambda b, s: (b * nt + s, 0)),
        out_shape=jax.ShapeDtypeStruct((m, d), F32),
        compiler_params=_params(("arbitrary", "arbitrary"), vmem),
        name="xattn",
    )(x2, pre_g, w_q, kv, kv, w_o, post_g)


def _pack_w_in(w_in):
    n_head = 2 * W_LRU + 2 * GLA_H * GLA_DK + 2 * GLA_H * GLA_DV
    head = w_in[..., :n_head]
    gate = w_in[..., n_head + GLA_RANK:]
    w_main = jnp.concatenate([gate, head], axis=-1).astype(BF16)
    w_gk = w_in[..., n_head:n_head + GK_PAD].astype(BF16)
    return w_main, w_gk


def _rows(v):
    return v.reshape(v.shape[0], 1, -1)


def kernel(x, mem, ffn1_pre_g, ffn1_post_g, ffn1_w_up, ffn1_w_down, mix_pre_g, mix_post_g, w_in,
           conv_w, conv_b, lru_w_a, lru_b_a, lru_w_i, lru_b_i, lru_lambda, gla_w_gk2, gla_b_gk,
           gla_norm_g, b_gate, w_branch, w_out, xa_pre_g, xa_post_g, mem_g, xa_w_q, xa_w_kv,
           xa_w_o, ffn2_pre_g, ffn2_post_g, ffn2_w_up, ffn2_w_down):
    batch, seq, d = x.shape
    mem_len = mem.shape[1]
    depth = ffn1_w_up.shape[0]
    x2 = x.reshape(batch * seq, d)
    mem2 = mem.reshape(batch * mem_len, d)

    ffn1_up, ffn1_down = ffn1_w_up[0].astype(BF16), ffn1_w_down[0].astype(BF16)
    w_main, w_gk = _pack_w_in(w_in)
    w_ai = jnp.concatenate([lru_w_a, lru_w_i], axis=-1).astype(BF16)
    w_gk2_p = jnp.pad(gla_w_gk2, ((0, 0), (0, GK_PAD - GLA_RANK), (0, 0)))
    wb, wo = w_branch.astype(BF16), w_out.astype(BF16)
    xa_q, xa_kv, xa_o = xa_w_q.astype(BF16), xa_w_kv.astype(BF16), xa_w_o.astype(BF16)
    ffn1_pre, ffn1_post = _rows(ffn1_pre_g), _rows(ffn1_post_g)
    ffn2_pre, ffn2_post = _rows(ffn2_pre_g), _rows(ffn2_post_g)
    mix_pre, mix_post = _rows(mix_pre_g), _rows(mix_post_g)
    xa_pre, xa_post, mem_gain = _rows(xa_pre_g), _rows(xa_post_g), _rows(mem_g)
    conv_bias, b_a, b_i, lam = _rows(conv_b), _rows(lru_b_a), _rows(lru_b_i), _rows(lru_lambda)
    b_gk, norm_g, b_gate_r = _rows(gla_b_gk), _rows(gla_norm_g), _rows(b_gate)

    for l in range(depth):
        x2 = _ffn(x2, ffn1_pre, ffn1_up, ffn1_down, ffn1_post, l)

        proj, gk = _norm_matmul(x2, mix_pre, w_main, l, w_side=w_gk, name="mix_in_proj")
        y_a, ffn2_up, ffn2_down = _lru(proj, conv_w, conv_bias, w_ai, b_a, b_i, lam, l, batch, seq,
                                       cast=(ffn2_w_up, ffn2_w_down, l))
        if l + 1 < depth:
            y_b, ffn1_up, ffn1_down = _gla(proj, gk, w_gk2_p, b_gk, norm_g, l, batch, seq,
                                           cast=(ffn1_w_up, ffn1_w_down, l + 1))
        else:
            y_b = _gla(proj, gk, w_gk2_p, b_gk, norm_g, l, batch, seq)
        x2 = _merge(x2, y_a, y_b, proj, b_gate_r, wb, wo, mix_post, l)

        kv = _norm_matmul(mem2, mem_gain, xa_kv, l, name="mem_kv_proj")
        x2 = _xattn(x2, xa_pre, xa_q, kv, xa_o, xa_post, l, batch, seq, mem_len)

        x2 = _ffn(x2, ffn2_pre, ffn2_up, ffn2_down, ffn2_post, l)

    return x2.reshape(batch, seq, d)
```

```python
import functools
import math

import jax
import jax.numpy as jnp
from jax import lax
from jax.experimental import pallas as pl
from jax.experimental.pallas import tpu as pltpu

F32 = jnp.float32
BF16 = jnp.bfloat16

D_MODEL = 2048
D_FF = 5504
FFN_RES_SCALE = 0.5
W_LRU = D_MODEL // 2
LRU_BLOCKS = 8
LRU_BW = W_LRU // LRU_BLOCKS
CONV_W = 4
LRU_C = 8.0
GLA_H = 4
GLA_DK = 128
GLA_DV = 256
GLA_RANK = 16
GLA_NORMALIZER = 16.0
GLA_CHUNK = 64
XA_H = 4
XA_DH = 128
N_BRANCH = 2
EPS = 1e-6

LANES = 128
SUBLANES = 8
V7X_VMEM_BYTES = 64 * 1024 * 1024
VMEM_LIMIT_CAP = V7X_VMEM_BYTES - 6 * 1024 * 1024

FFN_TF = 512
FFN_NF = -(-D_FF // FFN_TF)
FFN_TM = 1024
FFN_SLABS = 8
PROJ_TM = 1024
PROJ_TN = 2304
N_PROJ = N_BRANCH * D_MODEL + 2 * W_LRU + 2 * GLA_H * GLA_DV + 2 * GLA_H * GLA_DK
GK_PAD = LANES
LRU_T = 256
GLA_T = 512
MERGE_TM = 512
XA_TM = 512

COL_GATE = 0
COL_XLRU = (N_BRANCH * D_MODEL) // W_LRU
COL_GLRU = COL_XLRU + 1
COL_Q = (N_BRANCH * D_MODEL + 2 * W_LRU) // (GLA_H * GLA_DK)
COL_K = COL_Q + 1
COL_V = (N_BRANCH * D_MODEL + 2 * W_LRU + 2 * GLA_H * GLA_DK) // (GLA_H * GLA_DV)
COL_R = COL_V + 1


def _params(semantics, vmem_bytes):
    return pltpu.CompilerParams(dimension_semantics=semantics,
                                vmem_limit_bytes=int(min(VMEM_LIMIT_CAP, vmem_bytes)))


def _layer_resident(tail, layer):
    zeros = (0,) * len(tail)
    return pl.BlockSpec((None,) + tuple(tail), lambda *_: (layer,) + zeros,
                        pipeline_mode=pl.Buffered(1))


def _rms(x, g):
    ms = jnp.mean(x * x, axis=-1, keepdims=True)
    return x * lax.rsqrt(ms + EPS) * g


def _silu(x):
    return x * jax.nn.sigmoid(x)


def _ffn_window_start(j, base=0):
    return LANES * (base // LANES + jnp.minimum(j * (FFN_TF // LANES), (D_FF - FFN_TF) // LANES))


def _ffn_kernel(xnext_ref, xprev_ref, pre_g_ref, wg_ref, wu_ref, wd_ref, post_g_ref, o_ref,
                xn_even, xn_odd, acc_even, acc_odd, *, n_tiles):
    r = pl.program_id(0)
    j = pl.program_id(1)
    nf = pl.num_programs(1)
    slab = xnext_ref.shape[0]
    n_slabs = xn_even.shape[0] // slab
    row0 = pl.multiple_of(jnp.minimum(j, n_slabs - 1) * slab, slab)
    group = 2 * SUBLANES

    def pre_norm(xn_dst):
        for g0 in range(0, slab, group):
            y = _rms(xnext_ref[g0:g0 + group, :], pre_g_ref[...])
            xn_dst[pl.ds(row0 + g0, group), :] = y.astype(BF16)

    def matmul_step(xn_src, acc):
        xn = xn_src[...]
        gate = jnp.dot(xn, wg_ref[...], preferred_element_type=F32)
        up = jnp.dot(xn, wu_ref[...], preferred_element_type=F32)
        act = _silu(gate) * up
        covered = jnp.where(j == nf - 1, nf * FFN_TF - D_FF, 0)
        col = lax.broadcasted_iota(jnp.int32, act.shape, 1)
        act = jnp.where(col >= covered, act, 0.0).astype(BF16)
        prev = jnp.where(j == 0, 0.0, acc[...])
        acc[...] = prev + jnp.dot(act, wd_ref[...], preferred_element_type=F32)

    def post_norm(acc_src):
        for g0 in range(0, slab, group):
            h = acc_src[pl.ds(row0 + g0, group), :]
            o_ref[g0:g0 + group, :] = (xprev_ref[g0:g0 + group, :]
                                       + FFN_RES_SCALE * _rms(h, post_g_ref[...]))

    @pl.when(r == 0)
    def _():
        @pl.when(j == 0)
        def _():
            acc_even[...] = jnp.zeros_like(acc_even)
            acc_odd[...] = jnp.zeros_like(acc_odd)

        pre_norm(xn_even)

    steady = (r >= 1) & (r <= n_tiles)

    @pl.when(steady & (r % 2 == 1))
    def _():
        pre_norm(xn_odd)
        post_norm(acc_odd)
        matmul_step(xn_even, acc_even)

    @pl.when(steady & (r % 2 == 0))
    def _():
        pre_norm(xn_even)
        post_norm(acc_even)
        matmul_step(xn_odd, acc_odd)

    @pl.when(r == n_tiles + 1)
    def _():
        post_norm(acc_odd if (n_tiles - 1) % 2 else acc_even)


def _ffn(x2, pre_g, w_up, w_down, post_g, layer):
    m, d = x2.shape
    tm, tf = min(FFN_TM, m), FFN_TF
    n_tiles = m // tm
    slab = tm // FFN_SLABS
    vmem = (2 * tm * d * (2 + 4)
            + 2 * 3 * d * tf * 2
            + 3 * 2 * slab * d * 4
            + 2 * tm * d * 4 + 4 * tm * tf * 4)

    def next_slab(r, j):
        return (jnp.minimum(r, n_tiles - 1) * FFN_SLABS + jnp.minimum(j, FFN_SLABS - 1), 0)

    def prev_slab(r, j):
        tile = jnp.minimum(r - 2, n_tiles - 1)
        return (jnp.where(r < 2, 0, tile * FFN_SLABS + jnp.minimum(j, FFN_SLABS - 1)), 0)

    def window(r, j):
        return jnp.where(r == 0, 0, jnp.where(r == n_tiles + 1, FFN_NF - 1, j))

    return pl.pallas_call(
        functools.partial(_ffn_kernel, n_tiles=n_tiles),
        grid=(n_tiles + 2, FFN_NF),
        in_specs=[
            pl.BlockSpec((slab, d), next_slab),
            pl.BlockSpec((slab, d), prev_slab),
            _layer_resident((1, d), layer),
            pl.BlockSpec((pl.Element(d), pl.Element(tf)),
                         lambda r, j: (0, _ffn_window_start(window(r, j)))),
            pl.BlockSpec((pl.Element(d), pl.Element(tf)),
                         lambda r, j: (0, _ffn_window_start(window(r, j), base=D_FF))),
            pl.BlockSpec((pl.Element(tf), pl.Element(d)),
                         lambda r, j: (_ffn_window_start(window(r, j)), 0)),
            _layer_resident((1, d), layer),
        ],
        out_specs=pl.BlockSpec((slab, d), prev_slab),
        out_shape=jax.ShapeDtypeStruct((m, d), F32),
        scratch_shapes=[pltpu.VMEM((tm, d), BF16), pltpu.VMEM((tm, d), BF16),
                        pltpu.VMEM((tm, d), F32), pltpu.VMEM((tm, d), F32)],
        compiler_params=_params(("arbitrary", "arbitrary"), vmem),
        name="ffn",
    )(x2, x2, pre_g, w_up, w_up, w_down, post_g)


CAST_BLOCKS = D_FF // LANES
CAST_UP_COLS = 2 * D_FF // CAST_BLOCKS
CAST_DOWN_ROWS = D_FF // CAST_BLOCKS


def _cast_payload(w_up, w_down, layer, step_of):
    d = w_up.shape[1]

    def blk(*idx):
        return jnp.minimum(step_of(*idx), CAST_BLOCKS - 1)

    in_specs = [pl.BlockSpec((None, d, CAST_UP_COLS), lambda *idx: (layer, 0, blk(*idx))),
                pl.BlockSpec((None, CAST_DOWN_ROWS, d), lambda *idx: (layer, blk(*idx), 0))]
    out_specs = [pl.BlockSpec((d, CAST_UP_COLS), lambda *idx: (0, blk(*idx))),
                 pl.BlockSpec((CAST_DOWN_ROWS, d), lambda *idx: (blk(*idx), 0))]
    out_shape = [jax.ShapeDtypeStruct(w_up.shape[1:], BF16),
                 jax.ShapeDtypeStruct(w_down.shape[1:], BF16)]
    vmem = 2 * (d * CAST_UP_COLS + CAST_DOWN_ROWS * d) * (4 + 2)
    return in_specs, out_specs, out_shape, vmem


def _cast_blocks(wu_in, wd_in, wu_out, wd_out):
    wu_out[...] = wu_in[...].astype(BF16)
    wd_out[...] = wd_in[...].astype(BF16)


def _norm_matmul_kernel(x_ref, g_ref, w_ref, o_ref, xn_ref):
    @pl.when(pl.program_id(1) == 0)
    def _():
        xn_ref[...] = _rms(x_ref[...], g_ref[...]).astype(BF16)

    o_ref[...] = jnp.dot(xn_ref[...], w_ref[...], preferred_element_type=F32).astype(o_ref.dtype)


def _norm_matmul_side_kernel(x_ref, g_ref, w_ref, ws_ref, o_ref, side_ref, xn_ref):
    @pl.when(pl.program_id(1) == 0)
    def _():
        xn = _rms(x_ref[...], g_ref[...]).astype(BF16)
        xn_ref[...] = xn
        side_ref[...] = jnp.dot(xn, ws_ref[...], preferred_element_type=F32)

    o_ref[...] = jnp.dot(xn_ref[...], w_ref[...], preferred_element_type=F32).astype(o_ref.dtype)


def _norm_matmul(x2, g, w, layer, w_side=None, name="norm_matmul"):
    m, d = x2.shape
    n = w.shape[2]
    tm, tn = min(PROJ_TM, m), min(PROJ_TN, n)
    vmem = (2 * tm * d * 4 + tm * d * 2 + 2 * d * tn * 2 + 2 * tm * tn * 2
            + 2 * tm * tn * 4 + tm * d * 4)
    in_specs = [
        pl.BlockSpec((tm, d), lambda i, j: (i, 0)),
        _layer_resident((1, d), layer),
        pl.BlockSpec((None, d, tn), lambda i, j: (layer, 0, j)),
    ]
    out_specs = pl.BlockSpec((tm, tn), lambda i, j: (i, j))
    out_shape = jax.ShapeDtypeStruct((m, n), BF16)
    args = [x2, g, w]
    kern = _norm_matmul_kernel
    if w_side is not None:
        ns = w_side.shape[2]
        in_specs.append(_layer_resident((d, ns), layer))
        out_specs = [out_specs, pl.BlockSpec((tm, ns), lambda i, j: (i, 0))]
        out_shape = [out_shape, jax.ShapeDtypeStruct((m, ns), F32)]
        args.append(w_side)
        kern = _norm_matmul_side_kernel
        vmem += d * ns * 2 + 2 * tm * ns * 4
    return pl.pallas_call(
        kern,
        grid=(m // tm, n // tn),
        in_specs=in_specs,
        out_specs=out_specs,
        out_shape=out_shape,
        scratch_shapes=[pltpu.VMEM((tm, d), BF16)],
        compiler_params=_params(("arbitrary", "arbitrary"), vmem),
        name=name,
    )(*args)


def _gelu_tanh(x):
    c = math.sqrt(2.0 / math.pi)
    return 0.5 * x * (1.0 + jnp.tanh(c * (x + 0.044715 * (x * x * x))))


def _softplus(x):
    return jnp.maximum(x, 0.0) + jnp.log1p(jnp.exp(-jnp.abs(x)))


LRU_SEG = 32
LRU_PITCH = LRU_SEG + 4


def _lru_kernel(xl_ref, gl_ref, cw_ref, cb_ref, wai_ref, ba_ref, bi_ref, lam_ref, *rest, cast):
    if cast:
        wu_in, wd_in, y_ref, wu_out, wd_out = rest[:5]
        _cast_blocks(wu_in, wd_in, wu_out, wd_out)
        scratch = rest[5:]
    else:
        y_ref, scratch = rest[0], rest[1:]
    tail_ref, h_ref, sx_ref, sxc_ref, sr_ref, si_ref, sh_ref = scratch
    t_rows = xl_ref.shape[0]
    n_seg = t_rows // LRU_SEG
    assert n_seg == SUBLANES

    @pl.when(pl.program_id(1) == 0)
    def _():
        tail_ref[...] = jnp.zeros_like(tail_ref)
        h_ref[...] = jnp.zeros_like(h_ref)

    sub = lax.broadcasted_iota(jnp.int32, (SUBLANES, LRU_BW), 0)

    def to_segments(ref, v):
        for sg in range(n_seg):
            ref[sg * LRU_PITCH:sg * LRU_PITCH + LRU_SEG, :] = v[sg * LRU_SEG:(sg + 1) * LRU_SEG]
        return [ref[pl.ds(j, SUBLANES, stride=LRU_PITCH), :] for j in range(LRU_SEG)]

    def to_rows(ref, vs):
        for j, v in enumerate(vs):
            ref[pl.ds(j, SUBLANES, stride=LRU_PITCH), :] = v
        return jnp.concatenate([ref[sg * LRU_PITCH:sg * LRU_PITCH + LRU_SEG, :]
                                for sg in range(n_seg)], axis=0)

    def shift_in(v, first):
        return jnp.where(sub == 0, jnp.broadcast_to(first, v.shape), pltpu.roll(v, 1, 0))

    for n in range(LRU_BLOCKS):
        cs = slice(n * LRU_BW, (n + 1) * LRU_BW)
        x = xl_ref[:, cs].astype(F32)
        xm = to_segments(sx_ref, x)
        tail = tail_ref[:, cs]
        before = {k: shift_in(xm[LRU_SEG - k], tail[SUBLANES - k:SUBLANES - k + 1])
                  for k in range(1, CONV_W)}
        tail_ref[:, cs] = x[t_rows - SUBLANES:]
        xcm = []
        for j in range(LRU_SEG):
            acc = xm[j] * cw_ref[CONV_W - 1:CONV_W, cs] + cb_ref[:, cs]
            for k in range(1, CONV_W):
                src = xm[j - k] if j >= k else before[k - j]
                acc = acc + src * cw_ref[CONV_W - 1 - k:CONV_W - k, cs]
            xcm.append(acc)

        xc_rows = to_rows(sxc_ref, xcm)
        pre = jnp.dot(xc_rows.astype(BF16), wai_ref[n], preferred_element_type=F32)
        rm = to_segments(sr_ref, pre[:, :LRU_BW])
        im = to_segments(si_ref, pre[:, LRU_BW:])
        decay = -LRU_C * _softplus(-lam_ref[:, cs])

        hloc, aprod = [], []
        h = None
        for j in range(LRU_SEG):
            r = jax.nn.sigmoid(rm[j] + ba_ref[:, cs])
            i = jax.nn.sigmoid(im[j] + bi_ref[:, cs])
            log_a = decay * r
            a = jnp.exp(log_a)
            u = jnp.sqrt(-jnp.tanh(log_a) * (1.0 + a * a)) * (i * xcm[j])
            h = u if j == 0 else a * h + u
            ap = a if j == 0 else a * aprod[-1]
            hloc.append(h)
            aprod.append(ap)

        a_end, h_end = aprod[-1], hloc[-1]
        for dd in (1, 2, 4):
            keep = sub >= dd
            a_s = jnp.where(keep, pltpu.roll(a_end, dd, 0), 1.0)
            h_s = jnp.where(keep, pltpu.roll(h_end, dd, 0), 0.0)
            h_end = a_end * h_s + h_end
            a_end = a_end * a_s
        seg_state = h_end + a_end * h_ref[:, cs]
        state_in = shift_in(seg_state, h_ref[:, cs])
        h_ref[:, cs] = seg_state[SUBLANES - 1:SUBLANES]
        hm = [hloc[j] + aprod[j] * state_in for j in range(LRU_SEG)]

        h_rows = to_rows(sh_ref, hm)
        y = h_rows * _gelu_tanh(gl_ref[:, cs].astype(F32))
        y_ref[:, cs] = y.astype(y_ref.dtype)


def _lru(proj, conv_w, conv_b, w_ai, b_a, b_i, lam, layer, batch, seq, cast=None):
    t = min(LRU_T, seq)
    nt = seq // t
    w = W_LRU
    vmem = 2 * 3 * t * w * 2 + LRU_BLOCKS * LRU_BW * 2 * LRU_BW * 2 + 64 * t * LRU_BW * 4
    out_specs = pl.BlockSpec((t, w), lambda b, s: (b * nt + s, 0))
    out_shape = jax.ShapeDtypeStruct((batch * seq, w), BF16)
    args = [proj, proj, conv_w, conv_b, w_ai, b_a, b_i, lam]
    cast_in = []
    if cast is not None:
        assert batch * nt >= CAST_BLOCKS
        cast_in, cast_out, cast_shape, cast_vmem = _cast_payload(
            cast[0], cast[1], cast[2], lambda b, s: b * nt + s)
        out_specs, out_shape = [out_specs] + cast_out, [out_shape] + cast_shape
        args += [cast[0], cast[1]]
        vmem += cast_vmem
    return pl.pallas_call(
        functools.partial(_lru_kernel, cast=cast is not None),
        grid=(batch, nt),
        in_specs=[
            pl.BlockSpec((t, w), lambda b, s: (b * nt + s, COL_XLRU)),
            pl.BlockSpec((t, w), lambda b, s: (b * nt + s, COL_GLRU)),
            _layer_resident((CONV_W, w), layer),
            _layer_resident((1, w), layer),
            _layer_resident((LRU_BLOCKS, LRU_BW, 2 * LRU_BW), layer),
            _layer_resident((1, w), layer),
            _layer_resident((1, w), layer),
            _layer_resident((1, w), layer),
        ] + cast_in,
        out_specs=out_specs,
        out_shape=out_shape,
        scratch_shapes=[pltpu.VMEM((SUBLANES, w), F32), pltpu.VMEM((1, w), F32)]
        + [pltpu.VMEM((t // LRU_SEG * LRU_PITCH, LRU_BW), F32)] * 5,
        compiler_params=_params(("arbitrary", "arbitrary"), vmem + (8 << 20)),
        name="rglru",
    )(*args)


def _log_sigmoid(x):
    return jnp.minimum(x, 0.0) - jnp.log1p(jnp.exp(-jnp.abs(x)))


def _gla_kernel(q_ref, k_ref, v_ref, r_ref, gk_ref, wgk2_ref, bgk_ref, ng_ref, *rest, cast):
    if cast:
        wu_in, wd_in, y_ref, wu_out, wd_out = rest[:5]
        _cast_blocks(wu_in, wd_in, wu_out, wd_out)
        scratch = rest[5:]
    else:
        y_ref, scratch = rest[0], rest[1:]
    st_ref, cum_refs = scratch[0], scratch[1:]
    t_rows = q_ref.shape[0]
    c = GLA_CHUNK
    hk = GLA_H * GLA_DK

    @pl.when(pl.program_id(1) == 0)
    def _():
        st_ref[...] = jnp.zeros_like(st_ref)

    z = jnp.dot(gk_ref[...], wgk2_ref[...], preferred_element_type=F32,
                precision=lax.Precision.HIGHEST) + bgk_ref[...]
    la = _log_sigmoid(z) * (1.0 / GLA_NORMALIZER)
    seg = t_rows // SUBLANES
    pitch = seg + 4
    assert seg in (c, c // 2)
    sub = lax.broadcasted_iota(jnp.int32, (SUBLANES, LANES), 0)
    bcum_blocks = []
    for b, ref in enumerate(cum_refs):
        for sg in range(SUBLANES):
            ref[sg * pitch:sg * pitch + seg, :] = la[sg * seg:(sg + 1) * seg,
                                                     b * LANES:(b + 1) * LANES]
        run, sums = None, []
        for jj in range(seg):
            v = ref[pl.ds(jj, SUBLANES, stride=pitch), :]
            run = v if jj == 0 else run + v
            sums.append(run)
        if seg < c:
            carry_in = jnp.where(sub % 2 == 1, pltpu.roll(sums[-1], 1, 0), 0.0)
            sums = [v + carry_in for v in sums]
        for jj, v in enumerate(sums):
            ref[pl.ds(jj, SUBLANES, stride=pitch), :] = v
        bcum_blocks.append(jnp.concatenate(
            [ref[sg * pitch:sg * pitch + seg, :] for sg in range(SUBLANES)], axis=0))
    bcum = jnp.concatenate(bcum_blocks, axis=1)

    tril = (lax.broadcasted_iota(jnp.int32, (c, c), 0)
            >= lax.broadcasted_iota(jnp.int32, (c, c), 1))
    nt_dims = (((1,), (1,)), ((), ()))
    tn_dims = (((0,), (0,)), ((), ()))
    n_chunks = t_rows // c

    qe_c, g_c, o_intra, upd = [], [], [], []
    for ci in range(n_chunks):
        rows = slice(ci * c, (ci + 1) * c)
        bc = bcum[rows]
        b_last = bcum[(ci + 1) * c - 1:(ci + 1) * c]
        q = q_ref[rows, :].astype(F32) * (GLA_DK ** -0.5)
        k = k_ref[rows, :].astype(F32)
        qe = (q * jnp.exp(bc)).astype(BF16)
        ke = (k * jnp.exp(-bc)).astype(BF16)
        kd = (k * jnp.exp(b_last - bc)).astype(BF16)
        qe_c.append(qe)
        g_c.append(jnp.exp(b_last))
        o_h, upd_h = [], []
        for h in range(GLA_H):
            ks = slice(h * GLA_DK, (h + 1) * GLA_DK)
            v_h = v_ref[rows, h * GLA_DV:(h + 1) * GLA_DV]
            s = lax.dot_general(qe[:, ks], ke[:, ks], nt_dims, preferred_element_type=F32)
            s = jnp.where(tril, s, 0.0).astype(BF16)
            o_h.append(jnp.dot(s, v_h, preferred_element_type=F32))
            upd_h.append(lax.dot_general(v_h, kd[:, ks], tn_dims,
                                         preferred_element_type=F32))
        o_intra.append(o_h)
        upd.append(upd_h)

    for h in range(GLA_H):
        ks = slice(h * GLA_DK, (h + 1) * GLA_DK)
        vs = slice(h * GLA_DV, (h + 1) * GLA_DV)
        st = st_ref[h]
        for ci in range(n_chunks):
            rows = slice(ci * c, (ci + 1) * c)
            o = o_intra[ci][h] + lax.dot_general(qe_c[ci][:, ks], st.astype(BF16), nt_dims,
                                                 preferred_element_type=F32)
            st = st * g_c[ci][:, ks] + upd[ci][h]
            o = o * lax.rsqrt(jnp.mean(o * o, axis=-1, keepdims=True) + EPS) * ng_ref[...]
            o = o * _silu(r_ref[rows, vs].astype(F32))
            y_ref[rows, vs] = o.astype(y_ref.dtype)
        st_ref[h] = st


def _gla(proj, gk, w_gk2_p, b_gk, norm_g, layer, batch, seq, cast=None):
    t = min(GLA_T, seq)
    while cast is not None and batch * (seq // t) < CAST_BLOCKS:
        t //= 2
    nt = seq // t
    hk, hv = GLA_H * GLA_DK, GLA_H * GLA_DV
    vmem = (2 * (2 * t * hk * 2 + 3 * t * hv * 2 + t * GK_PAD * 4) + t * hk * 4
            + GLA_H * GLA_DV * GLA_DK * 4 + 8 * t * hk * 4)
    out_specs = pl.BlockSpec((t, hv), lambda b, s: (b * nt + s, 0))
    out_shape = jax.ShapeDtypeStruct((batch * seq, hv), BF16)
    args = [proj, proj, proj, proj, gk, w_gk2_p, b_gk, norm_g]
    cast_in = []
    if cast is not None:
        cast_in, cast_out, cast_shape, cast_vmem = _cast_payload(
            cast[0], cast[1], cast[2], lambda b, s: b * nt + s)
        out_specs, out_shape = [out_specs] + cast_out, [out_shape] + cast_shape
        args += [cast[0], cast[1]]
        vmem += cast_vmem
    return pl.pallas_call(
        functools.partial(_gla_kernel, cast=cast is not None),
        grid=(batch, nt),
        in_specs=[
            pl.BlockSpec((t, hk), lambda b, s: (b * nt + s, COL_Q)),
            pl.BlockSpec((t, hk), lambda b, s: (b * nt + s, COL_K)),
            pl.BlockSpec((t, hv), lambda b, s: (b * nt + s, COL_V)),
            pl.BlockSpec((t, hv), lambda b, s: (b * nt + s, COL_R)),
            pl.BlockSpec((t, GK_PAD), lambda b, s: (b * nt + s, 0)),
            _layer_resident((GK_PAD, hk), layer),
            _layer_resident((1, hk), layer),
            _layer_resident((1, GLA_DV), layer),
        ] + cast_in,
        out_specs=out_specs,
        out_shape=out_shape,
        scratch_shapes=[pltpu.VMEM((GLA_H, GLA_DV, GLA_DK), F32)]
        + [pltpu.VMEM((t + 4 * SUBLANES, LANES), F32)] * (hk // LANES),
        compiler_params=_params(("arbitrary", "arbitrary"), vmem + (8 << 20)),
        name="gla",
    )(*args)


def _merge_kernel(x_ref, ya_ref, yb_ref, gl_ref, bg_ref, wb_ref, wo_ref, post_g_ref, o_ref):
    d = x_ref.shape[1]
    z_a = jnp.dot(ya_ref[...], wb_ref[:W_LRU, :], preferred_element_type=F32)
    z_b = jnp.dot(yb_ref[...], wb_ref[W_LRU:, :], preferred_element_type=F32)
    g_a = jax.nn.sigmoid(gl_ref[:, :d].astype(F32) + bg_ref[:, :d])
    g_b = jax.nn.sigmoid(gl_ref[:, d:].astype(F32) + bg_ref[:, d:])
    merged = (g_a * z_a + g_b * z_b).astype(BF16)
    h = jnp.dot(merged, wo_ref[...], preferred_element_type=F32)
    o_ref[...] = x_ref[...] + _rms(h, post_g_ref[...])


def _merge(x2, y_a, y_b, proj, b_gate, w_branch, w_out, post_g, layer):
    m, d = x2.shape
    tm = min(MERGE_TM, m)
    wb_rows = w_branch.shape[1]
    vmem = (2 * 2 * tm * d * 4 + 2 * 2 * tm * W_LRU * 2 + 2 * tm * 2 * d * 2
            + wb_rows * d * 2 + d * d * 2 + 6 * tm * d * 4)
    return pl.pallas_call(
        _merge_kernel,
        grid=(m // tm,),
        in_specs=[
            pl.BlockSpec((tm, d), lambda i: (i, 0)),
            pl.BlockSpec((tm, W_LRU), lambda i: (i, 0)),
            pl.BlockSpec((tm, GLA_H * GLA_DV), lambda i: (i, 0)),
            pl.BlockSpec((tm, N_BRANCH * d), lambda i: (i, COL_GATE)),
            _layer_resident((1, N_BRANCH * d), layer),
            _layer_resident((wb_rows, d), layer),
            _layer_resident((d, d), layer),
            _layer_resident((1, d), layer),
        ],
        out_specs=pl.BlockSpec((tm, d), lambda i: (i, 0)),
        out_shape=jax.ShapeDtypeStruct((m, d), F32),
        compiler_params=_params(("arbitrary",), vmem),
        name="merge",
    )(x2, y_a, y_b, proj, b_gate, w_branch, w_out, post_g)


def _xattn_kernel(x_ref, pre_g_ref, wq_ref, kk_ref, vv_ref, wo_ref, post_g_ref, o_ref):
    xn = _rms(x_ref[...], pre_g_ref[...]).astype(BF16)
    q = (jnp.dot(xn, wq_ref[...], preferred_element_type=F32) * (XA_DH ** -0.5)).astype(BF16)
    nt_dims = (((1,), (1,)), ((), ()))
    outs = []
    for h in range(XA_H):
        hs = slice(h * XA_DH, (h + 1) * XA_DH)
        s = lax.dot_general(q[:, hs], kk_ref[:, hs], nt_dims,
                            preferred_element_type=F32)
        p = jnp.exp(s - jnp.max(s, axis=-1, keepdims=True))
        pv = jnp.dot(p.astype(BF16), vv_ref[:, hs], preferred_element_type=F32)
        outs.append(pv / jnp.sum(p, axis=-1, keepdims=True))
    o = jnp.concatenate(outs, axis=1).astype(BF16)
    h_out = jnp.dot(o, wo_ref[...], preferred_element_type=F32)
    o_ref[...] = x_ref[...] + _rms(h_out, post_g_ref[...])


def _xattn(x2, pre_g, w_q, kv, w_o, post_g, layer, batch, seq, mem_len):
    m, d = x2.shape
    tm = min(XA_TM, seq)
    nt = seq // tm
    hd = XA_H * XA_DH
    vmem = (2 * 2 * tm * d * 4 + 2 * d * hd * 2 + 2 * 2 * mem_len * hd * 2
            + 4 * tm * d * 4 + 8 * tm * mem_len * 4)
    return pl.pallas_call(
        _xattn_kernel,
        grid=(batch, nt),
        in_specs=[
            pl.BlockSpec((tm, d), lambda b, s: (b * nt + s, 0)),
            _layer_resident((1, d), layer),
            _layer_resident((d, hd), layer),
            pl.BlockSpec((mem_len, hd), lambda b, s: (b, 0)),
            pl.BlockSpec((mem_len, hd), lambda b, s: (b, 1)),
            _layer_resident((hd, d), layer),
            _layer_resident((1, d), layer),
        ],
        out_specs=pl.BlockSpec((tm, d), lambda b, s: (b * nt + s, 0)),
        out_shape=jax.ShapeDtypeStruct((m, d), F32),
        compiler_params=_params(("arbitrary", "arbitrary"), vmem),
        name="xattn",
    )(x2, pre_g, w_q, kv, kv, w_o, post_g)


def _pack_w_in(w_in):
    n_head = 2 * W_LRU + 2 * GLA_H * GLA_DK + 2 * GLA_H * GLA_DV
    head = w_in[..., :n_head]
    gate = w_in[..., n_head + GLA_RANK:]
    w_main = jnp.concatenate([gate, head], axis=-1).astype(BF16)
    w_gk = w_in[..., n_head:n_head + GK_PAD].astype(BF16)
    return w_main, w_gk


def _rows(v):
    return v.reshape(v.shape[0], 1, -1)


def kernel(x, mem, ffn1_pre_g, ffn1_post_g, ffn1_w_up, ffn1_w_down, mix_pre_g, mix_post_g, w_in,
           conv_w, conv_b, lru_w_a, lru_b_a, lru_w_i, lru_b_i, lru_lambda, gla_w_gk2, gla_b_gk,
           gla_norm_g, b_gate, w_branch, w_out, xa_pre_g, xa_post_g, mem_g, xa_w_q, xa_w_kv,
           xa_w_o, ffn2_pre_g, ffn2_post_g, ffn2_w_up, ffn2_w_down):
    batch, seq, d = x.shape
    mem_len = mem.shape[1]
    depth = ffn1_w_up.shape[0]
    x2 = x.reshape(batch * seq, d)
    mem2 = mem.reshape(batch * mem_len, d)

    ffn1_up, ffn1_down = ffn1_w_up[0].astype(BF16), ffn1_w_down[0].astype(BF16)
    w_main, w_gk = _pack_w_in(w_in)
    w_ai = jnp.concatenate([lru_w_a, lru_w_i], axis=-1).astype(BF16)
    w_gk2_p = jnp.pad(gla_w_gk2, ((0, 0), (0, GK_PAD - GLA_RANK), (0, 0)))
    wb, wo = w_branch.astype(BF16), w_out.astype(BF16)
    xa_q, xa_kv, xa_o = xa_w_q.astype(BF16), xa_w_kv.astype(BF16), xa_w_o.astype(BF16)
    ffn1_pre, ffn1_post = _rows(ffn1_pre_g), _rows(ffn1_post_g)
    ffn2_pre, ffn2_post = _rows(ffn2_pre_g), _rows(ffn2_post_g)
    mix_pre, mix_post = _rows(mix_pre_g), _rows(mix_post_g)
    xa_pre, xa_post, mem_gain = _rows(xa_pre_g), _rows(xa_post_g), _rows(mem_g)
    conv_bias, b_a, b_i, lam = _rows(conv_b), _rows(lru_b_a), _rows(lru_b_i), _rows(lru_lambda)
    b_gk, norm_g, b_gate_r = _rows(gla_b_gk), _rows(gla_norm_g), _rows(b_gate)

    for l in range(depth):
        x2 = _ffn(x2, ffn1_pre, ffn1_up, ffn1_down, ffn1_post, l)

        proj, gk = _norm_matmul(x2, mix_pre, w_main, l, w_side=w_gk, name="mix_in_proj")
        y_a, ffn2_up, ffn2_down = _lru(proj, conv_w, conv_bias, w_ai, b_a, b_i, lam, l, batch, seq,
                                       cast=(ffn2_w_up, ffn2_w_down, l))
        if l + 1 < depth:
            y_b, ffn1_up, ffn1_down = _gla(proj, gk, w_gk2_p, b_gk, norm_g, l, batch, seq,
                                           cast=(ffn1_w_up, ffn1_w_down, l + 1))
        else:
            y_b = _gla(proj, gk, w_gk2_p, b_gk, norm_g, l, batch, seq)
        x2 = _merge(x2, y_a, y_b, proj, b_gate_r, wb, wo, mix_post, l)

        kv = _norm_matmul(mem2, mem_gain, xa_kv, l, name="mem_kv_proj")
        x2 = _xattn(x2, xa_pre, xa_q, kv, xa_o, xa_post, l, batch, seq, mem_len)

        x2 = _ffn(x2, ffn2_pre, ffn2_up, ffn2_down, ffn2_post, l)

    return x2.reshape(batch, seq, d)
```

```python
import functools
import math

import jax
import jax.numpy as jnp
from jax import lax
from jax.experimental import pallas as pl
from jax.experimental.pallas import tpu as pltpu

F32 = jnp.float32
BF16 = jnp.bfloat16

D_MODEL = 2048
D_FF = 5504
FFN_RES_SCALE = 0.5
W_LRU = D_MODEL // 2
LRU_BLOCKS = 8
LRU_BW = W_LRU // LRU_BLOCKS
CONV_W = 4
LRU_C = 8.0
GLA_H = 4
GLA_DK = 128
GLA_DV = 256
GLA_RANK = 16
GLA_NORMALIZER = 16.0
GLA_CHUNK = 64
XA_H = 4
XA_DH = 128
N_BRANCH = 2
EPS = 1e-6

LANES = 128
SUBLANES = 8
V7X_VMEM_BYTES = 64 * 1024 * 1024
VMEM_LIMIT_CAP = V7X_VMEM_BYTES - 6 * 1024 * 1024

FFN_TF = 512
FFN_NF = -(-D_FF // FFN_TF)
FFN_TM = 1024
FFN_SLABS = 8
PROJ_TM = 1024
PROJ_TN = 2304
N_PROJ = N_BRANCH * D_MODEL + 2 * W_LRU + 2 * GLA_H * GLA_DV + 2 * GLA_H * GLA_DK
GK_PAD = LANES
LRU_T = 256
GLA_T = 512
MERGE_TM = 512
XA_TM = 512

COL_GATE = 0
COL_XLRU = (N_BRANCH * D_MODEL) // W_LRU
COL_GLRU = COL_XLRU + 1
COL_Q = (N_BRANCH * D_MODEL + 2 * W_LRU) // (GLA_H * GLA_DK)
COL_K = COL_Q + 1
COL_V = (N_BRANCH * D_MODEL + 2 * W_LRU + 2 * GLA_H * GLA_DK) // (GLA_H * GLA_DV)
COL_R = COL_V + 1


def _params(semantics, vmem_bytes):
    return pltpu.CompilerParams(dimension_semantics=semantics,
                                vmem_limit_bytes=int(min(VMEM_LIMIT_CAP, vmem_bytes)))


def _layer_resident(tail, layer):
    zeros = (0,) * len(tail)
    return pl.BlockSpec((None,) + tuple(tail), lambda *_: (layer,) + zeros,
                        pipeline_mode=pl.Buffered(1))


def _rms(x, g):
    ms = jnp.mean(x * x, axis=-1, keepdims=True)
    return x * lax.rsqrt(ms + EPS) * g


def _silu(x):
    return x * jax.nn.sigmoid(x)


def _ffn_window_start(j, base=0):
    return LANES * (base // LANES + jnp.minimum(j * (FFN_TF // LANES), (D_FF - FFN_TF) // LANES))


def _ffn_kernel(xnext_ref, xprev_ref, pre_g_ref, wg_ref, wu_ref, wd_ref, post_g_ref, o_ref,
                xn_even, xn_odd, acc_even, acc_odd, *, n_tiles):
    r = pl.program_id(0)
    j = pl.program_id(1)
    nf = pl.num_programs(1)
    slab = xnext_ref.shape[0]
    n_slabs = xn_even.shape[0] // slab
    row0 = pl.multiple_of(jnp.minimum(j, n_slabs - 1) * slab, slab)
    group = 2 * SUBLANES

    def pre_norm(xn_dst):
        for g0 in range(0, slab, group):
            y = _rms(xnext_ref[g0:g0 + group, :], pre_g_ref[...])
            xn_dst[pl.ds(row0 + g0, group), :] = y.astype(BF16)

    def matmul_step(xn_src, acc):
        xn = xn_src[...]
        gate = jnp.dot(xn, wg_ref[...], preferred_element_type=F32)
        up = jnp.dot(xn, wu_ref[...], preferred_element_type=F32)
        act = _silu(gate) * up
        covered = jnp.where(j == nf - 1, nf * FFN_TF - D_FF, 0)
        col = lax.broadcasted_iota(jnp.int32, act.shape, 1)
        act = jnp.where(col >= covered, act, 0.0).astype(BF16)
        prev = jnp.where(j == 0, 0.0, acc[...])
        acc[...] = prev + jnp.dot(act, wd_ref[...], preferred_element_type=F32)

    def post_norm(acc_src):
        for g0 in range(0, slab, group):
            h = acc_src[pl.ds(row0 + g0, group), :]
            o_ref[g0:g0 + group, :] = (xprev_ref[g0:g0 + group, :]
                                       + FFN_RES_SCALE * _rms(h, post_g_ref[...]))

    @pl.when(r == 0)
    def _():
        @pl.when(j == 0)
        def _():
            acc_even[...] = jnp.zeros_like(acc_even)
            acc_odd[...] = jnp.zeros_like(acc_odd)

        pre_norm(xn_even)

    steady = (r >= 1) & (r <= n_tiles)

    @pl.when(steady & (r % 2 == 1))
    def _():
        pre_norm(xn_odd)
        post_norm(acc_odd)
        matmul_step(xn_even, acc_even)

    @pl.when(steady & (r % 2 == 0))
    def _():
        pre_norm(xn_even)
        post_norm(acc_even)
        matmul_step(xn_odd, acc_odd)

    @pl.when(r == n_tiles + 1)
    def _():
        post_norm(acc_odd if (n_tiles - 1) % 2 else acc_even)


def _ffn(x2, pre_g, w_up, w_down, post_g, layer):
    m, d = x2.shape
    tm, tf = min(FFN_TM, m), FFN_TF
    n_tiles = m // tm
    slab = tm // FFN_SLABS
    vmem = (2 * tm * d * (2 + 4)
            + 2 * 3 * d * tf * 2
            + 3 * 2 * slab * d * 4
            + 2 * tm * d * 4 + 4 * tm * tf * 4)

    def next_slab(r, j):
        return (jnp.minimum(r, n_tiles - 1) * FFN_SLABS + jnp.minimum(j, FFN_SLABS - 1), 0)

    def prev_slab(r, j):
        tile = jnp.minimum(r - 2, n_tiles - 1)
        return (jnp.where(r < 2, 0, tile * FFN_SLABS + jnp.minimum(j, FFN_SLABS - 1)), 0)

    def window(r, j):
        return jnp.where(r == 0, 0, jnp.where(r == n_tiles + 1, FFN_NF - 1, j))

    return pl.pallas_call(
        functools.partial(_ffn_kernel, n_tiles=n_tiles),
        grid=(n_tiles + 2, FFN_NF),
        in_specs=[
            pl.BlockSpec((slab, d), next_slab),
            pl.BlockSpec((slab, d), prev_slab),
            _layer_resident((1, d), layer),
            pl.BlockSpec((pl.Element(d), pl.Element(tf)),
                         lambda r, j: (0, _ffn_window_start(window(r, j)))),
            pl.BlockSpec((pl.Element(d), pl.Element(tf)),
                         lambda r, j: (0, _ffn_window_start(window(r, j), base=D_FF))),
            pl.BlockSpec((pl.Element(tf), pl.Element(d)),
                         lambda r, j: (_ffn_window_start(window(r, j)), 0)),
            _layer_resident((1, d), layer),
        ],
        out_specs=pl.BlockSpec((slab, d), prev_slab),
        out_shape=jax.ShapeDtypeStruct((m, d), F32),
        scratch_shapes=[pltpu.VMEM((tm, d), BF16), pltpu.VMEM((tm, d), BF16),
                        pltpu.VMEM((tm, d), F32), pltpu.VMEM((tm, d), F32)],
        compiler_params=_params(("arbitrary", "arbitrary"), vmem),
        name="ffn",
    )(x2, x2, pre_g, w_up, w_up, w_down, post_g)


CAST_BLOCKS = D_FF // LANES
CAST_UP_COLS = 2 * D_FF // CAST_BLOCKS
CAST_DOWN_ROWS = D_FF // CAST_BLOCKS


def _cast_payload(w_up, w_down, layer, step_of):
    d = w_up.shape[1]

    def blk(*idx):
        return jnp.minimum(step_of(*idx), CAST_BLOCKS - 1)

    in_specs = [pl.BlockSpec((None, d, CAST_UP_COLS), lambda *idx: (layer, 0, blk(*idx))),
                pl.BlockSpec((None, CAST_DOWN_ROWS, d), lambda *idx: (layer, blk(*idx), 0))]
    out_specs = [pl.BlockSpec((d, CAST_UP_COLS), lambda *idx: (0, blk(*idx))),
                 pl.BlockSpec((CAST_DOWN_ROWS, d), lambda *idx: (blk(*idx), 0))]
    out_shape = [jax.ShapeDtypeStruct(w_up.shape[1:], BF16),
                 jax.ShapeDtypeStruct(w_down.shape[1:], BF16)]
    vmem = 2 * (d * CAST_UP_COLS + CAST_DOWN_ROWS * d) * (4 + 2)
    return in_specs, out_specs, out_shape, vmem


def _cast_blocks(wu_in, wd_in, wu_out, wd_out):
    wu_out[...] = wu_in[...].astype(BF16)
    wd_out[...] = wd_in[...].astype(BF16)


def _norm_matmul_kernel(x_ref, g_ref, w_ref, o_ref, xn_ref):
    @pl.when(pl.program_id(1) == 0)
    def _():
        xn_ref[...] = _rms(x_ref[...], g_ref[...]).astype(BF16)

    o_ref[...] = jnp.dot(xn_ref[...], w_ref[...], preferred_element_type=F32).astype(o_ref.dtype)


def _norm_matmul_side_kernel(x_ref, g_ref, w_ref, ws_ref, o_ref, side_ref, xn_ref):
    @pl.when(pl.program_id(1) == 0)
    def _():
        xn = _rms(x_ref[...], g_ref[...]).astype(BF16)
        xn_ref[...] = xn
        side_ref[...] = jnp.dot(xn, ws_ref[...], preferred_element_type=F32)

    o_ref[...] = jnp.dot(xn_ref[...], w_ref[...], preferred_element_type=F32).astype(o_ref.dtype)


def _norm_matmul(x2, g, w, layer, w_side=None, name="norm_matmul"):
    m, d = x2.shape
    n = w.shape[2]
    tm, tn = min(PROJ_TM, m), min(PROJ_TN, n)
    vmem = (2 * tm * d * 4 + tm * d * 2 + 2 * d * tn * 2 + 2 * tm * tn * 2
            + 2 * tm * tn * 4 + tm * d * 4)
    in_specs = [
        pl.BlockSpec((tm, d), lambda i, j: (i, 0)),
        _layer_resident((1, d), layer),
        pl.BlockSpec((None, d, tn), lambda i, j: (layer, 0, j)),
    ]
    out_specs = pl.BlockSpec((tm, tn), lambda i, j: (i, j))
    out_shape = jax.ShapeDtypeStruct((m, n), BF16)
    args = [x2, g, w]
    kern = _norm_matmul_kernel
    if w_side is not None:
        ns = w_side.shape[2]
        in_specs.append(_layer_resident((d, ns), layer))
        out_specs = [out_specs, pl.BlockSpec((tm, ns), lambda i, j: (i, 0))]
        out_shape = [out_shape, jax.ShapeDtypeStruct((m, ns), F32)]
        args.append(w_side)
        kern = _norm_matmul_side_kernel
        vmem += d * ns * 2 + 2 * tm * ns * 4
    return pl.pallas_call(
        kern,
        grid=(m // tm, n // tn),
        in_specs=in_specs,
        out_specs=out_specs,
        out_shape=out_shape,
        scratch_shapes=[pltpu.VMEM((tm, d), BF16)],
        compiler_params=_params(("arbitrary", "arbitrary"), vmem),
        name=name,
    )(*args)


def _gelu_tanh(x):
    c = math.sqrt(2.0 / math.pi)
    return 0.5 * x * (1.0 + jnp.tanh(c * (x + 0.044715 * (x * x * x))))


def _softplus(x):
    return jnp.maximum(x, 0.0) + jnp.log1p(jnp.exp(-jnp.abs(x)))


LRU_SEG = 32
LRU_PITCH = LRU_SEG + 4


def _lru_kernel(xl_ref, gl_ref, cw_ref, cb_ref, wai_ref, ba_ref, bi_ref, lam_ref, *rest, n_cast):
    cast_in, y_ref = rest[:2 * n_cast], rest[2 * n_cast]
    cast_out, scratch = rest[2 * n_cast + 1:4 * n_cast + 1], rest[4 * n_cast + 1:]
    for k in range(n_cast):
        _cast_blocks(cast_in[2 * k], cast_in[2 * k + 1], cast_out[2 * k], cast_out[2 * k + 1])
    tail_ref, h_ref, sx_ref, sxc_ref, sr_ref, si_ref, sh_ref = scratch
    t_rows = xl_ref.shape[0]
    n_seg = t_rows // LRU_SEG
    assert n_seg == SUBLANES

    @pl.when(pl.program_id(1) == 0)
    def _():
        tail_ref[...] = jnp.zeros_like(tail_ref)
        h_ref[...] = jnp.zeros_like(h_ref)

    sub = lax.broadcasted_iota(jnp.int32, (SUBLANES, LRU_BW), 0)

    def to_segments(ref, v):
        for sg in range(n_seg):
            ref[sg * LRU_PITCH:sg * LRU_PITCH + LRU_SEG, :] = v[sg * LRU_SEG:(sg + 1) * LRU_SEG]
        return [ref[pl.ds(j, SUBLANES, stride=LRU_PITCH), :] for j in range(LRU_SEG)]

    def to_rows(ref, vs):
        for j, v in enumerate(vs):
            ref[pl.ds(j, SUBLANES, stride=LRU_PITCH), :] = v
        return jnp.concatenate([ref[sg * LRU_PITCH:sg * LRU_PITCH + LRU_SEG, :]
                                for sg in range(n_seg)], axis=0)

    def shift_in(v, first):
        return jnp.where(sub == 0, jnp.broadcast_to(first, v.shape), pltpu.roll(v, 1, 0))

    for n in range(LRU_BLOCKS):
        cs = slice(n * LRU_BW, (n + 1) * LRU_BW)
        x = xl_ref[:, cs].astype(F32)
        xm = to_segments(sx_ref, x)
        tail = tail_ref[:, cs]
        before = {k: shift_in(xm[LRU_SEG - k], tail[SUBLANES - k:SUBLANES - k + 1])
                  for k in range(1, CONV_W)}
        tail_ref[:, cs] = x[t_rows - SUBLANES:]
        xcm = []
        for j in range(LRU_SEG):
            acc = xm[j] * cw_ref[CONV_W - 1:CONV_W, cs] + cb_ref[:, cs]
            for k in range(1, CONV_W):
                src = xm[j - k] if j >= k else before[k - j]
                acc = acc + src * cw_ref[CONV_W - 1 - k:CONV_W - k, cs]
            xcm.append(acc)

        xc_rows = to_rows(sxc_ref, xcm)
        pre = jnp.dot(xc_rows.astype(BF16), wai_ref[n], preferred_element_type=F32)
        rm = to_segments(sr_ref, pre[:, :LRU_BW])
        im = to_segments(si_ref, pre[:, LRU_BW:])
        decay = -LRU_C * _softplus(-lam_ref[:, cs])

        hloc, aprod = [], []
        h = None
        for j in range(LRU_SEG):
            r = jax.nn.sigmoid(rm[j] + ba_ref[:, cs])
            i = jax.nn.sigmoid(im[j] + bi_ref[:, cs])
            log_a = decay * r
            a = jnp.exp(log_a)
            u = jnp.sqrt(-jnp.tanh(log_a) * (1.0 + a * a)) * (i * xcm[j])
            h = u if j == 0 else a * h + u
            ap = a if j == 0 else a * aprod[-1]
            hloc.append(h)
            aprod.append(ap)

        a_end, h_end = aprod[-1], hloc[-1]
        for dd in (1, 2, 4):
            keep = sub >= dd
            a_s = jnp.where(keep, pltpu.roll(a_end, dd, 0), 1.0)
            h_s = jnp.where(keep, pltpu.roll(h_end, dd, 0), 0.0)
            h_end = a_end * h_s + h_end
            a_end = a_end * a_s
        seg_state = h_end + a_end * h_ref[:, cs]
        state_in = shift_in(seg_state, h_ref[:, cs])
        h_ref[:, cs] = seg_state[SUBLANES - 1:SUBLANES]
        hm = [hloc[j] + aprod[j] * state_in for j in range(LRU_SEG)]

        h_rows = to_rows(sh_ref, hm)
        y = h_rows * _gelu_tanh(gl_ref[:, cs].astype(F32))
        y_ref[:, cs] = y.astype(y_ref.dtype)


def _lru(proj, conv_w, conv_b, w_ai, b_a, b_i, lam, layer, batch, seq, casts=()):
    t = min(LRU_T, seq)
    nt = seq // t
    w = W_LRU
    vmem = 2 * 3 * t * w * 2 + LRU_BLOCKS * LRU_BW * 2 * LRU_BW * 2 + 64 * t * LRU_BW * 4
    out_specs = [pl.BlockSpec((t, w), lambda b, s: (b * nt + s, 0))]
    out_shape = [jax.ShapeDtypeStruct((batch * seq, w), BF16)]
    args = [proj, proj, conv_w, conv_b, w_ai, b_a, b_i, lam]
    cast_in = []
    for w_up, w_down, cast_layer in casts:
        assert batch * nt >= CAST_BLOCKS
        c_in, c_out, c_shape, c_vmem = _cast_payload(w_up, w_down, cast_layer,
                                                     lambda b, s: b * nt + s)
        cast_in += c_in
        out_specs += c_out
        out_shape += c_shape
        args += [w_up, w_down]
        vmem += c_vmem
    return pl.pallas_call(
        functools.partial(_lru_kernel, n_cast=len(casts)),
        grid=(batch, nt),
        in_specs=[
            pl.BlockSpec((t, w), lambda b, s: (b * nt + s, COL_XLRU)),
            pl.BlockSpec((t, w), lambda b, s: (b * nt + s, COL_GLRU)),
            _layer_resident((CONV_W, w), layer),
            _layer_resident((1, w), layer),
            _layer_resident((LRU_BLOCKS, LRU_BW, 2 * LRU_BW), layer),
            _layer_resident((1, w), layer),
            _layer_resident((1, w), layer),
            _layer_resident((1, w), layer),
        ] + cast_in,
        out_specs=out_specs,
        out_shape=out_shape,
        scratch_shapes=[pltpu.VMEM((SUBLANES, w), F32), pltpu.VMEM((1, w), F32)]
        + [pltpu.VMEM((t // LRU_SEG * LRU_PITCH, LRU_BW), F32)] * 5,
        compiler_params=_params(("arbitrary", "arbitrary"), vmem + (8 << 20)),
        name="rglru",
    )(*args)


def _log_sigmoid(x):
    return jnp.minimum(x, 0.0) - jnp.log1p(jnp.exp(-jnp.abs(x)))


def _gla_kernel(q_ref, k_ref, v_ref, r_ref, gk_ref, wgk2_ref, bgk_ref, ng_ref, y_ref, st_ref,
                *cum_refs):
    t_rows = q_ref.shape[0]
    c = GLA_CHUNK
    hk = GLA_H * GLA_DK

    @pl.when(pl.program_id(1) == 0)
    def _():
        st_ref[...] = jnp.zeros_like(st_ref)

    z = jnp.dot(gk_ref[...], wgk2_ref[...], preferred_element_type=F32,
                precision=lax.Precision.HIGHEST) + bgk_ref[...]
    la = _log_sigmoid(z) * (1.0 / GLA_NORMALIZER)
    seg = t_rows // SUBLANES
    pitch = seg + 4
    assert seg in (c, c // 2)
    sub = lax.broadcasted_iota(jnp.int32, (SUBLANES, LANES), 0)
    bcum_blocks = []
    for b, ref in enumerate(cum_refs):
        for sg in range(SUBLANES):
            ref[sg * pitch:sg * pitch + seg, :] = la[sg * seg:(sg + 1) * seg,
                                                     b * LANES:(b + 1) * LANES]
        run, sums = None, []
        for jj in range(seg):
            v = ref[pl.ds(jj, SUBLANES, stride=pitch), :]
            run = v if jj == 0 else run + v
            sums.append(run)
        if seg < c:
            carry_in = jnp.where(sub % 2 == 1, pltpu.roll(sums[-1], 1, 0), 0.0)
            sums = [v + carry_in for v in sums]
        for jj, v in enumerate(sums):
            ref[pl.ds(jj, SUBLANES, stride=pitch), :] = v
        bcum_blocks.append(jnp.concatenate(
            [ref[sg * pitch:sg * pitch + seg, :] for sg in range(SUBLANES)], axis=0))
    bcum = jnp.concatenate(bcum_blocks, axis=1)

    tril = (lax.broadcasted_iota(jnp.int32, (c, c), 0)
            >= lax.broadcasted_iota(jnp.int32, (c, c), 1))
    nt_dims = (((1,), (1,)), ((), ()))
    tn_dims = (((0,), (0,)), ((), ()))
    n_chunks = t_rows // c

    qe_c, g_c, o_intra, upd = [], [], [], []
    for ci in range(n_chunks):
        rows = slice(ci * c, (ci + 1) * c)
        bc = bcum[rows]
        b_last = bcum[(ci + 1) * c - 1:(ci + 1) * c]
        q = q_ref[rows, :].astype(F32) * (GLA_DK ** -0.5)
        k = k_ref[rows, :].astype(F32)
        qe = (q * jnp.exp(bc)).astype(BF16)
        ke = (k * jnp.exp(-bc)).astype(BF16)
        kd = (k * jnp.exp(b_last - bc)).astype(BF16)
        qe_c.append(qe)
        g_c.append(jnp.exp(b_last))
        o_h, upd_h = [], []
        for h in range(GLA_H):
            ks = slice(h * GLA_DK, (h + 1) * GLA_DK)
            v_h = v_ref[rows, h * GLA_DV:(h + 1) * GLA_DV]
            s = lax.dot_general(qe[:, ks], ke[:, ks], nt_dims, preferred_element_type=F32)
            s = jnp.where(tril, s, 0.0).astype(BF16)
            o_h.append(jnp.dot(s, v_h, preferred_element_type=F32))
            upd_h.append(lax.dot_general(v_h, kd[:, ks], tn_dims,
                                         preferred_element_type=F32))
        o_intra.append(o_h)
        upd.append(upd_h)

    for h in range(GLA_H):
        ks = slice(h * GLA_DK, (h + 1) * GLA_DK)
        vs = slice(h * GLA_DV, (h + 1) * GLA_DV)
        st = st_ref[h]
        for ci in range(n_chunks):
            rows = slice(ci * c, (ci + 1) * c)
            o = o_intra[ci][h] + lax.dot_general(qe_c[ci][:, ks], st.astype(BF16), nt_dims,
                                                 preferred_element_type=F32)
            st = st * g_c[ci][:, ks] + upd[ci][h]
            o = o * lax.rsqrt(jnp.mean(o * o, axis=-1, keepdims=True) + EPS) * ng_ref[...]
            o = o * _silu(r_ref[rows, vs].astype(F32))
            y_ref[rows, vs] = o.astype(y_ref.dtype)
        st_ref[h] = st


def _gla(proj, gk, w_gk2_p, b_gk, norm_g, layer, batch, seq):
    t = min(GLA_T, seq)
    nt = seq // t
    hk, hv = GLA_H * GLA_DK, GLA_H * GLA_DV
    vmem = (2 * (2 * t * hk * 2 + 3 * t * hv * 2 + t * GK_PAD * 4) + t * hk * 4
            + GLA_H * GLA_DV * GLA_DK * 4 + 8 * t * hk * 4)
    return pl.pallas_call(
        _gla_kernel,
        grid=(batch, nt),
        in_specs=[
            pl.BlockSpec((t, hk), lambda b, s: (b * nt + s, COL_Q)),
            pl.BlockSpec((t, hk), lambda b, s: (b * nt + s, COL_K)),
            pl.BlockSpec((t, hv), lambda b, s: (b * nt + s, COL_V)),
            pl.BlockSpec((t, hv), lambda b, s: (b * nt + s, COL_R)),
            pl.BlockSpec((t, GK_PAD), lambda b, s: (b * nt + s, 0)),
            _layer_resident((GK_PAD, hk), layer),
            _layer_resident((1, hk), layer),
            _layer_resident((1, GLA_DV), layer),
        ],
        out_specs=pl.BlockSpec((t, hv), lambda b, s: (b * nt + s, 0)),
        out_shape=jax.ShapeDtypeStruct((batch * seq, hv), BF16),
        scratch_shapes=[pltpu.VMEM((GLA_H, GLA_DV, GLA_DK), F32)]
        + [pltpu.VMEM((t + 4 * SUBLANES, LANES), F32)] * (hk // LANES),
        compiler_params=_params(("arbitrary", "arbitrary"), vmem + (8 << 20)),
        name="gla",
    )(proj, proj, proj, proj, gk, w_gk2_p, b_gk, norm_g)


def _merge_kernel(x_ref, ya_ref, yb_ref, gl_ref, bg_ref, wb_ref, wo_ref, post_g_ref, o_ref):
    d = x_ref.shape[1]
    z_a = jnp.dot(ya_ref[...], wb_ref[:W_LRU, :], preferred_element_type=F32)
    z_b = jnp.dot(yb_ref[...], wb_ref[W_LRU:, :], preferred_element_type=F32)
    g_a = jax.nn.sigmoid(gl_ref[:, :d].astype(F32) + bg_ref[:, :d])
    g_b = jax.nn.sigmoid(gl_ref[:, d:].astype(F32) + bg_ref[:, d:])
    merged = (g_a * z_a + g_b * z_b).astype(BF16)
    h = jnp.dot(merged, wo_ref[...], preferred_element_type=F32)
    o_ref[...] = x_ref[...] + _rms(h, post_g_ref[...])


def _merge(x2, y_a, y_b, proj, b_gate, w_branch, w_out, post_g, layer):
    m, d = x2.shape
    tm = min(MERGE_TM, m)
    wb_rows = w_branch.shape[1]
    vmem = (2 * 2 * tm * d * 4 + 2 * 2 * tm * W_LRU * 2 + 2 * tm * 2 * d * 2
            + wb_rows * d * 2 + d * d * 2 + 6 * tm * d * 4)
    return pl.pallas_call(
        _merge_kernel,
        grid=(m // tm,),
        in_specs=[
            pl.BlockSpec((tm, d), lambda i: (i, 0)),
            pl.BlockSpec((tm, W_LRU), lambda i: (i, 0)),
            pl.BlockSpec((tm, GLA_H * GLA_DV), lambda i: (i, 0)),
            pl.BlockSpec((tm, N_BRANCH * d), lambda i: (i, COL_GATE)),
            _layer_resident((1, N_BRANCH * d), layer),
            _layer_resident((wb_rows, d), layer),
            _layer_resident((d, d), layer),
            _layer_resident((1, d), layer),
        ],
        out_specs=pl.BlockSpec((tm, d), lambda i: (i, 0)),
        out_shape=jax.ShapeDtypeStruct((m, d), F32),
        compiler_params=_params(("arbitrary",), vmem),
        name="merge",
    )(x2, y_a, y_b, proj, b_gate, w_branch, w_out, post_g)


def _xattn_kernel(x_ref, pre_g_ref, wq_ref, kk_ref, vv_ref, wo_ref, post_g_ref, o_ref):
    xn = _rms(x_ref[...], pre_g_ref[...]).astype(BF16)
    q = (jnp.dot(xn, wq_ref[...], preferred_element_type=F32) * (XA_DH ** -0.5)).astype(BF16)
    nt_dims = (((1,), (1,)), ((), ()))
    outs = []
    for h in range(XA_H):
        hs = slice(h * XA_DH, (h + 1) * XA_DH)
        s = lax.dot_general(q[:, hs], kk_ref[:, hs], nt_dims,
                            preferred_element_type=F32)
        p = jnp.exp(s - jnp.max(s, axis=-1, keepdims=True))
        pv = jnp.dot(p.astype(BF16), vv_ref[:, hs], preferred_element_type=F32)
        outs.append(pv / jnp.sum(p, axis=-1, keepdims=True))
    o = jnp.concatenate(outs, axis=1).astype(BF16)
    h_out = jnp.dot(o, wo_ref[...], preferred_element_type=F32)
    o_ref[...] = x_ref[...] + _rms(h_out, post_g_ref[...])


def _xattn(x2, pre_g, w_q, kv, w_o, post_g, layer, batch, seq, mem_len):
    m, d = x2.shape
    tm = min(XA_TM, seq)
    nt = seq // tm
    hd = XA_H * XA_DH
    vmem = (2 * 2 * tm * d * 4 + 2 * d * hd * 2 + 2 * 2 * mem_len * hd * 2
            + 4 * tm * d * 4 + 8 * tm * mem_len * 4)
    return pl.pallas_call(
        _xattn_kernel,
        grid=(batch, nt),
        in_specs=[
            pl.BlockSpec((tm, d), lambda b, s: (b * nt + s, 0)),
            _layer_resident((1, d), layer),
            _layer_resident((d, hd), layer),
            pl.BlockSpec((mem_len, hd), lambda b, s: (b, 0)),
            pl.BlockSpec((mem_len, hd), lambda b, s: (b, 1)),
            _layer_resident((hd, d), layer),
            _layer_resident((1, d), layer),
        ],
        out_specs=pl.BlockSpec((tm, d), lambda b, s: (b * nt + s, 0)),
        out_shape=jax.ShapeDtypeStruct((m, d), F32),
        compiler_params=_params(("arbitrary", "arbitrary"), vmem),
        name="xattn",
    )(x2, pre_g, w_q, kv, kv, w_o, post_g)


def _pack_w_in(w_in):
    n_head = 2 * W_LRU + 2 * GLA_H * GLA_DK + 2 * GLA_H * GLA_DV
    head = w_in[..., :n_head]
    gate = w_in[..., n_head + GLA_RANK:]
    w_main = jnp.concatenate([gate, head], axis=-1).astype(BF16)
    w_gk = w_in[..., n_head:n_head + GK_PAD].astype(BF16)
    return w_main, w_gk


def _rows(v):
    return v.reshape(v.shape[0], 1, -1)


def kernel(x, mem, ffn1_pre_g, ffn1_post_g, ffn1_w_up, ffn1_w_down, mix_pre_g, mix_post_g, w_in,
           conv_w, conv_b, lru_w_a, lru_b_a, lru_w_i, lru_b_i, lru_lambda, gla_w_gk2, gla_b_gk,
           gla_norm_g, b_gate, w_branch, w_out, xa_pre_g, xa_post_g, mem_g, xa_w_q, xa_w_kv,
           xa_w_o, ffn2_pre_g, ffn2_post_g, ffn2_w_up, ffn2_w_down):
    batch, seq, d = x.shape
    mem_len = mem.shape[1]
    depth = ffn1_w_up.shape[0]
    x2 = x.reshape(batch * seq, d)
    mem2 = mem.reshape(batch * mem_len, d)

    ffn1_up, ffn1_down = ffn1_w_up[0].astype(BF16), ffn1_w_down[0].astype(BF16)
    w_main, w_gk = _pack_w_in(w_in)
    w_ai = jnp.concatenate([lru_w_a, lru_w_i], axis=-1).astype(BF16)
    w_gk2_p = jnp.pad(gla_w_gk2, ((0, 0), (0, GK_PAD - GLA_RANK), (0, 0)))
    wb, wo = w_branch.astype(BF16), w_out.astype(BF16)
    xa_q, xa_kv, xa_o = xa_w_q.astype(BF16), xa_w_kv.astype(BF16), xa_w_o.astype(BF16)
    ffn1_pre, ffn1_post = _rows(ffn1_pre_g), _rows(ffn1_post_g)
    ffn2_pre, ffn2_post = _rows(ffn2_pre_g), _rows(ffn2_post_g)
    mix_pre, mix_post = _rows(mix_pre_g), _rows(mix_post_g)
    xa_pre, xa_post, mem_gain = _rows(xa_pre_g), _rows(xa_post_g), _rows(mem_g)
    conv_bias, b_a, b_i, lam = _rows(conv_b), _rows(lru_b_a), _rows(lru_b_i), _rows(lru_lambda)
    b_gk, norm_g, b_gate_r = _rows(gla_b_gk), _rows(gla_norm_g), _rows(b_gate)

    for l in range(depth):
        x2 = _ffn(x2, ffn1_pre, ffn1_up, ffn1_down, ffn1_post, l)

        proj, gk = _norm_matmul(x2, mix_pre, w_main, l, w_side=w_gk, name="mix_in_proj")
        casts = [(ffn2_w_up, ffn2_w_down, l)]
        if l + 1 < depth:
            casts.append((ffn1_w_up, ffn1_w_down, l + 1))
        y_a, ffn2_up, ffn2_down, *next_ffn1 = _lru(proj, conv_w, conv_bias, w_ai, b_a, b_i, lam,
                                                   l, batch, seq, casts=casts)
        if next_ffn1:
            ffn1_up, ffn1_down = next_ffn1
        y_b = _gla(proj, gk, w_gk2_p, b_gk, norm_g, l, batch, seq)
        x2 = _merge(x2, y_a, y_b, proj, b_gate_r, wb, wo, mix_post, l)

        kv = _norm_matmul(mem2, mem_gain, xa_kv, l, name="mem_kv_proj")
        x2 = _xattn(x2, xa_pre, xa_q, kv, xa_o, xa_post, l, batch, seq, mem_len)

        x2 = _ffn(x2, ffn2_pre, ffn2_up, ffn2_down, ffn2_post, l)

    return x2.reshape(batch, seq, d)
```

```python
import functools
import math

import jax
import jax.numpy as jnp
from jax import lax
from jax.experimental import pallas as pl
from jax.experimental.pallas import tpu as pltpu

F32 = jnp.float32
BF16 = jnp.bfloat16

D_MODEL = 2048
D_FF = 5504
FFN_RES_SCALE = 0.5
W_LRU = D_MODEL // 2
LRU_BLOCKS = 8
LRU_BW = W_LRU // LRU_BLOCKS
CONV_W = 4
LRU_C = 8.0
GLA_H = 4
GLA_DK = 128
GLA_DV = 256
GLA_RANK = 16
GLA_NORMALIZER = 16.0
GLA_CHUNK = 64
XA_H = 4
XA_DH = 128
N_BRANCH = 2
EPS = 1e-6

LANES = 128
SUBLANES = 8
V7X_VMEM_BYTES = 64 * 1024 * 1024
VMEM_LIMIT_CAP = V7X_VMEM_BYTES - 6 * 1024 * 1024

FFN_TF = 512
FFN_NF = -(-D_FF // FFN_TF)
FFN_TM = 1024
FFN_SLABS = 8
PROJ_TM = 1024
PROJ_TN = 2304
N_PROJ = N_BRANCH * D_MODEL + 2 * W_LRU + 2 * GLA_H * GLA_DV + 2 * GLA_H * GLA_DK
GK_PAD = LANES
LRU_T = 256
GLA_T = 512
MERGE_TM = 512
XA_TM = 512

COL_GATE = 0
COL_XLRU = (N_BRANCH * D_MODEL) // W_LRU
COL_GLRU = COL_XLRU + 1
COL_Q = (N_BRANCH * D_MODEL + 2 * W_LRU) // (GLA_H * GLA_DK)
COL_K = COL_Q + 1
COL_V = (N_BRANCH * D_MODEL + 2 * W_LRU + 2 * GLA_H * GLA_DK) // (GLA_H * GLA_DV)
COL_R = COL_V + 1


def _params(semantics, vmem_bytes):
    return pltpu.CompilerParams(dimension_semantics=semantics,
                                vmem_limit_bytes=int(min(VMEM_LIMIT_CAP, vmem_bytes)))


def _layer_resident(tail, layer):
    zeros = (0,) * len(tail)
    return pl.BlockSpec((None,) + tuple(tail), lambda *_: (layer,) + zeros,
                        pipeline_mode=pl.Buffered(1))


def _rms(x, g):
    ms = jnp.mean(x * x, axis=-1, keepdims=True)
    return x * lax.rsqrt(ms + EPS) * g


def _silu(x):
    return x * jax.nn.sigmoid(x)


def _ffn_window_start(j, base=0):
    return LANES * (base // LANES + jnp.minimum(j * (FFN_TF // LANES), (D_FF - FFN_TF) // LANES))


def _ffn_kernel(xnext_ref, xprev_ref, pre_g_ref, wg_ref, wu_ref, wd_ref, post_g_ref, o_ref,
                xn_even, xn_odd, acc_even, acc_odd, *, n_tiles):
    r = pl.program_id(0)
    j = pl.program_id(1)
    nf = pl.num_programs(1)
    slab = xnext_ref.shape[0]
    n_slabs = xn_even.shape[0] // slab
    row0 = pl.multiple_of(jnp.minimum(j, n_slabs - 1) * slab, slab)
    group = 2 * SUBLANES

    def pre_norm(xn_dst):
        for g0 in range(0, slab, group):
            y = _rms(xnext_ref[g0:g0 + group, :], pre_g_ref[...])
            xn_dst[pl.ds(row0 + g0, group), :] = y.astype(BF16)

    def matmul_step(xn_src, acc):
        xn = xn_src[...]
        gate = jnp.dot(xn, wg_ref[...], preferred_element_type=F32)
        up = jnp.dot(xn, wu_ref[...], preferred_element_type=F32)
        act = _silu(gate) * up
        covered = jnp.where(j == nf - 1, nf * FFN_TF - D_FF, 0)
        col = lax.broadcasted_iota(jnp.int32, act.shape, 1)
        act = jnp.where(col >= covered, act, 0.0).astype(BF16)
        prev = jnp.where(j == 0, 0.0, acc[...])
        acc[...] = prev + jnp.dot(act, wd_ref[...], preferred_element_type=F32)

    def post_norm(acc_src):
        for g0 in range(0, slab, group):
            h = acc_src[pl.ds(row0 + g0, group), :]
            o_ref[g0:g0 + group, :] = (xprev_ref[g0:g0 + group, :]
                                       + FFN_RES_SCALE * _rms(h, post_g_ref[...]))

    @pl.when(r == 0)
    def _():
        @pl.when(j == 0)
        def _():
            acc_even[...] = jnp.zeros_like(acc_even)
            acc_odd[...] = jnp.zeros_like(acc_odd)

        pre_norm(xn_even)

    steady = (r >= 1) & (r <= n_tiles)

    @pl.when(steady & (r % 2 == 1))
    def _():
        pre_norm(xn_odd)
        post_norm(acc_odd)
        matmul_step(xn_even, acc_even)

    @pl.when(steady & (r % 2 == 0))
    def _():
        pre_norm(xn_even)
        post_norm(acc_even)
        matmul_step(xn_odd, acc_odd)

    @pl.when(r == n_tiles + 1)
    def _():
        post_norm(acc_odd if (n_tiles - 1) % 2 else acc_even)


def _ffn(x2, pre_g, w_up, w_down, post_g, layer):
    m, d = x2.shape
    tm, tf = min(FFN_TM, m), FFN_TF
    n_tiles = m // tm
    slab = tm // FFN_SLABS
    vmem = (2 * tm * d * (2 + 4)
            + 2 * 3 * d * tf * 2
            + 3 * 2 * slab * d * 4
            + 2 * tm * d * 4 + 4 * tm * tf * 4)

    def next_slab(r, j):
        return (jnp.minimum(r, n_tiles - 1) * FFN_SLABS + jnp.minimum(j, FFN_SLABS - 1), 0)

    def prev_slab(r, j):
        tile = jnp.minimum(r - 2, n_tiles - 1)
        return (jnp.where(r < 2, 0, tile * FFN_SLABS + jnp.minimum(j, FFN_SLABS - 1)), 0)

    def window(r, j):
        return jnp.where(r == 0, 0, jnp.where(r == n_tiles + 1, FFN_NF - 1, j))

    return pl.pallas_call(
        functools.partial(_ffn_kernel, n_tiles=n_tiles),
        grid=(n_tiles + 2, FFN_NF),
        in_specs=[
            pl.BlockSpec((slab, d), next_slab),
            pl.BlockSpec((slab, d), prev_slab),
            _layer_resident((1, d), layer),
            pl.BlockSpec((pl.Element(d), pl.Element(tf)),
                         lambda r, j: (0, _ffn_window_start(window(r, j)))),
            pl.BlockSpec((pl.Element(d), pl.Element(tf)),
                         lambda r, j: (0, _ffn_window_start(window(r, j), base=D_FF))),
            pl.BlockSpec((pl.Element(tf), pl.Element(d)),
                         lambda r, j: (_ffn_window_start(window(r, j)), 0)),
            _layer_resident((1, d), layer),
        ],
        out_specs=pl.BlockSpec((slab, d), prev_slab),
        out_shape=jax.ShapeDtypeStruct((m, d), F32),
        scratch_shapes=[pltpu.VMEM((tm, d), BF16), pltpu.VMEM((tm, d), BF16),
                        pltpu.VMEM((tm, d), F32), pltpu.VMEM((tm, d), F32)],
        compiler_params=_params(("arbitrary", "arbitrary"), vmem),
        name="ffn",
    )(x2, x2, pre_g, w_up, w_up, w_down, post_g)


CAST_UP_BLOCKS = 32
CAST_UP_ROWS = D_MODEL // CAST_UP_BLOCKS
CAST_DOWN_BLOCKS = D_MODEL // LANES


def _cast_payload(w_up, w_down, layer, step_of):
    n_up, n_down = w_up.shape[2], w_down.shape[1]

    def up_blk(*idx):
        return jnp.minimum(step_of(*idx), CAST_UP_BLOCKS - 1)

    def down_blk(*idx):
        return jnp.minimum(step_of(*idx), CAST_DOWN_BLOCKS - 1)

    in_specs = [pl.BlockSpec((None, CAST_UP_ROWS, n_up), lambda *idx: (layer, up_blk(*idx), 0)),
                pl.BlockSpec((None, n_down, LANES), lambda *idx: (layer, 0, down_blk(*idx)))]
    out_specs = [pl.BlockSpec((CAST_UP_ROWS, n_up), lambda *idx: (up_blk(*idx), 0)),
                 pl.BlockSpec((n_down, LANES), lambda *idx: (0, down_blk(*idx)))]
    out_shape = [jax.ShapeDtypeStruct(w_up.shape[1:], BF16),
                 jax.ShapeDtypeStruct(w_down.shape[1:], BF16)]
    vmem = 2 * (CAST_UP_ROWS * n_up + n_down * LANES) * (4 + 2)
    return in_specs, out_specs, out_shape, vmem


def _cast_blocks(wu_in, wd_in, wu_out, wd_out):
    step = pl.program_id(0) * pl.num_programs(1) + pl.program_id(1)

    @pl.when(step < CAST_UP_BLOCKS)
    def _():
        wu_out[...] = wu_in[...].astype(BF16)

    @pl.when(step < CAST_DOWN_BLOCKS)
    def _():
        wd_out[...] = wd_in[...].astype(BF16)


def _norm_matmul_kernel(x_ref, g_ref, w_ref, o_ref, xn_ref):
    @pl.when(pl.program_id(1) == 0)
    def _():
        xn_ref[...] = _rms(x_ref[...], g_ref[...]).astype(BF16)

    o_ref[...] = jnp.dot(xn_ref[...], w_ref[...], preferred_element_type=F32).astype(o_ref.dtype)


def _norm_matmul_side_kernel(x_ref, g_ref, w_ref, ws_ref, o_ref, side_ref, xn_ref):
    @pl.when(pl.program_id(1) == 0)
    def _():
        xn = _rms(x_ref[...], g_ref[...]).astype(BF16)
        xn_ref[...] = xn
        side_ref[...] = jnp.dot(xn, ws_ref[...], preferred_element_type=F32)

    o_ref[...] = jnp.dot(xn_ref[...], w_ref[...], preferred_element_type=F32).astype(o_ref.dtype)


def _norm_matmul(x2, g, w, layer, w_side=None, name="norm_matmul"):
    m, d = x2.shape
    n = w.shape[2]
    tm, tn = min(PROJ_TM, m), min(PROJ_TN, n)
    vmem = (2 * tm * d * 4 + tm * d * 2 + 2 * d * tn * 2 + 2 * tm * tn * 2
            + 2 * tm * tn * 4 + tm * d * 4)
    in_specs = [
        pl.BlockSpec((tm, d), lambda i, j: (i, 0)),
        _layer_resident((1, d), layer),
        pl.BlockSpec((None, d, tn), lambda i, j: (layer, 0, j)),
    ]
    out_specs = pl.BlockSpec((tm, tn), lambda i, j: (i, j))
    out_shape = jax.ShapeDtypeStruct((m, n), BF16)
    args = [x2, g, w]
    kern = _norm_matmul_kernel
    if w_side is not None:
        ns = w_side.shape[2]
        in_specs.append(_layer_resident((d, ns), layer))
        out_specs = [out_specs, pl.BlockSpec((tm, ns), lambda i, j: (i, 0))]
        out_shape = [out_shape, jax.ShapeDtypeStruct((m, ns), F32)]
        args.append(w_side)
        kern = _norm_matmul_side_kernel
        vmem += d * ns * 2 + 2 * tm * ns * 4
    return pl.pallas_call(
        kern,
        grid=(m // tm, n // tn),
        in_specs=in_specs,
        out_specs=out_specs,
        out_shape=out_shape,
        scratch_shapes=[pltpu.VMEM((tm, d), BF16)],
        compiler_params=_params(("arbitrary", "arbitrary"), vmem),
        name=name,
    )(*args)


def _gelu_tanh(x):
    c = math.sqrt(2.0 / math.pi)
    return 0.5 * x * (1.0 + jnp.tanh(c * (x + 0.044715 * (x * x * x))))


def _softplus(x):
    return jnp.maximum(x, 0.0) + jnp.log1p(jnp.exp(-jnp.abs(x)))


LRU_SEG = 32
LRU_PITCH = LRU_SEG + 4


def _lru_kernel(xl_ref, gl_ref, cw_ref, cb_ref, wai_ref, ba_ref, bi_ref, lam_ref, *rest, cast):
    if cast:
        wu_in, wd_in, y_ref, wu_out, wd_out = rest[:5]
        _cast_blocks(wu_in, wd_in, wu_out, wd_out)
        scratch = rest[5:]
    else:
        y_ref, scratch = rest[0], rest[1:]
    tail_ref, h_ref, sx_ref, sxc_ref, sr_ref, si_ref, sh_ref = scratch
    t_rows = xl_ref.shape[0]
    n_seg = t_rows // LRU_SEG
    assert n_seg == SUBLANES

    @pl.when(pl.program_id(1) == 0)
    def _():
        tail_ref[...] = jnp.zeros_like(tail_ref)
        h_ref[...] = jnp.zeros_like(h_ref)

    sub = lax.broadcasted_iota(jnp.int32, (SUBLANES, LRU_BW), 0)

    def to_segments(ref, v):
        for sg in range(n_seg):
            ref[sg * LRU_PITCH:sg * LRU_PITCH + LRU_SEG, :] = v[sg * LRU_SEG:(sg + 1) * LRU_SEG]
        return [ref[pl.ds(j, SUBLANES, stride=LRU_PITCH), :] for j in range(LRU_SEG)]

    def to_rows(ref, vs):
        for j, v in enumerate(vs):
            ref[pl.ds(j, SUBLANES, stride=LRU_PITCH), :] = v
        return jnp.concatenate([ref[sg * LRU_PITCH:sg * LRU_PITCH + LRU_SEG, :]
                                for sg in range(n_seg)], axis=0)

    def shift_in(v, first):
        return jnp.where(sub == 0, jnp.broadcast_to(first, v.shape), pltpu.roll(v, 1, 0))

    for n in range(LRU_BLOCKS):
        cs = slice(n * LRU_BW, (n + 1) * LRU_BW)
        x = xl_ref[:, cs].astype(F32)
        xm = to_segments(sx_ref, x)
        tail = tail_ref[:, cs]
        before = {k: shift_in(xm[LRU_SEG - k], tail[SUBLANES - k:SUBLANES - k + 1])
                  for k in range(1, CONV_W)}
        tail_ref[:, cs] = x[t_rows - SUBLANES:]
        xcm = []
        for j in range(LRU_SEG):
            acc = xm[j] * cw_ref[CONV_W - 1:CONV_W, cs] + cb_ref[:, cs]
            for k in range(1, CONV_W):
                src = xm[j - k] if j >= k else before[k - j]
                acc = acc + src * cw_ref[CONV_W - 1 - k:CONV_W - k, cs]
            xcm.append(acc)

        xc_rows = to_rows(sxc_ref, xcm)
        pre = jnp.dot(xc_rows.astype(BF16), wai_ref[n], preferred_element_type=F32)
        rm = to_segments(sr_ref, pre[:, :LRU_BW])
        im = to_segments(si_ref, pre[:, LRU_BW:])
        decay = -LRU_C * _softplus(-lam_ref[:, cs])

        hloc, aprod = [], []
        h = None
        for j in range(LRU_SEG):
            r = jax.nn.sigmoid(rm[j] + ba_ref[:, cs])
            i = jax.nn.sigmoid(im[j] + bi_ref[:, cs])
            log_a = decay * r
            a = jnp.exp(log_a)
            u = jnp.sqrt(-jnp.tanh(log_a) * (1.0 + a * a)) * (i * xcm[j])
            h = u if j == 0 else a * h + u
            ap = a if j == 0 else a * aprod[-1]
            hloc.append(h)
            aprod.append(ap)

        a_end, h_end = aprod[-1], hloc[-1]
        for dd in (1, 2, 4):
            keep = sub >= dd
            a_s = jnp.where(keep, pltpu.roll(a_end, dd, 0), 1.0)
            h_s = jnp.where(keep, pltpu.roll(h_end, dd, 0), 0.0)
            h_end = a_end * h_s + h_end
            a_end = a_end * a_s
        seg_state = h_end + a_end * h_ref[:, cs]
        state_in = shift_in(seg_state, h_ref[:, cs])
        h_ref[:, cs] = seg_state[SUBLANES - 1:SUBLANES]
        hm = [hloc[j] + aprod[j] * state_in for j in range(LRU_SEG)]

        h_rows = to_rows(sh_ref, hm)
        y = h_rows * _gelu_tanh(gl_ref[:, cs].astype(F32))
        y_ref[:, cs] = y.astype(y_ref.dtype)


def _lru(proj, conv_w, conv_b, w_ai, b_a, b_i, lam, layer, batch, seq, cast=None):
    t = min(LRU_T, seq)
    nt = seq // t
    w = W_LRU
    vmem = 2 * 3 * t * w * 2 + LRU_BLOCKS * LRU_BW * 2 * LRU_BW * 2 + 64 * t * LRU_BW * 4
    out_specs = pl.BlockSpec((t, w), lambda b, s: (b * nt + s, 0))
    out_shape = jax.ShapeDtypeStruct((batch * seq, w), BF16)
    args = [proj, proj, conv_w, conv_b, w_ai, b_a, b_i, lam]
    cast_in = []
    if cast is not None:
        assert batch * nt >= CAST_UP_BLOCKS
        cast_in, cast_out, cast_shape, cast_vmem = _cast_payload(
            cast[0], cast[1], cast[2], lambda b, s: b * nt + s)
        out_specs, out_shape = [out_specs] + cast_out, [out_shape] + cast_shape
        args += [cast[0], cast[1]]
        vmem += cast_vmem
    return pl.pallas_call(
        functools.partial(_lru_kernel, cast=cast is not None),
        grid=(batch, nt),
        in_specs=[
            pl.BlockSpec((t, w), lambda b, s: (b * nt + s, COL_XLRU)),
            pl.BlockSpec((t, w), lambda b, s: (b * nt + s, COL_GLRU)),
            _layer_resident((CONV_W, w), layer),
            _layer_resident((1, w), layer),
            _layer_resident((LRU_BLOCKS, LRU_BW, 2 * LRU_BW), layer),
            _layer_resident((1, w), layer),
            _layer_resident((1, w), layer),
            _layer_resident((1, w), layer),
        ] + cast_in,
        out_specs=out_specs,
        out_shape=out_shape,
        scratch_shapes=[pltpu.VMEM((SUBLANES, w), F32), pltpu.VMEM((1, w), F32)]
        + [pltpu.VMEM((t // LRU_SEG * LRU_PITCH, LRU_BW), F32)] * 5,
        compiler_params=_params(("arbitrary", "arbitrary"), vmem + (8 << 20)),
        name="rglru",
    )(*args)


def _log_sigmoid(x):
    return jnp.minimum(x, 0.0) - jnp.log1p(jnp.exp(-jnp.abs(x)))


def _gla_kernel(q_ref, k_ref, v_ref, r_ref, gk_ref, wgk2_ref, bgk_ref, ng_ref, *rest, cast):
    if cast:
        wu_in, wd_in, y_ref, wu_out, wd_out = rest[:5]
        _cast_blocks(wu_in, wd_in, wu_out, wd_out)
        scratch = rest[5:]
    else:
        y_ref, scratch = rest[0], rest[1:]
    st_ref, cum_refs = scratch[0], scratch[1:]
    t_rows = q_ref.shape[0]
    c = GLA_CHUNK
    hk = GLA_H * GLA_DK

    @pl.when(pl.program_id(1) == 0)
    def _():
        st_ref[...] = jnp.zeros_like(st_ref)

    z = jnp.dot(gk_ref[...], wgk2_ref[...], preferred_element_type=F32,
                precision=lax.Precision.HIGHEST) + bgk_ref[...]
    la = _log_sigmoid(z) * (1.0 / GLA_NORMALIZER)
    seg = t_rows // SUBLANES
    pitch = seg + 4
    assert seg in (c, c // 2)
    sub = lax.broadcasted_iota(jnp.int32, (SUBLANES, LANES), 0)
    bcum_blocks = []
    for b, ref in enumerate(cum_refs):
        for sg in range(SUBLANES):
            ref[sg * pitch:sg * pitch + seg, :] = la[sg * seg:(sg + 1) * seg,
                                                     b * LANES:(b + 1) * LANES]
        run, sums = None, []
        for jj in range(seg):
            v = ref[pl.ds(jj, SUBLANES, stride=pitch), :]
            run = v if jj == 0 else run + v
            sums.append(run)
        if seg < c:
            carry_in = jnp.where(sub % 2 == 1, pltpu.roll(sums[-1], 1, 0), 0.0)
            sums = [v + carry_in for v in sums]
        for jj, v in enumerate(sums):
            ref[pl.ds(jj, SUBLANES, stride=pitch), :] = v
        bcum_blocks.append(jnp.concatenate(
            [ref[sg * pitch:sg * pitch + seg, :] for sg in range(SUBLANES)], axis=0))
    bcum = jnp.concatenate(bcum_blocks, axis=1)

    tril = (lax.broadcasted_iota(jnp.int32, (c, c), 0)
            >= lax.broadcasted_iota(jnp.int32, (c, c), 1))
    nt_dims = (((1,), (1,)), ((), ()))
    tn_dims = (((0,), (0,)), ((), ()))
    n_chunks = t_rows // c

    qe_c, g_c, o_intra, upd = [], [], [], []
    for ci in range(n_chunks):
        rows = slice(ci * c, (ci + 1) * c)
        bc = bcum[rows]
        b_last = bcum[(ci + 1) * c - 1:(ci + 1) * c]
        q = q_ref[rows, :].astype(F32) * (GLA_DK ** -0.5)
        k = k_ref[rows, :].astype(F32)
        qe = (q * jnp.exp(bc)).astype(BF16)
        ke = (k * jnp.exp(-bc)).astype(BF16)
        kd = (k * jnp.exp(b_last - bc)).astype(BF16)
        qe_c.append(qe)
        g_c.append(jnp.exp(b_last))
        o_h, upd_h = [], []
        for h in range(GLA_H):
            ks = slice(h * GLA_DK, (h + 1) * GLA_DK)
            v_h = v_ref[rows, h * GLA_DV:(h + 1) * GLA_DV]
            s = lax.dot_general(qe[:, ks], ke[:, ks], nt_dims, preferred_element_type=F32)
            s = jnp.where(tril, s, 0.0).astype(BF16)
            o_h.append(jnp.dot(s, v_h, preferred_element_type=F32))
            upd_h.append(lax.dot_general(v_h, kd[:, ks], tn_dims,
                                         preferred_element_type=F32))
        o_intra.append(o_h)
        upd.append(upd_h)

    for h in range(GLA_H):
        ks = slice(h * GLA_DK, (h + 1) * GLA_DK)
        vs = slice(h * GLA_DV, (h + 1) * GLA_DV)
        st = st_ref[h]
        for ci in range(n_chunks):
            rows = slice(ci * c, (ci + 1) * c)
            o = o_intra[ci][h] + lax.dot_general(qe_c[ci][:, ks], st.astype(BF16), nt_dims,
                                                 preferred_element_type=F32)
            st = st * g_c[ci][:, ks] + upd[ci][h]
            o = o * lax.rsqrt(jnp.mean(o * o, axis=-1, keepdims=True) + EPS) * ng_ref[...]
            o = o * _silu(r_ref[rows, vs].astype(F32))
            y_ref[rows, vs] = o.astype(y_ref.dtype)
        st_ref[h] = st


def _gla(proj, gk, w_gk2_p, b_gk, norm_g, layer, batch, seq, cast=None):
    t = min(GLA_T, seq)
    nt = seq // t
    assert cast is None or batch * nt >= CAST_UP_BLOCKS
    hk, hv = GLA_H * GLA_DK, GLA_H * GLA_DV
    vmem = (2 * (2 * t * hk * 2 + 3 * t * hv * 2 + t * GK_PAD * 4) + t * hk * 4
            + GLA_H * GLA_DV * GLA_DK * 4 + 8 * t * hk * 4)
    out_specs = pl.BlockSpec((t, hv), lambda b, s: (b * nt + s, 0))
    out_shape = jax.ShapeDtypeStruct((batch * seq, hv), BF16)
    args = [proj, proj, proj, proj, gk, w_gk2_p, b_gk, norm_g]
    cast_in = []
    if cast is not None:
        cast_in, cast_out, cast_shape, cast_vmem = _cast_payload(
            cast[0], cast[1], cast[2], lambda b, s: b * nt + s)
        out_specs, out_shape = [out_specs] + cast_out, [out_shape] + cast_shape
        args += [cast[0], cast[1]]
        vmem += cast_vmem
    return pl.pallas_call(
        functools.partial(_gla_kernel, cast=cast is not None),
        grid=(batch, nt),
        in_specs=[
            pl.BlockSpec((t, hk), lambda b, s: (b * nt + s, COL_Q)),
            pl.BlockSpec((t, hk), lambda b, s: (b * nt + s, COL_K)),
            pl.BlockSpec((t, hv), lambda b, s: (b * nt + s, COL_V)),
            pl.BlockSpec((t, hv), lambda b, s: (b * nt + s, COL_R)),
            pl.BlockSpec((t, GK_PAD), lambda b, s: (b * nt + s, 0)),
            _layer_resident((GK_PAD, hk), layer),
            _layer_resident((1, hk), layer),
            _layer_resident((1, GLA_DV), layer),
        ] + cast_in,
        out_specs=out_specs,
        out_shape=out_shape,
        scratch_shapes=[pltpu.VMEM((GLA_H, GLA_DV, GLA_DK), F32)]
        + [pltpu.VMEM((t + 4 * SUBLANES, LANES), F32)] * (hk // LANES),
        compiler_params=_params(("arbitrary", "arbitrary"), vmem + (8 << 20)),
        name="gla",
    )(*args)


def _merge_kernel(x_ref, ya_ref, yb_ref, gl_ref, bg_ref, wb_ref, wo_ref, post_g_ref, o_ref):
    d = x_ref.shape[1]
    z_a = jnp.dot(ya_ref[...], wb_ref[:W_LRU, :], preferred_element_type=F32)
    z_b = jnp.dot(yb_ref[...], wb_ref[W_LRU:, :], preferred_element_type=F32)
    g_a = jax.nn.sigmoid(gl_ref[:, :d].astype(F32) + bg_ref[:, :d])
    g_b = jax.nn.sigmoid(gl_ref[:, d:].astype(F32) + bg_ref[:, d:])
    merged = (g_a * z_a + g_b * z_b).astype(BF16)
    h = jnp.dot(merged, wo_ref[...], preferred_element_type=F32)
    o_ref[...] = x_ref[...] + _rms(h, post_g_ref[...])


def _merge(x2, y_a, y_b, proj, b_gate, w_branch, w_out, post_g, layer):
    m, d = x2.shape
    tm = min(MERGE_TM, m)
    wb_rows = w_branch.shape[1]
    vmem = (2 * 2 * tm * d * 4 + 2 * 2 * tm * W_LRU * 2 + 2 * tm * 2 * d * 2
            + wb_rows * d * 2 + d * d * 2 + 6 * tm * d * 4)
    return pl.pallas_call(
        _merge_kernel,
        grid=(m // tm,),
        in_specs=[
            pl.BlockSpec((tm, d), lambda i: (i, 0)),
            pl.BlockSpec((tm, W_LRU), lambda i: (i, 0)),
            pl.BlockSpec((tm, GLA_H * GLA_DV), lambda i: (i, 0)),
            pl.BlockSpec((tm, N_BRANCH * d), lambda i: (i, COL_GATE)),
            _layer_resident((1, N_BRANCH * d), layer),
            _layer_resident((wb_rows, d), layer),
            _layer_resident((d, d), layer),
            _layer_resident((1, d), layer),
        ],
        out_specs=pl.BlockSpec((tm, d), lambda i: (i, 0)),
        out_shape=jax.ShapeDtypeStruct((m, d), F32),
        compiler_params=_params(("arbitrary",), vmem),
        name="merge",
    )(x2, y_a, y_b, proj, b_gate, w_branch, w_out, post_g)


def _xattn_kernel(x_ref, pre_g_ref, wq_ref, kk_ref, vv_ref, wo_ref, post_g_ref, o_ref):
    xn = _rms(x_ref[...], pre_g_ref[...]).astype(BF16)
    q = (jnp.dot(xn, wq_ref[...], preferred_element_type=F32) * (XA_DH ** -0.5)).astype(BF16)
    nt_dims = (((1,), (1,)), ((), ()))
    outs = []
    for h in range(XA_H):
        hs = slice(h * XA_DH, (h + 1) * XA_DH)
        s = lax.dot_general(q[:, hs], kk_ref[:, hs], nt_dims,
                            preferred_element_type=F32)
        p = jnp.exp(s - jnp.max(s, axis=-1, keepdims=True))
        pv = jnp.dot(p.astype(BF16), vv_ref[:, hs], preferred_element_type=F32)
        outs.append(pv / jnp.sum(p, axis=-1, keepdims=True))
    o = jnp.concatenate(outs, axis=1).astype(BF16)
    h_out = jnp.dot(o, wo_ref[...], preferred_element_type=F32)
    o_ref[...] = x_ref[...] + _rms(h_out, post_g_ref[...])


def _xattn(x2, pre_g, w_q, kv, w_o, post_g, layer, batch, seq, mem_len):
    m, d = x2.shape
    tm = min(XA_TM, seq)
    nt = seq // tm
    hd = XA_H * XA_DH
    vmem = (2 * 2 * tm * d * 4 + 2 * d * hd * 2 + 2 * 2 * mem_len * hd * 2
            + 4 * tm * d * 4 + 8 * tm * mem_len * 4)
    return pl.pallas_call(
        _xattn_kernel,
        grid=(batch, nt),
        in_specs=[
            pl.BlockSpec((tm, d), lambda b, s: (b * nt + s, 0)),
            _layer_resident((1, d), layer),
            _layer_resident((d, hd), layer),
            pl.BlockSpec((mem_len, hd), lambda b, s: (b, 0)),
            pl.BlockSpec((mem_len, hd), lambda b, s: (b, 1)),
            _layer_resident((hd, d), layer),
            _layer_resident((1, d), layer),
        ],
        out_specs=pl.BlockSpec((tm, d), lambda b, s: (b * nt + s, 0)),
        out_shape=jax.ShapeDtypeStruct((m, d), F32),
        compiler_params=_params(("arbitrary", "arbitrary"), vmem),
        name="xattn",
    )(x2, pre_g, w_q, kv, kv, w_o, post_g)


def _pack_w_in(w_in):
    n_head = 2 * W_LRU + 2 * GLA_H * GLA_DK + 2 * GLA_H * GLA_DV
    head = w_in[..., :n_head]
    gate = w_in[..., n_head + GLA_RANK:]
    w_main = jnp.concatenate([gate, head], axis=-1).astype(BF16)
    w_gk = w_in[..., n_head:n_head + GK_PAD].astype(BF16)
    return w_main, w_gk


def _rows(v):
    return v.reshape(v.shape[0], 1, -1)


def kernel(x, mem, ffn1_pre_g, ffn1_post_g, ffn1_w_up, ffn1_w_down, mix_pre_g, mix_post_g, w_in,
           conv_w, conv_b, lru_w_a, lru_b_a, lru_w_i, lru_b_i, lru_lambda, gla_w_gk2, gla_b_gk,
           gla_norm_g, b_gate, w_branch, w_out, xa_pre_g, xa_post_g, mem_g, xa_w_q, xa_w_kv,
           xa_w_o, ffn2_pre_g, ffn2_post_g, ffn2_w_up, ffn2_w_down):
    batch, seq, d = x.shape
    mem_len = mem.shape[1]
    depth = ffn1_w_up.shape[0]
    x2 = x.reshape(batch * seq, d)
    mem2 = mem.reshape(batch * mem_len, d)

    ffn1_up, ffn1_down = ffn1_w_up[0].astype(BF16), ffn1_w_down[0].astype(BF16)
    w_main, w_gk = _pack_w_in(w_in)
    w_ai = jnp.concatenate([lru_w_a, lru_w_i], axis=-1).astype(BF16)
    w_gk2_p = jnp.pad(gla_w_gk2, ((0, 0), (0, GK_PAD - GLA_RANK), (0, 0)))
    wb, wo = w_branch.astype(BF16), w_out.astype(BF16)
    xa_q, xa_kv, xa_o = xa_w_q.astype(BF16), xa_w_kv.astype(BF16), xa_w_o.astype(BF16)
    ffn1_pre, ffn1_post = _rows(ffn1_pre_g), _rows(ffn1_post_g)
    ffn2_pre, ffn2_post = _rows(ffn2_pre_g), _rows(ffn2_post_g)
    mix_pre, mix_post = _rows(mix_pre_g), _rows(mix_post_g)
    xa_pre, xa_post, mem_gain = _rows(xa_pre_g), _rows(xa_post_g), _rows(mem_g)
    conv_bias, b_a, b_i, lam = _rows(conv_b), _rows(lru_b_a), _rows(lru_b_i), _rows(lru_lambda)
    b_gk, norm_g, b_gate_r = _rows(gla_b_gk), _rows(gla_norm_g), _rows(b_gate)

    for l in range(depth):
        x2 = _ffn(x2, ffn1_pre, ffn1_up, ffn1_down, ffn1_post, l)

        proj, gk = _norm_matmul(x2, mix_pre, w_main, l, w_side=w_gk, name="mix_in_proj")
        y_a, ffn2_up, ffn2_down = _lru(proj, conv_w, conv_bias, w_ai, b_a, b_i, lam, l, batch, seq,
                                       cast=(ffn2_w_up, ffn2_w_down, l))
        if l + 1 < depth:
            y_b, ffn1_up, ffn1_down = _gla(proj, gk, w_gk2_p, b_gk, norm_g, l, batch, seq,
                                           cast=(ffn1_w_up, ffn1_w_down, l + 1))
        else:
            y_b = _gla(proj, gk, w_gk2_p, b_gk, norm_g, l, batch, seq)
        x2 = _merge(x2, y_a, y_b, proj, b_gate_r, wb, wo, mix_post, l)

        kv = _norm_matmul(mem2, mem_gain, xa_kv, l, name="mem_kv_proj")
        x2 = _xattn(x2, xa_pre, xa_q, kv, xa_o, xa_post, l, batch, seq, mem_len)

        x2 = _ffn(x2, ffn2_pre, ffn2_up, ffn2_down, ffn2_post, l)

    return x2.reshape(batch, seq, d)
```

```python
import functools
import math

import jax
import jax.numpy as jnp
from jax import lax
from jax.experimental import pallas as pl
from jax.experimental.pallas import tpu as pltpu

F32 = jnp.float32
BF16 = jnp.bfloat16

D_MODEL = 2048
D_FF = 5504
FFN_RES_SCALE = 0.5
W_LRU = D_MODEL // 2
LRU_BLOCKS = 8
LRU_BW = W_LRU // LRU_BLOCKS
CONV_W = 4
LRU_C = 8.0
GLA_H = 4
GLA_DK = 128
GLA_DV = 256
GLA_RANK = 16
GLA_NORMALIZER = 16.0
GLA_CHUNK = 64
XA_H = 4
XA_DH = 128
N_BRANCH = 2
EPS = 1e-6

LANES = 128
SUBLANES = 8
V7X_VMEM_BYTES = 64 * 1024 * 1024
VMEM_LIMIT_CAP = V7X_VMEM_BYTES - 6 * 1024 * 1024

FFN_TF = 512
FFN_NF = -(-D_FF // FFN_TF)
FFN_TM = 1024
FFN_SLABS = 8
PROJ_TM = 1024
PROJ_TN = 2304
N_PROJ = N_BRANCH * D_MODEL + 2 * W_LRU + 2 * GLA_H * GLA_DV + 2 * GLA_H * GLA_DK
GK_PAD = LANES
LRU_T = 256
GLA_T = 512
MERGE_TM = 512
XA_TM = 1024

COL_GATE = 0
COL_XLRU = (N_BRANCH * D_MODEL) // W_LRU
COL_GLRU = COL_XLRU + 1
COL_Q = (N_BRANCH * D_MODEL + 2 * W_LRU) // (GLA_H * GLA_DK)
COL_K = COL_Q + 1
COL_V = (N_BRANCH * D_MODEL + 2 * W_LRU + 2 * GLA_H * GLA_DK) // (GLA_H * GLA_DV)
COL_R = COL_V + 1


def _params(semantics, vmem_bytes):
    return pltpu.CompilerParams(dimension_semantics=semantics,
                                vmem_limit_bytes=int(min(VMEM_LIMIT_CAP, vmem_bytes)))


def _layer_resident(tail, layer):
    zeros = (0,) * len(tail)
    return pl.BlockSpec((None,) + tuple(tail), lambda *_: (layer,) + zeros,
                        pipeline_mode=pl.Buffered(1))


def _rms(x, g):
    ms = jnp.mean(x * x, axis=-1, keepdims=True)
    return x * lax.rsqrt(ms + EPS) * g


def _silu(x):
    return x * jax.nn.sigmoid(x)


def _ffn_window_start(j, base=0):
    return LANES * (base // LANES + jnp.minimum(j * (FFN_TF // LANES), (D_FF - FFN_TF) // LANES))


def _ffn_kernel(xnext_ref, xprev_ref, pre_g_ref, wg_ref, wu_ref, wd_ref, post_g_ref, o_ref,
                xn_even, xn_odd, acc_even, acc_odd, *, n_tiles):
    r = pl.program_id(0)
    j = pl.program_id(1)
    nf = pl.num_programs(1)
    slab = xnext_ref.shape[0]
    n_slabs = xn_even.shape[0] // slab
    row0 = pl.multiple_of(jnp.minimum(j, n_slabs - 1) * slab, slab)
    group = 2 * SUBLANES

    def pre_norm(xn_dst):
        for g0 in range(0, slab, group):
            y = _rms(xnext_ref[g0:g0 + group, :], pre_g_ref[...])
            xn_dst[pl.ds(row0 + g0, group), :] = y.astype(BF16)

    def matmul_step(xn_src, acc):
        xn = xn_src[...]
        gate = jnp.dot(xn, wg_ref[...], preferred_element_type=F32)
        up = jnp.dot(xn, wu_ref[...], preferred_element_type=F32)
        act = _silu(gate) * up
        covered = jnp.where(j == nf - 1, nf * FFN_TF - D_FF, 0)
        col = lax.broadcasted_iota(jnp.int32, act.shape, 1)
        act = jnp.where(col >= covered, act, 0.0).astype(BF16)
        prev = jnp.where(j == 0, 0.0, acc[...])
        acc[...] = prev + jnp.dot(act, wd_ref[...], preferred_element_type=F32)

    def post_norm(acc_src):
        for g0 in range(0, slab, group):
            h = acc_src[pl.ds(row0 + g0, group), :]
            o_ref[g0:g0 + group, :] = (xprev_ref[g0:g0 + group, :]
                                       + FFN_RES_SCALE * _rms(h, post_g_ref[...]))

    @pl.when(r == 0)
    def _():
        @pl.when(j == 0)
        def _():
            acc_even[...] = jnp.zeros_like(acc_even)
            acc_odd[...] = jnp.zeros_like(acc_odd)

        pre_norm(xn_even)

    steady = (r >= 1) & (r <= n_tiles)

    @pl.when(steady & (r % 2 == 1))
    def _():
        pre_norm(xn_odd)
        post_norm(acc_odd)
        matmul_step(xn_even, acc_even)

    @pl.when(steady & (r % 2 == 0))
    def _():
        pre_norm(xn_even)
        post_norm(acc_even)
        matmul_step(xn_odd, acc_odd)

    @pl.when(r == n_tiles + 1)
    def _():
        post_norm(acc_odd if (n_tiles - 1) % 2 else acc_even)


def _ffn(x2, pre_g, w_up, w_down, post_g, layer):
    m, d = x2.shape
    tm, tf = min(FFN_TM, m), FFN_TF
    n_tiles = m // tm
    slab = tm // FFN_SLABS
    vmem = (2 * tm * d * (2 + 4)
            + 2 * 3 * d * tf * 2
            + 3 * 2 * slab * d * 4
            + 2 * tm * d * 4 + 4 * tm * tf * 4)

    def next_slab(r, j):
        return (jnp.minimum(r, n_tiles - 1) * FFN_SLABS + jnp.minimum(j, FFN_SLABS - 1), 0)

    def prev_slab(r, j):
        tile = jnp.minimum(r - 2, n_tiles - 1)
        return (jnp.where(r < 2, 0, tile * FFN_SLABS + jnp.minimum(j, FFN_SLABS - 1)), 0)

    def window(r, j):
        return jnp.where(r == 0, 0, jnp.where(r == n_tiles + 1, FFN_NF - 1, j))

    return pl.pallas_call(
        functools.partial(_ffn_kernel, n_tiles=n_tiles),
        grid=(n_tiles + 2, FFN_NF),
        in_specs=[
            pl.BlockSpec((slab, d), next_slab),
            pl.BlockSpec((slab, d), prev_slab),
            _layer_resident((1, d), layer),
            pl.BlockSpec((pl.Element(d), pl.Element(tf)),
                         lambda r, j: (0, _ffn_window_start(window(r, j)))),
            pl.BlockSpec((pl.Element(d), pl.Element(tf)),
                         lambda r, j: (0, _ffn_window_start(window(r, j), base=D_FF))),
            pl.BlockSpec((pl.Element(tf), pl.Element(d)),
                         lambda r, j: (_ffn_window_start(window(r, j)), 0)),
            _layer_resident((1, d), layer),
        ],
        out_specs=pl.BlockSpec((slab, d), prev_slab),
        out_shape=jax.ShapeDtypeStruct((m, d), F32),
        scratch_shapes=[pltpu.VMEM((tm, d), BF16), pltpu.VMEM((tm, d), BF16),
                        pltpu.VMEM((tm, d), F32), pltpu.VMEM((tm, d), F32)],
        compiler_params=_params(("arbitrary", "arbitrary"), vmem),
        name="ffn",
    )(x2, x2, pre_g, w_up, w_up, w_down, post_g)


CAST_UP_BLOCKS = 32
CAST_UP_ROWS = D_MODEL // CAST_UP_BLOCKS
CAST_DOWN_BLOCKS = D_MODEL // LANES


def _cast_payload(w_up, w_down, layer, step_of):
    n_up, n_down = w_up.shape[2], w_down.shape[1]

    def up_blk(*idx):
        return jnp.minimum(step_of(*idx), CAST_UP_BLOCKS - 1)

    def down_blk(*idx):
        return jnp.minimum(step_of(*idx), CAST_DOWN_BLOCKS - 1)

    in_specs = [pl.BlockSpec((None, CAST_UP_ROWS, n_up), lambda *idx: (layer, up_blk(*idx), 0)),
                pl.BlockSpec((None, n_down, LANES), lambda *idx: (layer, 0, down_blk(*idx)))]
    out_specs = [pl.BlockSpec((CAST_UP_ROWS, n_up), lambda *idx: (up_blk(*idx), 0)),
                 pl.BlockSpec((n_down, LANES), lambda *idx: (0, down_blk(*idx)))]
    out_shape = [jax.ShapeDtypeStruct(w_up.shape[1:], BF16),
                 jax.ShapeDtypeStruct(w_down.shape[1:], BF16)]
    vmem = 2 * (CAST_UP_ROWS * n_up + n_down * LANES) * (4 + 2)
    return in_specs, out_specs, out_shape, vmem


def _cast_blocks(wu_in, wd_in, wu_out, wd_out):
    step = pl.program_id(0) * pl.num_programs(1) + pl.program_id(1)

    @pl.when(step < CAST_UP_BLOCKS)
    def _():
        wu_out[...] = wu_in[...].astype(BF16)

    @pl.when(step < CAST_DOWN_BLOCKS)
    def _():
        wd_out[...] = wd_in[...].astype(BF16)


def _norm_matmul_kernel(x_ref, g_ref, w_ref, o_ref, xn_ref):
    @pl.when(pl.program_id(1) == 0)
    def _():
        xn_ref[...] = _rms(x_ref[...], g_ref[...]).astype(BF16)

    o_ref[...] = jnp.dot(xn_ref[...], w_ref[...], preferred_element_type=F32).astype(o_ref.dtype)


def _norm_matmul_side_kernel(x_ref, g_ref, w_ref, ws_ref, o_ref, side_ref, xn_ref):
    @pl.when(pl.program_id(1) == 0)
    def _():
        xn = _rms(x_ref[...], g_ref[...]).astype(BF16)
        xn_ref[...] = xn
        side_ref[...] = jnp.dot(xn, ws_ref[...], preferred_element_type=F32)

    o_ref[...] = jnp.dot(xn_ref[...], w_ref[...], preferred_element_type=F32).astype(o_ref.dtype)


def _norm_matmul(x2, g, w, layer, w_side=None, name="norm_matmul"):
    m, d = x2.shape
    n = w.shape[2]
    tm, tn = min(PROJ_TM, m), min(PROJ_TN, n)
    vmem = (2 * tm * d * 4 + tm * d * 2 + 2 * d * tn * 2 + 2 * tm * tn * 2
            + 2 * tm * tn * 4 + tm * d * 4)
    in_specs = [
        pl.BlockSpec((tm, d), lambda i, j: (i, 0)),
        _layer_resident((1, d), layer),
        pl.BlockSpec((None, d, tn), lambda i, j: (layer, 0, j)),
    ]
    out_specs = pl.BlockSpec((tm, tn), lambda i, j: (i, j))
    out_shape = jax.ShapeDtypeStruct((m, n), BF16)
    args = [x2, g, w]
    kern = _norm_matmul_kernel
    if w_side is not None:
        ns = w_side.shape[2]
        in_specs.append(_layer_resident((d, ns), layer))
        out_specs = [out_specs, pl.BlockSpec((tm, ns), lambda i, j: (i, 0))]
        out_shape = [out_shape, jax.ShapeDtypeStruct((m, ns), F32)]
        args.append(w_side)
        kern = _norm_matmul_side_kernel
        vmem += d * ns * 2 + 2 * tm * ns * 4
    return pl.pallas_call(
        kern,
        grid=(m // tm, n // tn),
        in_specs=in_specs,
        out_specs=out_specs,
        out_shape=out_shape,
        scratch_shapes=[pltpu.VMEM((tm, d), BF16)],
        compiler_params=_params(("arbitrary", "arbitrary"), vmem),
        name=name,
    )(*args)


def _gelu_tanh(x):
    c = math.sqrt(2.0 / math.pi)
    return 0.5 * x * (1.0 + jnp.tanh(c * (x + 0.044715 * (x * x * x))))


def _softplus(x):
    return jnp.maximum(x, 0.0) + jnp.log1p(jnp.exp(-jnp.abs(x)))


LRU_SEG = 32
LRU_PITCH = LRU_SEG + 4


def _lru_kernel(xl_ref, gl_ref, cw_ref, cb_ref, wai_ref, ba_ref, bi_ref, lam_ref, *rest, cast):
    if cast:
        wu_in, wd_in, y_ref, wu_out, wd_out = rest[:5]
        _cast_blocks(wu_in, wd_in, wu_out, wd_out)
        scratch = rest[5:]
    else:
        y_ref, scratch = rest[0], rest[1:]
    tail_ref, h_ref, sx_ref, sxc_ref, sr_ref, si_ref, sh_ref = scratch
    t_rows = xl_ref.shape[0]
    n_seg = t_rows // LRU_SEG
    assert n_seg == SUBLANES

    @pl.when(pl.program_id(1) == 0)
    def _():
        tail_ref[...] = jnp.zeros_like(tail_ref)
        h_ref[...] = jnp.zeros_like(h_ref)

    sub = lax.broadcasted_iota(jnp.int32, (SUBLANES, LRU_BW), 0)

    def to_segments(ref, v):
        for sg in range(n_seg):
            ref[sg * LRU_PITCH:sg * LRU_PITCH + LRU_SEG, :] = v[sg * LRU_SEG:(sg + 1) * LRU_SEG]
        return [ref[pl.ds(j, SUBLANES, stride=LRU_PITCH), :] for j in range(LRU_SEG)]

    def to_rows(ref, vs):
        for j, v in enumerate(vs):
            ref[pl.ds(j, SUBLANES, stride=LRU_PITCH), :] = v
        return jnp.concatenate([ref[sg * LRU_PITCH:sg * LRU_PITCH + LRU_SEG, :]
                                for sg in range(n_seg)], axis=0)

    def shift_in(v, first):
        return jnp.where(sub == 0, jnp.broadcast_to(first, v.shape), pltpu.roll(v, 1, 0))

    for n in range(LRU_BLOCKS):
        cs = slice(n * LRU_BW, (n + 1) * LRU_BW)
        x = xl_ref[:, cs].astype(F32)
        xm = to_segments(sx_ref, x)
        tail = tail_ref[:, cs]
        before = {k: shift_in(xm[LRU_SEG - k], tail[SUBLANES - k:SUBLANES - k + 1])
                  for k in range(1, CONV_W)}
        tail_ref[:, cs] = x[t_rows - SUBLANES:]
        xcm = []
        for j in range(LRU_SEG):
            acc = xm[j] * cw_ref[CONV_W - 1:CONV_W, cs] + cb_ref[:, cs]
            for k in range(1, CONV_W):
                src = xm[j - k] if j >= k else before[k - j]
                acc = acc + src * cw_ref[CONV_W - 1 - k:CONV_W - k, cs]
            xcm.append(acc)

        xc_rows = to_rows(sxc_ref, xcm)
        pre = jnp.dot(xc_rows.astype(BF16), wai_ref[n], preferred_element_type=F32)
        rm = to_segments(sr_ref, pre[:, :LRU_BW])
        im = to_segments(si_ref, pre[:, LRU_BW:])
        decay = -LRU_C * _softplus(-lam_ref[:, cs])

        hloc, aprod = [], []
        h = None
        for j in range(LRU_SEG):
            r = jax.nn.sigmoid(rm[j] + ba_ref[:, cs])
            i = jax.nn.sigmoid(im[j] + bi_ref[:, cs])
            log_a = decay * r
            a = jnp.exp(log_a)
            u = jnp.sqrt(-jnp.tanh(log_a) * (1.0 + a * a)) * (i * xcm[j])
            h = u if j == 0 else a * h + u
            ap = a if j == 0 else a * aprod[-1]
            hloc.append(h)
            aprod.append(ap)

        a_end, h_end = aprod[-1], hloc[-1]
        for dd in (1, 2, 4):
            keep = sub >= dd
            a_s = jnp.where(keep, pltpu.roll(a_end, dd, 0), 1.0)
            h_s = jnp.where(keep, pltpu.roll(h_end, dd, 0), 0.0)
            h_end = a_end * h_s + h_end
            a_end = a_end * a_s
        seg_state = h_end + a_end * h_ref[:, cs]
        state_in = shift_in(seg_state, h_ref[:, cs])
        h_ref[:, cs] = seg_state[SUBLANES - 1:SUBLANES]
        hm = [hloc[j] + aprod[j] * state_in for j in range(LRU_SEG)]

        h_rows = to_rows(sh_ref, hm)
        y = h_rows * _gelu_tanh(gl_ref[:, cs].astype(F32))
        y_ref[:, cs] = y.astype(y_ref.dtype)


def _lru(proj, conv_w, conv_b, w_ai, b_a, b_i, lam, layer, batch, seq, cast=None):
    t = min(LRU_T, seq)
    nt = seq // t
    w = W_LRU
    vmem = 2 * 3 * t * w * 2 + LRU_BLOCKS * LRU_BW * 2 * LRU_BW * 2 + 64 * t * LRU_BW * 4
    out_specs = pl.BlockSpec((t, w), lambda b, s: (b * nt + s, 0))
    out_shape = jax.ShapeDtypeStruct((batch * seq, w), BF16)
    args = [proj, proj, conv_w, conv_b, w_ai, b_a, b_i, lam]
    cast_in = []
    if cast is not None:
        assert batch * nt >= CAST_UP_BLOCKS
        cast_in, cast_out, cast_shape, cast_vmem = _cast_payload(
            cast[0], cast[1], cast[2], lambda b, s: b * nt + s)
        out_specs, out_shape = [out_specs] + cast_out, [out_shape] + cast_shape
        args += [cast[0], cast[1]]
        vmem += cast_vmem
    return pl.pallas_call(
        functools.partial(_lru_kernel, cast=cast is not None),
        grid=(batch, nt),
        in_specs=[
            pl.BlockSpec((t, w), lambda b, s: (b * nt + s, COL_XLRU)),
            pl.BlockSpec((t, w), lambda b, s: (b * nt + s, COL_GLRU)),
            _layer_resident((CONV_W, w), layer),
            _layer_resident((1, w), layer),
            _layer_resident((LRU_BLOCKS, LRU_BW, 2 * LRU_BW), layer),
            _layer_resident((1, w), layer),
            _layer_resident((1, w), layer),
            _layer_resident((1, w), layer),
        ] + cast_in,
        out_specs=out_specs,
        out_shape=out_shape,
        scratch_shapes=[pltpu.VMEM((SUBLANES, w), F32), pltpu.VMEM((1, w), F32)]
        + [pltpu.VMEM((t // LRU_SEG * LRU_PITCH, LRU_BW), F32)] * 5,
        compiler_params=_params(("arbitrary", "arbitrary"), vmem + (8 << 20)),
        name="rglru",
    )(*args)


def _log_sigmoid(x):
    return jnp.minimum(x, 0.0) - jnp.log1p(jnp.exp(-jnp.abs(x)))


def _gla_kernel(q_ref, k_ref, v_ref, r_ref, gk_ref, wgk2_ref, bgk_ref, ng_ref, *rest, cast):
    if cast:
        wu_in, wd_in, y_ref, wu_out, wd_out = rest[:5]
        _cast_blocks(wu_in, wd_in, wu_out, wd_out)
        scratch = rest[5:]
    else:
        y_ref, scratch = rest[0], rest[1:]
    st_ref, cum_refs = scratch[0], scratch[1:]
    t_rows = q_ref.shape[0]
    c = GLA_CHUNK
    hk = GLA_H * GLA_DK

    @pl.when(pl.program_id(1) == 0)
    def _():
        st_ref[...] = jnp.zeros_like(st_ref)

    z = jnp.dot(gk_ref[...], wgk2_ref[...], preferred_element_type=F32,
                precision=lax.Precision.HIGHEST) + bgk_ref[...]
    la = _log_sigmoid(z) * (1.0 / GLA_NORMALIZER)
    seg = t_rows // SUBLANES
    pitch = seg + 4
    assert seg in (c, c // 2)
    sub = lax.broadcasted_iota(jnp.int32, (SUBLANES, LANES), 0)
    bcum_blocks = []
    for b, ref in enumerate(cum_refs):
        for sg in range(SUBLANES):
            ref[sg * pitch:sg * pitch + seg, :] = la[sg * seg:(sg + 1) * seg,
                                                     b * LANES:(b + 1) * LANES]
        run, sums = None, []
        for jj in range(seg):
            v = ref[pl.ds(jj, SUBLANES, stride=pitch), :]
            run = v if jj == 0 else run + v
            sums.append(run)
        if seg < c:
            carry_in = jnp.where(sub % 2 == 1, pltpu.roll(sums[-1], 1, 0), 0.0)
            sums = [v + carry_in for v in sums]
        for jj, v in enumerate(sums):
            ref[pl.ds(jj, SUBLANES, stride=pitch), :] = v
        bcum_blocks.append(jnp.concatenate(
            [ref[sg * pitch:sg * pitch + seg, :] for sg in range(SUBLANES)], axis=0))
    bcum = jnp.concatenate(bcum_blocks, axis=1)

    tril = (lax.broadcasted_iota(jnp.int32, (c, c), 0)
            >= lax.broadcasted_iota(jnp.int32, (c, c), 1))
    nt_dims = (((1,), (1,)), ((), ()))
    tn_dims = (((0,), (0,)), ((), ()))
    n_chunks = t_rows // c

    qe_c, g_c, o_intra, upd = [], [], [], []
    for ci in range(n_chunks):
        rows = slice(ci * c, (ci + 1) * c)
        bc = bcum[rows]
        b_last = bcum[(ci + 1) * c - 1:(ci + 1) * c]
        q = q_ref[rows, :].astype(F32) * (GLA_DK ** -0.5)
        k = k_ref[rows, :].astype(F32)
        qe = (q * jnp.exp(bc)).astype(BF16)
        ke = (k * jnp.exp(-bc)).astype(BF16)
        kd = (k * jnp.exp(b_last - bc)).astype(BF16)
        qe_c.append(qe)
        g_c.append(jnp.exp(b_last))
        o_h, upd_h = [], []
        for h in range(GLA_H):
            ks = slice(h * GLA_DK, (h + 1) * GLA_DK)
            v_h = v_ref[rows, h * GLA_DV:(h + 1) * GLA_DV]
            s = lax.dot_general(qe[:, ks], ke[:, ks], nt_dims, preferred_element_type=F32)
            s = jnp.where(tril, s, 0.0).astype(BF16)
            o_h.append(jnp.dot(s, v_h, preferred_element_type=F32))
            upd_h.append(lax.dot_general(v_h, kd[:, ks], tn_dims,
                                         preferred_element_type=F32))
        o_intra.append(o_h)
        upd.append(upd_h)

    for h in range(GLA_H):
        ks = slice(h * GLA_DK, (h + 1) * GLA_DK)
        vs = slice(h * GLA_DV, (h + 1) * GLA_DV)
        st = st_ref[h]
        for ci in range(n_chunks):
            rows = slice(ci * c, (ci + 1) * c)
            o = o_intra[ci][h] + lax.dot_general(qe_c[ci][:, ks], st.astype(BF16), nt_dims,
                                                 preferred_element_type=F32)
            st = st * g_c[ci][:, ks] + upd[ci][h]
            o = o * lax.rsqrt(jnp.mean(o * o, axis=-1, keepdims=True) + EPS) * ng_ref[...]
            o = o * _silu(r_ref[rows, vs].astype(F32))
            y_ref[rows, vs] = o.astype(y_ref.dtype)
        st_ref[h] = st


def _gla(proj, gk, w_gk2_p, b_gk, norm_g, layer, batch, seq, cast=None):
    t = min(GLA_T, seq)
    nt = seq // t
    assert cast is None or batch * nt >= CAST_UP_BLOCKS
    hk, hv = GLA_H * GLA_DK, GLA_H * GLA_DV
    vmem = (2 * (2 * t * hk * 2 + 3 * t * hv * 2 + t * GK_PAD * 4) + t * hk * 4
            + GLA_H * GLA_DV * GLA_DK * 4 + 8 * t * hk * 4)
    out_specs = pl.BlockSpec((t, hv), lambda b, s: (b * nt + s, 0))
    out_shape = jax.ShapeDtypeStruct((batch * seq, hv), BF16)
    args = [proj, proj, proj, proj, gk, w_gk2_p, b_gk, norm_g]
    cast_in = []
    if cast is not None:
        cast_in, cast_out, cast_shape, cast_vmem = _cast_payload(
            cast[0], cast[1], cast[2], lambda b, s: b * nt + s)
        out_specs, out_shape = [out_specs] + cast_out, [out_shape] + cast_shape
        args += [cast[0], cast[1]]
        vmem += cast_vmem
    return pl.pallas_call(
        functools.partial(_gla_kernel, cast=cast is not None),
        grid=(batch, nt),
        in_specs=[
            pl.BlockSpec((t, hk), lambda b, s: (b * nt + s, COL_Q)),
            pl.BlockSpec((t, hk), lambda b, s: (b * nt + s, COL_K)),
            pl.BlockSpec((t, hv), lambda b, s: (b * nt + s, COL_V)),
            pl.BlockSpec((t, hv), lambda b, s: (b * nt + s, COL_R)),
            pl.BlockSpec((t, GK_PAD), lambda b, s: (b * nt + s, 0)),
            _layer_resident((GK_PAD, hk), layer),
            _layer_resident((1, hk), layer),
            _layer_resident((1, GLA_DV), layer),
        ] + cast_in,
        out_specs=out_specs,
        out_shape=out_shape,
        scratch_shapes=[pltpu.VMEM((GLA_H, GLA_DV, GLA_DK), F32)]
        + [pltpu.VMEM((t + 4 * SUBLANES, LANES), F32)] * (hk // LANES),
        compiler_params=_params(("arbitrary", "arbitrary"), vmem + (8 << 20)),
        name="gla",
    )(*args)


def _merge_kernel(x_ref, ya_ref, yb_ref, gl_ref, bg_ref, wb_ref, wo_ref, post_g_ref, o_ref):
    d = x_ref.shape[1]
    z_a = jnp.dot(ya_ref[...], wb_ref[:W_LRU, :], preferred_element_type=F32)
    z_b = jnp.dot(yb_ref[...], wb_ref[W_LRU:, :], preferred_element_type=F32)
    g_a = jax.nn.sigmoid(gl_ref[:, :d].astype(F32) + bg_ref[:, :d])
    g_b = jax.nn.sigmoid(gl_ref[:, d:].astype(F32) + bg_ref[:, d:])
    merged = (g_a * z_a + g_b * z_b).astype(BF16)
    h = jnp.dot(merged, wo_ref[...], preferred_element_type=F32)
    o_ref[...] = x_ref[...] + _rms(h, post_g_ref[...])


def _merge(x2, y_a, y_b, proj, b_gate, w_branch, w_out, post_g, layer):
    m, d = x2.shape
    tm = min(MERGE_TM, m)
    wb_rows = w_branch.shape[1]
    vmem = (2 * 2 * tm * d * 4 + 2 * 2 * tm * W_LRU * 2 + 2 * tm * 2 * d * 2
            + wb_rows * d * 2 + d * d * 2 + 6 * tm * d * 4)
    return pl.pallas_call(
        _merge_kernel,
        grid=(m // tm,),
        in_specs=[
            pl.BlockSpec((tm, d), lambda i: (i, 0)),
            pl.BlockSpec((tm, W_LRU), lambda i: (i, 0)),
            pl.BlockSpec((tm, GLA_H * GLA_DV), lambda i: (i, 0)),
            pl.BlockSpec((tm, N_BRANCH * d), lambda i: (i, COL_GATE)),
            _layer_resident((1, N_BRANCH * d), layer),
            _layer_resident((wb_rows, d), layer),
            _layer_resident((d, d), layer),
            _layer_resident((1, d), layer),
        ],
        out_specs=pl.BlockSpec((tm, d), lambda i: (i, 0)),
        out_shape=jax.ShapeDtypeStruct((m, d), F32),
        compiler_params=_params(("arbitrary",), vmem),
        name="merge",
    )(x2, y_a, y_b, proj, b_gate, w_branch, w_out, post_g)


def _xattn_kernel(x_ref, pre_g_ref, wq_ref, kk_ref, vv_ref, wo_ref, post_g_ref, o_ref):
    xn = _rms(x_ref[...], pre_g_ref[...]).astype(BF16)
    q = (jnp.dot(xn, wq_ref[...], preferred_element_type=F32) * (XA_DH ** -0.5)).astype(BF16)
    nt_dims = (((1,), (1,)), ((), ()))
    outs = []
    for h in range(XA_H):
        hs = slice(h * XA_DH, (h + 1) * XA_DH)
        s = lax.dot_general(q[:, hs], kk_ref[:, hs], nt_dims,
                            preferred_element_type=F32)
        p = jnp.exp(s - jnp.max(s, axis=-1, keepdims=True))
        pv = jnp.dot(p.astype(BF16), vv_ref[:, hs], preferred_element_type=F32)
        outs.append(pv / jnp.sum(p, axis=-1, keepdims=True))
    o = jnp.concatenate(outs, axis=1).astype(BF16)
    h_out = jnp.dot(o, wo_ref[...], preferred_element_type=F32)
    o_ref[...] = x_ref[...] + _rms(h_out, post_g_ref[...])


def _xattn(x2, pre_g, w_q, kv, w_o, post_g, layer, batch, seq, mem_len):
    m, d = x2.shape
    tm = min(XA_TM, seq)
    nt = seq // tm
    hd = XA_H * XA_DH
    vmem = (2 * 2 * tm * d * 4 + 2 * d * hd * 2 + 2 * 2 * mem_len * hd * 2
            + 4 * tm * d * 4 + 8 * tm * mem_len * 4)
    return pl.pallas_call(
        _xattn_kernel,
        grid=(batch, nt),
        in_specs=[
            pl.BlockSpec((tm, d), lambda b, s: (b * nt + s, 0)),
            _layer_resident((1, d), layer),
            _layer_resident((d, hd), layer),
            pl.BlockSpec((mem_len, hd), lambda b, s: (b, 0)),
            pl.BlockSpec((mem_len, hd), lambda b, s: (b, 1)),
            _layer_resident((hd, d), layer),
            _layer_resident((1, d), layer),
        ],
        out_specs=pl.BlockSpec((tm, d), lambda b, s: (b * nt + s, 0)),
        out_shape=jax.ShapeDtypeStruct((m, d), F32),
        compiler_params=_params(("arbitrary", "arbitrary"), vmem),
        name="xattn",
    )(x2, pre_g, w_q, kv, kv, w_o, post_g)


def _pack_w_in(w_in):
    n_head = 2 * W_LRU + 2 * GLA_H * GLA_DK + 2 * GLA_H * GLA_DV
    head = w_in[..., :n_head]
    gate = w_in[..., n_head + GLA_RANK:]
    w_main = jnp.concatenate([gate, head], axis=-1).astype(BF16)
    w_gk = w_in[..., n_head:n_head + GK_PAD].astype(BF16)
    return w_main, w_gk


def _rows(v):
    return v.reshape(v.shape[0], 1, -1)


def kernel(x, mem, ffn1_pre_g, ffn1_post_g, ffn1_w_up, ffn1_w_down, mix_pre_g, mix_post_g, w_in,
           conv_w, conv_b, lru_w_a, lru_b_a, lru_w_i, lru_b_i, lru_lambda, gla_w_gk2, gla_b_gk,
           gla_norm_g, b_gate, w_branch, w_out, xa_pre_g, xa_post_g, mem_g, xa_w_q, xa_w_kv,
           xa_w_o, ffn2_pre_g, ffn2_post_g, ffn2_w_up, ffn2_w_down):
    batch, seq, d = x.shape
    mem_len = mem.shape[1]
    depth = ffn1_w_up.shape[0]
    x2 = x.reshape(batch * seq, d)
    mem2 = mem.reshape(batch * mem_len, d)

    ffn1_up, ffn1_down = ffn1_w_up[0].astype(BF16), ffn1_w_down[0].astype(BF16)
    w_main, w_gk = _pack_w_in(w_in)
    w_ai = jnp.concatenate([lru_w_a, lru_w_i], axis=-1).astype(BF16)
    w_gk2_p = jnp.pad(gla_w_gk2, ((0, 0), (0, GK_PAD - GLA_RANK), (0, 0)))
    wb, wo = w_branch.astype(BF16), w_out.astype(BF16)
    xa_q, xa_kv, xa_o = xa_w_q.astype(BF16), xa_w_kv.astype(BF16), xa_w_o.astype(BF16)
    ffn1_pre, ffn1_post = _rows(ffn1_pre_g), _rows(ffn1_post_g)
    ffn2_pre, ffn2_post = _rows(ffn2_pre_g), _rows(ffn2_post_g)
    mix_pre, mix_post = _rows(mix_pre_g), _rows(mix_post_g)
    xa_pre, xa_post, mem_gain = _rows(xa_pre_g), _rows(xa_post_g), _rows(mem_g)
    conv_bias, b_a, b_i, lam = _rows(conv_b), _rows(lru_b_a), _rows(lru_b_i), _rows(lru_lambda)
    b_gk, norm_g, b_gate_r = _rows(gla_b_gk), _rows(gla_norm_g), _rows(b_gate)

    for l in range(depth):
        x2 = _ffn(x2, ffn1_pre, ffn1_up, ffn1_down, ffn1_post, l)

        proj, gk = _norm_matmul(x2, mix_pre, w_main, l, w_side=w_gk, name="mix_in_proj")
        y_a, ffn2_up, ffn2_down = _lru(proj, conv_w, conv_bias, w_ai, b_a, b_i, lam, l, batch, seq,
                                       cast=(ffn2_w_up, ffn2_w_down, l))
        if l + 1 < depth:
            y_b, ffn1_up, ffn1_down = _gla(proj, gk, w_gk2_p, b_gk, norm_g, l, batch, seq,
                                           cast=(ffn1_w_up, ffn1_w_down, l + 1))
        else:
            y_b = _gla(proj, gk, w_gk2_p, b_gk, norm_g, l, batch, seq)
        x2 = _merge(x2, y_a, y_b, proj, b_gate_r, wb, wo, mix_post, l)

        kv = _norm_matmul(mem2, mem_gain, xa_kv, l, name="mem_kv_proj")
        x2 = _xattn(x2, xa_pre, xa_q, kv, xa_o, xa_post, l, batch, seq, mem_len)

        x2 = _ffn(x2, ffn2_pre, ffn2_up, ffn2_down, ffn2_post, l)

    return x2.reshape(batch, seq, d)
```
